```python
import jax, jax.numpy as jnp
from jax import lax
import numpy as np

D_MODEL = 1024
BATCH = 4
SEQ = 4096
DEPTH = 2
DEC_BATCH = 16
DEC_SEQ = 16
PAST_LEN = 1024

CHUNK = 64
Q_BLOCK = 128
ROPE_THETA = 10000.0
LN_EPS = 1e-5
RMS_EPS = 1e-6

H_A = 6
DH_A = 64
H_I = 8
D_I = 64
TOPK_MAX = 256

G_B = 4
DG_B = 64
W_B = G_B * DG_B
SGU_CHUNK = 128

H_C = 6
DN_C = 64
DR_C = 32
DV_C = 64
D_CQ = 256
D_C = 256
MLA_SCALE = (DN_C + DR_C) ** -0.5

D_MIX = H_A * DH_A + W_B + H_C * DV_C

N_EXPERTS = 32
TOP_K = 4
D_FF = 1024
SWIGLU_LIMIT = 7.0
SWIGLU_ALPHA = 1.702
MOE_BLOCK = 64

ALPHA = (2 * DEPTH) ** 0.25
BETA = (8 * DEPTH) ** -0.25

IN_SIZES = (H_A * DH_A, H_A * DH_A, H_A * DH_A, H_I * D_I, D_I, H_I, 2 * W_B, D_CQ, D_C, DR_C)
IN_COLS = sum(IN_SIZES)
V_A_START = 2 * H_A * DH_A
V_A_END = 3 * H_A * DH_A

kernel_name = "hybrid_dsa_gmlp_mla_moe_stream_step"


def _in_split_points():
    pts, acc = [], 0
    for s in IN_SIZES[:-1]:
        acc += s
        pts.append(acc)
    return pts


def layer_norm(x, g, b):
    xf = x.astype(jnp.float32)
    mu = jnp.mean(xf, axis=-1, keepdims=True)
    var = jnp.mean(jnp.square(xf - mu), axis=-1, keepdims=True)
    return ((xf - mu) * lax.rsqrt(var + LN_EPS) * g + b).astype(x.dtype)


def rms_norm(x, g):
    xf = x.astype(jnp.float32)
    return (xf * lax.rsqrt(jnp.mean(jnp.square(xf), axis=-1, keepdims=True) + RMS_EPS) * g).astype(x.dtype)


def rope(x, pos):
    half = x.shape[-1] // 2
    inv = ROPE_THETA ** (-jnp.arange(half, dtype=jnp.float32) / half)
    ang = pos.astype(jnp.float32)[:, None] * inv[None, :]
    cos = jnp.cos(ang)[:, None, :]
    sin = jnp.sin(ang)[:, None, :]
    x1 = x[..., :half].astype(jnp.float32)
    x2 = x[..., half:].astype(jnp.float32)
    return jnp.concatenate([x1 * cos - x2 * sin, x2 * cos + x1 * sin], axis=-1).astype(x.dtype)


def chunk_allowed(q_pos, k_pos):
    return (k_pos[None, :] // CHUNK) <= (q_pos[:, None] // CHUNK)


def dsa_block(q, qi, wi, q_pos, k, v, ki, k_pos, top_k):
    f32 = jnp.float32
    dots = jnp.einsum('bthd,bsd->bths', qi.astype(f32), ki.astype(f32)) * (D_I ** -0.5)
    score = jnp.einsum('bth,bths->bts', wi.astype(f32), jax.nn.relu(dots))
    score = jnp.where(chunk_allowed(q_pos, k_pos)[None], score, -jnp.inf)
    _, idx = lax.top_k(score, top_k)
    valid = (k_pos[idx] // CHUNK) <= (q_pos[None, :, None] // CHUNK)
    gather = jax.vmap(lambda arr, ii: arr[ii])
    k_sel = gather(k, idx)
    v_sel = gather(v, idx)
    logits = jnp.einsum('bthd,btkhd->bthk', q.astype(f32), k_sel.astype(f32)) * (DH_A ** -0.5)
    p = jax.nn.softmax(jnp.where(valid[:, :, None, :], logits, -jnp.inf), axis=-1)
    return jnp.einsum('bthk,btkhd->bthd', p, v_sel.astype(f32)).astype(q.dtype)


def mla_project(c_q, c_kv, kr_raw, pos, q_norm, kv_norm, w_uq):
    bsz, n, _ = c_q.shape
    q = (rms_norm(c_q, q_norm) @ w_uq).reshape(bsz, n, H_C, DN_C + DR_C)
    q_nope = q[..., :DN_C]
    q_rope = rope(q[..., DN_C:], pos)
    lat = rms_norm(c_kv, kv_norm)
    k_rope = rope(kr_raw[:, :, None, :], pos)[:, :, 0, :]
    return q_nope, q_rope, lat, k_rope


def mla_block(q_nope, q_rope, q_pos, k_nope, k_rope, v, k_pos):
    f32 = jnp.float32
    logits = (jnp.einsum('bthd,bshd->bhts', q_nope.astype(f32), k_nope.astype(f32))
              + jnp.einsum('bthr,bsr->bhts', q_rope.astype(f32), k_rope.astype(f32))) * MLA_SCALE
    p = jax.nn.softmax(jnp.where(chunk_allowed(q_pos, k_pos)[None, None], logits, -jnp.inf), axis=-1)
    return jnp.einsum('bhts,bshd->bthd', p, v.astype(f32)).astype(q_nope.dtype)


def sgu_mix(u, v, w_s, b_s, first_rows):
    bsz, n, _ = v.shape
    i = jnp.arange(SGU_CHUNK)
    w_m = jnp.where((i[None, :] // CHUNK) <= (i[:, None] // CHUNK), w_s, 0.0)
    if first_rows:
        vg = v.reshape(bsz, n, G_B, DG_B)
        mixed = jnp.einsum('gij,bjgd->bigd', w_m[:, :n, :n], vg) + b_s[:, :n].T[None, :, :, None]
    else:
        vg = v.reshape(bsz, n // SGU_CHUNK, SGU_CHUNK, G_B, DG_B)
        mixed = jnp.einsum('gij,bcjgd->bcigd', w_m, vg) + b_s.T[None, None, :, :, None]
    return u * mixed.reshape(bsz, n, W_B)


def moe(x, router_w, router_b, w_gu, b_gu, w_dn, b_dn):
    bsz, n, d = x.shape
    xf = x.reshape(-1, d)
    n_tok = xf.shape[0]
    logits = (xf @ router_w + router_b).astype(jnp.float32)
    top_val, top_idx = lax.top_k(logits, TOP_K)
    gates = jax.nn.softmax(top_val, axis=-1).astype(x.dtype)
    n_assign = n_tok * TOP_K
    flat_e = top_idx.reshape(-1)
    order = jnp.argsort(flat_e)
    sorted_e = flat_e[order]
    token_of = order // TOP_K
    counts = jnp.zeros((N_EXPERTS,), jnp.int32).at[flat_e].add(1)
    padded = ((counts + MOE_BLOCK - 1) // MOE_BLOCK) * MOE_BLOCK
    pad_end = jnp.cumsum(padded)
    pad_start = pad_end - padded
    start = jnp.cumsum(counts) - counts
    dest = pad_start[sorted_e] + (jnp.arange(n_assign) - start[sorted_e])
    n_blocks = -(-(n_assign + N_EXPERTS * (MOE_BLOCK - 1)) // MOE_BLOCK)
    rows = jnp.zeros((n_blocks * MOE_BLOCK, d), x.dtype).at[dest].set(xf[token_of])
    block_e = jnp.minimum(jnp.searchsorted(pad_end, jnp.arange(n_blocks) * MOE_BLOCK, side='right'), N_EXPERTS - 1)

    def expert_rows(args):
        xb, e = args
        h = xb @ w_gu[e] + b_gu[e]
        gate = jnp.minimum(h[:, :D_FF], SWIGLU_LIMIT)
        lin = jnp.clip(h[:, D_FF:], -SWIGLU_LIMIT, SWIGLU_LIMIT)
        act = (lin + 1.0) * (gate * jax.nn.sigmoid(SWIGLU_ALPHA * gate))
        return act @ w_dn[e] + b_dn[e]

    y_rows = lax.map(expert_rows, (rows.reshape(n_blocks, MOE_BLOCK, d), block_e)).reshape(-1, d)
    y_assign = y_rows[dest] * gates.reshape(-1)[order][:, None]
    return jax.ops.segment_sum(y_assign, token_of, num_segments=n_tok).reshape(bsz, n, d)


def trunk_layer(x, pos, p, cache):
    bsz, n, _ = x.shape
    q_a, k_a, v_a, q_i, k_i, w_i, z_b, c_q, c_kv, kr_c = jnp.split(x @ p['w_in'], _in_split_points(), axis=-1)
    q_a = rope(q_a.reshape(bsz, n, H_A, DH_A), pos)
    k_a = rope(k_a.reshape(bsz, n, H_A, DH_A), pos)
    v_a = v_a.reshape(bsz, n, H_A, DH_A)
    q_i = rope(q_i.reshape(bsz, n, H_I, D_I), pos)
    k_i = rope(k_i[:, :, None, :], pos)[:, :, 0, :]
    w_i = w_i * (H_I ** -0.5)
    q_nope, q_rope, lat, k_rope = mla_project(c_q, c_kv, kr_c, pos, p['q_norm'], p['kv_norm'], p['w_uq'])
    if cache is None:
        ka_all, va_all, ki_all, lat_all, kr_all = k_a, v_a, k_i, lat, k_rope
    else:
        ca_k, ca_v, ca_ki, cc_lat, cc_kr = cache
        ka_all = jnp.concatenate([ca_k, k_a], axis=1)
        va_all = jnp.concatenate([ca_v, v_a], axis=1)
        ki_all = jnp.concatenate([ca_ki, k_i], axis=1)
        lat_all = jnp.concatenate([cc_lat, lat], axis=1)
        kr_all = jnp.concatenate([cc_kr, k_rope], axis=1)
    n_keys = ka_all.shape[1]
    k_pos = jnp.arange(n_keys)
    top_k = min(TOPK_MAX, n_keys // 4)
    kn_all = (lat_all @ p['w_uk']).reshape(bsz, n_keys, H_C, DN_C)
    vc_all = (lat_all @ p['w_uv']).reshape(bsz, n_keys, H_C, DV_C)

    def attend(start, length):
        sl = lambda a: lax.dynamic_slice_in_dim(a, start, length, axis=1)
        qp = lax.dynamic_slice_in_dim(pos, start, length)
        o_a = dsa_block(sl(q_a), sl(q_i), sl(w_i), qp, ka_all, va_all, ki_all, k_pos, top_k)
        o_c = mla_block(sl(q_nope), sl(q_rope), qp, kn_all, kr_all, vc_all, k_pos)
        return jnp.concatenate([o_a.reshape(bsz, length, H_A * DH_A), o_c.reshape(bsz, length, H_C * DV_C)], axis=-1)

    if cache is None:
        blocks = lax.map(lambda i: attend(i * Q_BLOCK, Q_BLOCK), jnp.arange(n // Q_BLOCK))
        o_ac = jnp.moveaxis(blocks, 0, 1).reshape(bsz, n, H_A * DH_A + H_C * DV_C)
    else:
        o_ac = attend(0, n)
    o_a, o_c = o_ac[..., :H_A * DH_A], o_ac[..., H_A * DH_A:]
    z_b = jax.nn.gelu(z_b)
    u_b = z_b[..., :W_B]
    v_b = layer_norm(z_b[..., W_B:], p['sgu_g'], p['sgu_b'])
    o_b = sgu_mix(u_b, v_b, p['w_s'], p['b_s'], first_rows=cache is not None)
    mix = jnp.concatenate([o_a, o_b, o_c], axis=-1) @ p['w_out']
    x = layer_norm(ALPHA * x + mix, p['ln1_g'], p['ln1_b'])
    x = layer_norm(ALPHA * x + moe(x, p['router_w'], p['router_b'], p['w_gu'], p['b_gu'], p['w_dn'], p['b_dn']),
                   p['ln2_g'], p['ln2_b'])
    return x, (k_a, v_a, k_i, lat, k_rope, v_b)


def setup_inputs(seed: int = 0) -> dict:
    key = jax.random.key(seed)
    ks = iter(jax.random.split(key, 32))

    def nrm(shape, scale):
        return jax.random.normal(next(ks), shape, jnp.float32) * scale

    def gain(shape):
        return 1.0 + nrm(shape, 0.02)

    col_scale = jnp.ones((IN_COLS,), jnp.float32).at[V_A_START:V_A_END].set(BETA)
    return {
        'x_prompt': nrm((BATCH, SEQ, D_MODEL), 1.0),
        'x_sample': nrm((DEC_BATCH, DEC_SEQ, D_MODEL), 1.0),
        'cache_a_k': nrm((DEPTH, DEC_BATCH, PAST_LEN, H_A, DH_A), 1.0),
        'cache_a_v': nrm((DEPTH, DEC_BATCH, PAST_LEN, H_A, DH_A), BETA),
        'cache_a_kidx': nrm((DEPTH, DEC_BATCH, PAST_LEN, D_I), 1.0),
        'cache_c_latent': nrm((DEPTH, DEC_BATCH, PAST_LEN, D_C), 1.0),
        'cache_c_krope': nrm((DEPTH, DEC_BATCH, PAST_LEN, DR_C), 1.0),
        'w_in': nrm((DEPTH, D_MODEL, IN_COLS), D_MODEL ** -0.5) * col_scale,
        'mla_q_norm': gain((DEPTH, D_CQ)),
        'mla_kv_norm': gain((DEPTH, D_C)),
        'w_uq': nrm((DEPTH, D_CQ, H_C * (DN_C + DR_C)), D_CQ ** -0.5),
        'w_uk': nrm((DEPTH, D_C, H_C * DN_C), D_C ** -0.5),
        'w_uv': nrm((DEPTH, D_C, H_C * DV_C), BETA * D_C ** -0.5),
        'sgu_ln_g': gain((DEPTH, W_B)),
        'sgu_ln_b': nrm((DEPTH, W_B), 0.02),
        'w_spatial': nrm((DEPTH, G_B, SGU_CHUNK, SGU_CHUNK), SGU_CHUNK ** -0.5),
        'b_spatial': gain((DEPTH, G_B, SGU_CHUNK)),
        'w_out': nrm((DEPTH, D_MIX, D_MODEL), BETA * D_MIX ** -0.5),
        'ln1_g': gain((DEPTH, D_MODEL)),
        'ln1_b': nrm((DEPTH, D_MODEL), 0.02),
        'router_w': nrm((DEPTH, D_MODEL, N_EXPERTS), D_MODEL ** -0.5),
        'router_b': nrm((DEPTH, N_EXPERTS), 0.01),
        'w_gate_up': nrm((DEPTH, N_EXPERTS, D_MODEL, 2 * D_FF), D_MODEL ** -0.5),
        'b_gate_up': nrm((DEPTH, N_EXPERTS, 2 * D_FF), 0.02),
        'w_down': nrm((DEPTH, N_EXPERTS, D_FF, D_MODEL), BETA * D_FF ** -0.5),
        'b_down': nrm((DEPTH, N_EXPERTS, D_MODEL), 0.02),
        'ln2_g': gain((DEPTH, D_MODEL)),
        'ln2_b': nrm((DEPTH, D_MODEL), 0.02),
    }


def reference(x_prompt, x_sample, cache_a_k, cache_a_v, cache_a_kidx, cache_c_latent, cache_c_krope,
              w_in, mla_q_norm, mla_kv_norm, w_uq, w_uk, w_uv, sgu_ln_g, sgu_ln_b, w_spatial, b_spatial,
              w_out, ln1_g, ln1_b, router_w, router_b, w_gate_up, b_gate_up, w_down, b_down, ln2_g, ln2_b):
    pos_p = jnp.arange(x_prompt.shape[1])
    pos_s = cache_a_k.shape[2] + jnp.arange(x_sample.shape[1])
    yp, ys = x_prompt, x_sample
    ak_p, av_p, aki_p, cl_p, ckr_p = [], [], [], [], []
    ak_s, av_s, aki_s, cl_s, ckr_s, bv_s = [], [], [], [], [], []
    for l in range(DEPTH):
        p = {'w_in': w_in[l], 'q_norm': mla_q_norm[l], 'kv_norm': mla_kv_norm[l], 'w_uq': w_uq[l],
             'w_uk': w_uk[l], 'w_uv': w_uv[l], 'sgu_g': sgu_ln_g[l], 'sgu_b': sgu_ln_b[l],
             'w_s': w_spatial[l], 'b_s': b_spatial[l], 'w_out': w_out[l], 'ln1_g': ln1_g[l], 'ln1_b': ln1_b[l],
             'router_w': router_w[l], 'router_b': router_b[l], 'w_gu': w_gate_up[l], 'b_gu': b_gate_up[l],
             'w_dn': w_down[l], 'b_dn': b_down[l], 'ln2_g': ln2_g[l], 'ln2_b': ln2_b[l]}
        yp, (k1, v1, ki1, lat1, kr1, _) = trunk_layer(yp, pos_p, p, None)
        ys, (k2, v2, ki2, lat2, kr2, vb2) = trunk_layer(
            ys, pos_s, p, (cache_a_k[l], cache_a_v[l], cache_a_kidx[l], cache_c_latent[l], cache_c_krope[l]))
        ak_p.append(k1); av_p.append(v1); aki_p.append(ki1); cl_p.append(lat1); ckr_p.append(kr1)
        ak_s.append(k2); av_s.append(v2); aki_s.append(ki2); cl_s.append(lat2); ckr_s.append(kr2); bv_s.append(vb2)
    return (yp, ys,
            jnp.stack(ak_p), jnp.stack(av_p), jnp.stack(aki_p), jnp.stack(cl_p), jnp.stack(ckr_p),
            jnp.stack(ak_s), jnp.stack(av_s), jnp.stack(aki_s), jnp.stack(cl_s), jnp.stack(ckr_s), jnp.stack(bv_s))
```

```python
import functools

import numpy as np
import jax
import jax.numpy as jnp
from jax import lax
from jax.experimental import pallas as pl
from jax.experimental.pallas import tpu as pltpu

F32 = jnp.float32
BF16 = jnp.bfloat16
I32 = jnp.int32

D_MODEL = 1024
DEPTH = 2
CHUNK = 64
ROPE_THETA = 10000.0
LN_EPS = 1e-5
RMS_EPS = 1e-6
H_A, DH_A = 6, 64
H_I, D_I = 8, 64
TOPK_MAX = 256
G_B, DG_B = 4, 64
W_B = G_B * DG_B
SGU_CHUNK = 128
H_C, DN_C, DR_C, DV_C = 6, 64, 32, 64
D_CQ, D_C = 256, 256
MLA_SCALE = (DN_C + DR_C) ** -0.5
D_A = H_A * DH_A
D_VC = H_C * DV_C
N_EXPERTS = 32
TOP_K = 4
D_FF = 1024
SWIGLU_LIMIT = 7.0
SWIGLU_ALPHA = 1.702
ALPHA = (2 * DEPTH) ** 0.25

LANES = 128
ROW_TILE = 256
MOE_TILE = 256
VMEM_LIMIT = 48 * 1024 * 1024

C_QA, C_KA, C_VA, C_QI, C_KIW, C_ZU, C_ZV, C_CQ, C_CKV, C_KR = (
    0, 384, 768, 1152, 1664, 1792, 2048, 2304, 2560, 2816)
IN_PACKED = 2944
QC_SLOT = 128
D_QC = H_C * QC_SLOT

NEG_BIG = -1e30
KEY_NEG_INF = -2139095041
INT_MIN = -2147483648


def _nt_dot(a, b):
    return lax.dot_general(a, b, (((1,), (1,)), ((), ())), preferred_element_type=F32)


def _rope128(x, c, sa, sb, half):
    return x * c + pltpu.roll(x, LANES - half, 1) * sa + pltpu.roll(x, half, 1) * sb


def _proj_kernel(x_ref, w_ref, c64_ref, sa64_ref, sb64_ref, cq_ref, saq_ref, sbq_ref,
                 ck_ref, sak_ref, sbk_ref, qn_ref, kvn_ref, wuq_ref, wkc_ref, wuv_ref,
                 sgug_ref, sgub_ref, bd_ref, bsb_ref,
                 qa_o, ka_o, kab_o, va_o, vab_o, qi_o, ki_o, kib_o, wi_o, ob_o, vb_o,
                 qc_o, lat_o, kr_o, kc_o, vc_o):
    h = jnp.dot(x_ref[...].astype(BF16), w_ref[...], preferred_element_type=F32)
    c64, sa64, sb64 = c64_ref[...], sa64_ref[...], sb64_ref[...]

    def rope64(col):
        return _rope128(h[:, col:col + LANES], c64, sa64, sb64, 32)

    for c in range(D_A // LANES):
        qa_o[:, c * LANES:(c + 1) * LANES] = (rope64(C_QA + c * LANES) * (DH_A ** -0.5)).astype(BF16)
        ka = rope64(C_KA + c * LANES)
        ka_o[:, c * LANES:(c + 1) * LANES] = ka
        kab_o[:, c * LANES:(c + 1) * LANES] = ka.astype(BF16)
    va = h[:, C_VA:C_VA + D_A]
    va_o[...] = va
    vab_o[...] = va.astype(BF16)
    for c in range(H_I * D_I // LANES):
        qi_o[:, c * LANES:(c + 1) * LANES] = (rope64(C_QI + c * LANES) * (D_I ** -0.5)).astype(BF16)
    kiw = rope64(C_KIW)
    ki_o[...] = kiw[:, :D_I]
    kib_o[...] = kiw[:, :D_I].astype(BF16)
    wi_o[...] = h[:, C_KIW:C_KIW + LANES] * (H_I ** -0.5)

    z = h[:, C_ZU:C_ZU + 2 * W_B]
    z = 0.5 * z * (1.0 + jnp.tanh(np.sqrt(2.0 / np.pi) * (z + 0.044715 * (z * z * z))))
    u = z[:, :W_B]
    v = z[:, W_B:]
    mu = jnp.mean(v, axis=-1, keepdims=True)
    var = jnp.mean(jnp.square(v - mu), axis=-1, keepdims=True)
    v = (v - mu) * lax.rsqrt(var + LN_EPS) * sgug_ref[...] + sgub_ref[...]
    vb_o[...] = v
    v16 = v.astype(BF16)
    lane = lax.broadcasted_iota(I32, (v.shape[0], LANES), 1)
    for w in range(W_B // LANES):
        vw = v16[:, w * LANES:(w + 1) * LANES]
        m0 = jnp.dot(bd_ref[0, 2 * w], vw, preferred_element_type=F32)
        m1 = jnp.dot(bd_ref[0, 2 * w + 1], vw, preferred_element_type=F32)
        mixed = jnp.where(lane < DG_B, m0, m1) + bsb_ref[0, :, w * LANES:(w + 1) * LANES]
        ob_o[:, w * LANES:(w + 1) * LANES] = (u[:, w * LANES:(w + 1) * LANES] * mixed).astype(BF16)

    cq = h[:, C_CQ:C_CQ + D_CQ]
    cq = cq * lax.rsqrt(jnp.mean(jnp.square(cq), axis=-1, keepdims=True) + RMS_EPS) * qn_ref[...]
    q = jnp.dot(cq.astype(BF16), wuq_ref[...], preferred_element_type=F32)
    cqt, saq, sbq = cq_ref[...], saq_ref[...], sbq_ref[...]
    for hh in range(H_C):
        qs = _rope128(q[:, hh * QC_SLOT:(hh + 1) * QC_SLOT], cqt, saq, sbq, DR_C // 2)
        qc_o[:, hh * QC_SLOT:(hh + 1) * QC_SLOT] = (qs * MLA_SCALE).astype(BF16)
    ckv = h[:, C_CKV:C_CKV + D_C]
    lat = ckv * lax.rsqrt(jnp.mean(jnp.square(ckv), axis=-1, keepdims=True) + RMS_EPS) * kvn_ref[...]
    lat_o[...] = lat
    kr = _rope128(h[:, C_KR:C_KR + LANES], ck_ref[...], sak_ref[...], sbk_ref[...], DR_C // 2)
    kr_o[...] = kr[:, :DR_C]
    lat16 = lat.astype(BF16)
    kc = (jnp.dot(lat16, wkc_ref[:D_C, :], preferred_element_type=F32)
          + jnp.dot(kr.astype(BF16), wkc_ref[D_C:, :], preferred_element_type=F32))
    kc_o[...] = kc.astype(BF16)
    vc_o[...] = jnp.dot(lat16, wuv_ref[...], preferred_element_type=F32).astype(BF16)


def _proj_call(x, w_packed, tabs, qn, kvn, wuq, wkc, wuv, sgug, sgub, bd, bsb, n_prompt_tiles, tab_period):
    n = x.shape[0]
    tm = ROW_TILE
    grid = (n // tm,)

    def row(i):
        return (i, 0)

    def const2(i):
        return (0, 0)

    def tab(i):
        return (jnp.where(i < n_prompt_tiles, i % tab_period, tab_period), 0)

    def grp(i):
        return (jnp.where(i < n_prompt_tiles, 0, 1), 0, 0, 0)

    def grp3(i):
        return (jnp.where(i < n_prompt_tiles, 0, 1), 0, 0)

    in_specs = [pl.BlockSpec((tm, D_MODEL), row), pl.BlockSpec((D_MODEL, IN_PACKED), const2)]
    in_specs += [pl.BlockSpec((tm, LANES), tab)] * 9
    in_specs += [pl.BlockSpec((1, D_CQ), const2), pl.BlockSpec((1, D_C), const2),
                 pl.BlockSpec((D_CQ, D_QC), const2), pl.BlockSpec((D_C + LANES, D_QC), const2),
                 pl.BlockSpec((D_C, D_VC), const2),
                 pl.BlockSpec((1, W_B), const2), pl.BlockSpec((1, W_B), const2),
                 pl.BlockSpec((1, G_B, tm, tm), grp), pl.BlockSpec((1, tm, W_B), grp3)]
    outs = [(D_A, BF16), (D_A, F32), (D_A, BF16), (D_A, F32), (D_A, BF16), (H_I * D_I, BF16),
            (D_I, F32), (D_I, BF16), (LANES, F32), (W_B, BF16), (W_B, F32),
            (D_QC, BF16), (D_C, F32), (DR_C, F32), (D_QC, BF16), (D_VC, BF16)]
    out_shape = [jax.ShapeDtypeStruct((n, c), d) for c, d in outs]
    out_specs = [pl.BlockSpec((tm, c), row) for c, _ in outs]
    return pl.pallas_call(
        _proj_kernel, grid=grid, in_specs=in_specs, out_specs=out_specs, out_shape=out_shape,
        compiler_params=pltpu.CompilerParams(dimension_semantics=("arbitrary",), vmem_limit_bytes=VMEM_LIMIT),
        name="proj",
    )(x, w_packed, *tabs, qn, kvn, wuq, wkc, wuv, sgug, sgub, bd, bsb)


def _kvup_kernel(lat_ref, kr_ref, wkc_ref, wuv_ref, kc_o, vc_o):
    lat16 = lat_ref[...].astype(BF16)
    kc = (jnp.dot(lat16, wkc_ref[:D_C, :], preferred_element_type=F32)
          + jnp.dot(kr_ref[...].astype(BF16), wkc_ref[D_C:D_C + DR_C, :], preferred_element_type=F32))
    kc_o[...] = kc.astype(BF16)
    vc_o[...] = jnp.dot(lat16, wuv_ref[...], preferred_element_type=F32).astype(BF16)


def _kvup_call(lat, kr, wkc, wuv):
    n = lat.shape[0]
    tm = 1024
    return pl.pallas_call(
        _kvup_kernel, grid=(n // tm,),
        in_specs=[pl.BlockSpec((tm, D_C), lambda i: (i, 0)), pl.BlockSpec((tm, DR_C), lambda i: (i, 0)),
                  pl.BlockSpec((D_C + LANES, D_QC), lambda i: (0, 0)), pl.BlockSpec((D_C, D_VC), lambda i: (0, 0))],
        out_specs=[pl.BlockSpec((tm, D_QC), lambda i: (i, 0)), pl.BlockSpec((tm, D_VC), lambda i: (i, 0))],
        out_shape=[jax.ShapeDtypeStruct((n, D_QC), BF16), jax.ShapeDtypeStruct((n, D_VC), BF16)],
        compiler_params=pltpu.CompilerParams(dimension_semantics=("arbitrary",), vmem_limit_bytes=VMEM_LIMIT),
        name="kvup",
    )(lat, kr, wkc, wuv)


def _key_bounds(j, tq, tk, n_valid, q_pos0):
    q_first = q_pos0 + j * tq
    kmax = jnp.minimum(((q_first + tq - 1) // CHUNK + 1) * CHUNK, n_valid)
    n_tiles = (kmax + tk - 1) // tk
    qpos = q_first + lax.broadcasted_iota(I32, (tq, 1), 0)
    bound = jnp.minimum((qpos // CHUNK + 1) * CHUNK, n_valid)
    return n_tiles, bound


def _dsa_kernel(qi_ref, wi_ref, qa_ref, ki_ref, ka_ref, va_ref, o_ref, key_ref, *,
                tq, tk, n_valid, q_pos0, top_k):
    j = pl.program_id(1)
    n_tiles, bound = _key_bounds(j, tq, tk, n_valid, q_pos0)
    lane_tk = lax.broadcasted_iota(I32, (tq, tk), 1)
    lane = lax.broadcasted_iota(I32, (tq, LANES), 1)
    n_sub = tk // LANES

    qi = qi_ref[...]
    wi = wi_ref[...]
    qis = [qi[:, hh * D_I:(hh + 1) * D_I] for hh in range(H_I)]
    wis = [wi[:, D_I + hh:D_I + hh + 1] for hh in range(H_I)]

    def score_tile(kt, carry):
        k0 = pl.multiple_of(kt * tk, tk)
        ki_t = ki_ref[pl.ds(k0, tk), :]
        s = jnp.zeros((tq, tk), F32)
        for hh in range(H_I):
            s = s + wis[hh] * jnp.maximum(_nt_dot(qis[hh], ki_t), 0.0)
        bits = pltpu.bitcast(s, I32)
        key = jnp.where(bits < 0, bits ^ 0x7FFFFFFF, bits)
        key = jnp.where(key == -1, 0, key)
        key = jnp.where(k0 + lane_tk < bound, key, KEY_NEG_INF)
        key_ref[kt] = key
        return carry

    lax.fori_loop(0, n_tiles, score_tile, 0)

    def count(pred):
        def body(kt, acc):
            kk = key_ref[kt]
            for c in range(n_sub):
                acc = acc + jnp.where(pred(kk[:, c * LANES:(c + 1) * LANES], kt * tk + c * LANES), 1.0, 0.0)
            return acc
        acc = lax.fori_loop(0, n_tiles, body, jnp.zeros((tq, LANES), F32))
        return jnp.sum(acc, axis=1, keepdims=True)

    kf = float(top_k)
    c0 = count(lambda kk, base: kk >= 0)
    t0 = jnp.where(c0 >= kf, 0, INT_MIN).astype(I32)

    def bit_body(it, t):
        cand = t | lax.shift_left(jnp.int32(1), 30 - it)
        cnt = count(lambda kk, base: kk >= cand)
        return jnp.where(cnt >= kf, cand, t)

    t = lax.fori_loop(0, 31, bit_body, t0)

    c_gt = count(lambda kk, base: kk > t)
    c_eq = count(lambda kk, base: kk == t)
    need = kf - c_gt
    excess = jnp.max(jnp.where((c_eq > need) & (t > KEY_NEG_INF), 1.0, 0.0)) > 0.5

    def tie_path():
        def jb(it, jj):
            cand = jj | lax.shift_left(jnp.int32(1), 12 - it)
            f = count(lambda kk, base: (kk == t) & (base + lane < cand))
            return jnp.where(f < need, cand, jj)
        return lax.fori_loop(0, 13, jb, jnp.zeros((tq, 1), I32))

    j_last = lax.cond(excess, tie_path, lambda: jnp.full((tq, 1), 1 << 30, I32))

    def bias_tile(kt, carry):
        kk = key_ref[kt]
        kpos = kt * tk + lane_tk
        sel = ((kk > t) | ((kk == t) & (kpos <= j_last))) & (kk != KEY_NEG_INF)
        key_ref[kt] = pltpu.bitcast(jnp.where(sel, 0.0, NEG_BIG).astype(F32), I32)
        return carry

    lax.fori_loop(0, n_tiles, bias_tile, 0)

    qa = qa_ref[...]
    half_masks = [jnp.where(lane < DH_A, 1.0, 0.0).astype(BF16), jnp.where(lane >= DH_A, 1.0, 0.0).astype(BF16)]
    for w in range(D_A // LANES):
        qw = qa[:, w * LANES:(w + 1) * LANES]
        outs = []
        for hf in range(2):
            qh = qw * half_masks[hf]

            def body(kt, carry, qh=qh, w=w):
                m, l, acc = carry
                k0 = pl.multiple_of(kt * tk, tk)
                k_t = ka_ref[pl.ds(k0, tk), w * LANES:(w + 1) * LANES]
                v_t = va_ref[pl.ds(k0, tk), w * LANES:(w + 1) * LANES]
                s = _nt_dot(qh, k_t) + pltpu.bitcast(key_ref[kt], F32)
                m_new = jnp.maximum(m, jnp.max(s, axis=1, keepdims=True))
                p = jnp.exp(s - m_new)
                a = jnp.exp(m - m_new)
                l = a * l + jnp.sum(p, axis=1, keepdims=True)
                acc = a * acc + jnp.dot(p.astype(BF16), v_t, preferred_element_type=F32)
                return m_new, l, acc

            m, l, acc = lax.fori_loop(
                0, n_tiles, body,
                (jnp.full((tq, 1), NEG_BIG, F32), jnp.zeros((tq, 1), F32), jnp.zeros((tq, LANES), F32)))
            outs.append(acc / l)
        o_ref[:, w * LANES:(w + 1) * LANES] = jnp.where(lane < DH_A, outs[0], outs[1]).astype(o_ref.dtype)


def _dsa_call(qi, wi, qa, ki, ka, va, *, n_batch, n_q, q_row0, k_rows, tq, tk, n_valid, q_pos0, top_k):
    nq_blocks = n_q // tq
    qb0 = q_row0 // tq
    kern = functools.partial(_dsa_kernel, tq=tq, tk=tk, n_valid=n_valid, q_pos0=q_pos0, top_k=top_k)

    def qmap(b, j):
        return (qb0 + b * nq_blocks + j, 0)

    def kmap(b, j):
        return (b, 0)

    def omap(b, j):
        return (b * nq_blocks + j, 0)

    return pl.pallas_call(
        kern, grid=(n_batch, nq_blocks),
        in_specs=[pl.BlockSpec((tq, H_I * D_I), qmap), pl.BlockSpec((tq, LANES), qmap), pl.BlockSpec((tq, D_A), qmap),
                  pl.BlockSpec((k_rows, D_I), kmap), pl.BlockSpec((k_rows, D_A), kmap), pl.BlockSpec((k_rows, D_A), kmap)],
        out_specs=pl.BlockSpec((tq, D_A), omap),
        out_shape=jax.ShapeDtypeStruct((n_batch * n_q, D_A), BF16),
        scratch_shapes=[pltpu.VMEM((k_rows // tk, tq, tk), I32)],
        compiler_params=pltpu.CompilerParams(dimension_semantics=("arbitrary", "arbitrary"),
                                             vmem_limit_bytes=VMEM_LIMIT),
        name="dsa",
    )(qi, wi, qa, ki, ka, va)


def _mla_kernel(qc_ref, kc_ref, vc_ref, o_ref, *, tq, tk, n_valid, q_pos0):
    j = pl.program_id(1)
    n_tiles, bound = _key_bounds(j, tq, tk, n_valid, q_pos0)
    lane_tk = lax.broadcasted_iota(I32, (tq, tk), 1)
    lane = lax.broadcasted_iota(I32, (tq, LANES), 1)
    qc = qc_ref[...]
    for w in range(D_VC // LANES):
        outs = []
        for hf in range(2):
            hh = 2 * w + hf
            qh = qc[:, hh * QC_SLOT:(hh + 1) * QC_SLOT]

            def body(kt, carry, qh=qh, hh=hh, w=w):
                m, l, acc = carry
                k0 = pl.multiple_of(kt * tk, tk)
                k_t = kc_ref[pl.ds(k0, tk), hh * QC_SLOT:(hh + 1) * QC_SLOT]
                v_t = vc_ref[pl.ds(k0, tk), w * LANES:(w + 1) * LANES]
                s = _nt_dot(qh, k_t) + jnp.where(k0 + lane_tk < bound, 0.0, NEG_BIG)
                m_new = jnp.maximum(m, jnp.max(s, axis=1, keepdims=True))
                p = jnp.exp(s - m_new)
                a = jnp.exp(m - m_new)
                l = a * l + jnp.sum(p, axis=1, keepdims=True)
                acc = a * acc + jnp.dot(p.astype(BF16), v_t, preferred_element_type=F32)
                return m_new, l, acc

            m, l, acc = lax.fori_loop(
                0, n_tiles, body,
                (jnp.full((tq, 1), NEG_BIG, F32), jnp.zeros((tq, 1), F32), jnp.zeros((tq, LANES), F32)))
            outs.append(acc / l)
        o_ref[:, w * LANES:(w + 1) * LANES] = jnp.where(lane < DV_C, outs[0], outs[1]).astype(o_ref.dtype)


def _mla_call(qc, kc, vc, *, n_batch, n_q, q_row0, k_rows, tq, tk, n_valid, q_pos0):
    nq_blocks = n_q // tq
    qb0 = q_row0 // tq
    kern = functools.partial(_mla_kernel, tq=tq, tk=tk, n_valid=n_valid, q_pos0=q_pos0)
    return pl.pallas_call(
        kern, grid=(n_batch, nq_blocks),
        in_specs=[pl.BlockSpec((tq, D_QC), lambda b, j: (qb0 + b * nq_blocks + j, 0)),
                  pl.BlockSpec((k_rows, D_QC), lambda b, j: (b, 0)),
                  pl.BlockSpec((k_rows, D_VC), lambda b, j: (b, 0))],
        out_specs=pl.BlockSpec((tq, D_VC), lambda b, j: (b * nq_blocks + j, 0)),
        out_shape=jax.ShapeDtypeStruct((n_batch * n_q, D_VC), BF16),
        compiler_params=pltpu.CompilerParams(dimension_semantics=("arbitrary", "arbitrary"),
                                             vmem_limit_bytes=VMEM_LIMIT),
        name="mla",
    )(qc, kc, vc)


def _layer_norm(y, g, b):
    mu = jnp.mean(y, axis=-1, keepdims=True)
    var = jnp.mean(jnp.square(y - mu), axis=-1, keepdims=True)
    return (y - mu) * lax.rsqrt(var + LN_EPS) * g + b


def _outproj_kernel(oa_ref, ob_ref, oc_ref, x_ref, w_ref, g_ref, b_ref, rw_ref, rb_ref,
                    x1_o, x1b_o, idx_o, gate_o):
    mix = (jnp.dot(oa_ref[...], w_ref[0:D_A, :], preferred_element_type=F32)
           + jnp.dot(ob_ref[...], w_ref[D_A:D_A + W_B, :], preferred_element_type=F32)
           + jnp.dot(oc_ref[...], w_ref[D_A + W_B:, :], preferred_element_type=F32))
    x1 = _layer_norm(ALPHA * x_ref[...] + mix, g_ref[...], b_ref[...])
    x1_o[...] = x1
    x1b = x1.astype(BF16)
    x1b_o[...] = x1b
    lg = jnp.dot(x1b, rw_ref[...], preferred_element_type=F32) + rb_ref[...]
    e = lax.broadcasted_iota(I32, lg.shape, 1).astype(F32)
    vals = []
    for k in range(TOP_K):
        m = jnp.max(lg, axis=1, keepdims=True)
        idx = jnp.min(jnp.where(lg == m, e, float(N_EXPERTS)), axis=1, keepdims=True)
        vals.append(m)
        idx_o[:, k:k + 1] = idx.astype(I32)
        lg = jnp.where(e == idx, -jnp.inf, lg)
    ex = [jnp.exp(vv - vals[0]) for vv in vals]
    den = ex[0] + ex[1] + ex[2] + ex[3]
    for k in range(TOP_K):
        gate_o[:, k:k + 1] = ex[k] / den


def _outproj_call(oa, ob, oc, x, w_out, g, b, rw, rb):
    n = x.shape[0]
    tm = ROW_TILE
    row = lambda i: (i, 0)
    const2 = lambda i: (0, 0)
    return pl.pallas_call(
        _outproj_kernel, grid=(n // tm,),
        in_specs=[pl.BlockSpec((tm, D_A), row), pl.BlockSpec((tm, W_B), row), pl.BlockSpec((tm, D_VC), row),
                  pl.BlockSpec((tm, D_MODEL), row), pl.BlockSpec((D_MODEL, D_MODEL), const2),
                  pl.BlockSpec((1, D_MODEL), const2), pl.BlockSpec((1, D_MODEL), const2),
                  pl.BlockSpec((D_MODEL, N_EXPERTS), const2), pl.BlockSpec((1, N_EXPERTS), const2)],
        out_specs=[pl.BlockSpec((tm, D_MODEL), row), pl.BlockSpec((tm, D_MODEL), row),
                   pl.BlockSpec((tm, TOP_K), row), pl.BlockSpec((tm, TOP_K), row)],
        out_shape=[jax.ShapeDtypeStruct((n, D_MODEL), F32), jax.ShapeDtypeStruct((n, D_MODEL), BF16),
                   jax.ShapeDtypeStruct((n, TOP_K), I32), jax.ShapeDtypeStruct((n, TOP_K), F32)],
        compiler_params=pltpu.CompilerParams(dimension_semantics=("arbitrary",), vmem_limit_bytes=VMEM_LIMIT),
        name="outproj",
    )(oa, ob, oc, x, w_out, g, b, rw, rb)


def _moe_kernel(be_ref, nb_ref, xg_ref, g_ref, wgu_ref, bgu_ref, wdn_ref, bdn_ref, y_ref, wgu_s, wdn_s):
    i = pl.program_id(0)

    @pl.when(i < nb_ref[0])
    def _():
        e = be_ref[i]
        prev = be_ref[jnp.maximum(i - 1, 0)]

        @pl.when((i == 0) | (e != prev))
        def _():
            wgu_s[...] = wgu_ref[0].astype(BF16)
            wdn_s[...] = wdn_ref[0].astype(BF16)

        h = jnp.dot(xg_ref[...], wgu_s[...], preferred_element_type=F32) + bgu_ref[0]
        gate = jnp.minimum(h[:, :D_FF], SWIGLU_LIMIT)
        lin = jnp.clip(h[:, D_FF:], -SWIGLU_LIMIT, SWIGLU_LIMIT)
        act = (lin + 1.0) * (gate * jax.nn.sigmoid(SWIGLU_ALPHA * gate))
        y = jnp.dot(act.astype(BF16), wdn_s[...], preferred_element_type=F32) + bdn_ref[0]
        y_ref[...] = y * g_ref[...]

    @pl.when(i >= nb_ref[0])
    def _():
        y_ref[...] = jnp.zeros_like(y_ref)


def _moe_call(block_e, n_used, xg, row_gate, w_gu, b_gu, w_dn, b_dn):
    n_pad = xg.shape[0]
    tm = MOE_TILE
    grid_spec = pltpu.PrefetchScalarGridSpec(
        num_scalar_prefetch=2, grid=(n_pad // tm,),
        in_specs=[pl.BlockSpec((tm, D_MODEL), lambda i, be, nb: (i, 0)),
                  pl.BlockSpec((tm, 1), lambda i, be, nb: (i, 0)),
                  pl.BlockSpec((1, D_MODEL, 2 * D_FF), lambda i, be, nb: (be[i], 0, 0)),
                  pl.BlockSpec((1, 1, 2 * D_FF), lambda i, be, nb: (be[i], 0, 0)),
                  pl.BlockSpec((1, D_FF, D_MODEL), lambda i, be, nb: (be[i], 0, 0)),
                  pl.BlockSpec((1, 1, D_MODEL), lambda i, be, nb: (be[i], 0, 0))],
        out_specs=pl.BlockSpec((tm, D_MODEL), lambda i, be, nb: (i, 0)),
        scratch_shapes=[pltpu.VMEM((D_MODEL, 2 * D_FF), BF16), pltpu.VMEM((D_FF, D_MODEL), BF16)])
    return pl.pallas_call(
        _moe_kernel, grid_spec=grid_spec,
        out_shape=jax.ShapeDtypeStruct((n_pad, D_MODEL), F32),
        compiler_params=pltpu.CompilerParams(dimension_semantics=("arbitrary",),
                                             vmem_limit_bytes=56 * 1024 * 1024),
        name="moe",
    )(block_e, n_used, xg, row_gate, w_gu, b_gu.reshape(N_EXPERTS, 1, 2 * D_FF), w_dn,
      b_dn.reshape(N_EXPERTS, 1, D_MODEL))


def _ln2_kernel(x1_ref, moe_ref, g_ref, b_ref, o_ref):
    o_ref[...] = _layer_norm(ALPHA * x1_ref[...] + moe_ref[...], g_ref[...], b_ref[...])


def _ln2_call(x1, moe_out, g, b):
    n = x1.shape[0]
    tm = ROW_TILE
    row = lambda i: (i, 0)
    const2 = lambda i: (0, 0)
    return pl.pallas_call(
        _ln2_kernel, grid=(n // tm,),
        in_specs=[pl.BlockSpec((tm, D_MODEL), row), pl.BlockSpec((tm, D_MODEL), row),
                  pl.BlockSpec((1, D_MODEL), const2), pl.BlockSpec((1, D_MODEL), const2)],
        out_specs=pl.BlockSpec((tm, D_MODEL), row),
        out_shape=jax.ShapeDtypeStruct((n, D_MODEL), F32),
        compiler_params=pltpu.CompilerParams(dimension_semantics=("arbitrary",), vmem_limit_bytes=VMEM_LIMIT),
        name="ln2",
    )(x1, moe_out, g, b)


def _pack_w_in(w):
    sizes = (D_A, D_A, D_A, H_I * D_I, D_I, H_I, 2 * W_B, D_CQ, D_C, DR_C)
    offs = np.cumsum((0,) + sizes)
    seg = [w[:, offs[k]:offs[k + 1]] for k in range(len(sizes))]
    z = lambda c: jnp.zeros((w.shape[0], c), w.dtype)
    return jnp.concatenate(
        [seg[0], seg[1], seg[2], seg[3], seg[4], seg[5], z(LANES - D_I - H_I), seg[6], seg[7], seg[8],
         seg[9], z(LANES - DR_C)], axis=1).astype(BF16)


def _pack_w_uq(w):
    w3 = w.reshape(D_CQ, H_C, DN_C + DR_C)
    w3 = jnp.pad(w3, ((0, 0), (0, 0), (0, QC_SLOT - DN_C - DR_C)))
    return w3.reshape(D_CQ, D_QC).astype(BF16)


def _pack_w_kc(w_uk):
    wk = jnp.pad(w_uk.reshape(D_C, H_C, DN_C), ((0, 0), (0, 0), (0, QC_SLOT - DN_C))).reshape(D_C, D_QC)
    eye = jnp.pad(jnp.eye(DR_C, dtype=w_uk.dtype), ((0, LANES - DR_C), (DN_C, QC_SLOT - DN_C - DR_C)))
    return jnp.concatenate([wk, jnp.tile(eye, (1, H_C))], axis=0).astype(BF16)


def _rope_tables(pos):
    posf = pos.astype(F32)[:, None]
    n = pos.shape[0]

    def cs(half):
        inv = ROPE_THETA ** (-jnp.arange(half, dtype=F32) / half)
        ang = posf * inv[None, :]
        return jnp.cos(ang), jnp.sin(ang)

    c32, s32 = cs(32)
    z32 = jnp.zeros_like(s32)
    c64 = jnp.tile(jnp.concatenate([c32, c32], 1), (1, 2))
    sa64 = jnp.tile(jnp.concatenate([-s32, z32], 1), (1, 2))
    sb64 = jnp.tile(jnp.concatenate([z32, s32], 1), (1, 2))
    c16, s16 = cs(16)
    z16 = jnp.zeros_like(s16)
    one = lambda c: jnp.ones((n, c), F32)
    zero = lambda c: jnp.zeros((n, c), F32)
    cq = jnp.concatenate([one(DN_C), c16, c16, one(32)], 1)
    saq = jnp.concatenate([zero(DN_C), -s16, z16, zero(32)], 1)
    sbq = jnp.concatenate([zero(DN_C), z16, s16, zero(32)], 1)
    ck = jnp.concatenate([c16, c16, one(96)], 1)
    sak = jnp.concatenate([-s16, z16, zero(96)], 1)
    sbk = jnp.concatenate([z16, s16, zero(96)], 1)
    return [c64, sa64, sb64, cq, saq, sbq, ck, sak, sbk]


def _sgu_tables(w_s, b_s, n_sample_seq, n_sample_batch):
    i = jnp.arange(SGU_CHUNK)
    w_m = jnp.where((i[None, :] // CHUNK) <= (i[:, None] // CHUNK), w_s, 0.0)
    reps = ROW_TILE // SGU_CHUNK
    bd_p = jax.vmap(lambda m: jnp.kron(jnp.eye(reps, dtype=m.dtype), m))(w_m)
    ns = n_sample_seq
    bd_s = jax.vmap(lambda m: jnp.kron(jnp.eye(n_sample_batch, dtype=m.dtype), m[:ns, :ns]))(w_m)
    bd = jnp.stack([bd_p, bd_s]).astype(BF16)
    bias_chunk = jnp.repeat(b_s.T, DG_B, axis=1)
    bias = jnp.stack([jnp.tile(bias_chunk, (reps, 1)), jnp.tile(bias_chunk[:ns], (n_sample_batch, 1))])
    return bd, bias


def _route(top_idx, gates):
    n_assign = top_idx.size
    flat_e = top_idx.reshape(-1)
    order = jnp.argsort(flat_e, stable=True)
    sorted_e = flat_e[order]
    counts = jnp.zeros((N_EXPERTS,), I32).at[flat_e].add(1)
    padded = ((counts + MOE_TILE - 1) // MOE_TILE) * MOE_TILE
    pad_end = jnp.cumsum(padded)
    pad_start = pad_end - padded
    start = jnp.cumsum(counts) - counts
    dest = pad_start[sorted_e] + (jnp.arange(n_assign, dtype=I32) - start[sorted_e])
    n_blocks = -(-(n_assign + N_EXPERTS * (MOE_TILE - 1)) // MOE_TILE)
    n_pad = n_blocks * MOE_TILE
    row_token = jnp.zeros((n_pad,), I32).at[dest].set((order // TOP_K).astype(I32))
    row_gate = jnp.zeros((n_pad,), F32).at[dest].set(gates.reshape(-1)[order])
    pos = jnp.zeros((n_assign,), I32).at[order].set(dest.astype(I32))
    n_used = (pad_end[-1] // MOE_TILE).astype(I32)
    blk = jnp.minimum(jnp.arange(n_blocks, dtype=I32), n_used - 1) * MOE_TILE
    block_e = jnp.minimum(jnp.searchsorted(pad_end, blk, side='right'), N_EXPERTS - 1).astype(I32)
    return row_token, row_gate, pos, block_e, n_used.reshape(1)


def kernel(x_prompt, x_sample, cache_a_k, cache_a_v, cache_a_kidx, cache_c_latent, cache_c_krope, w_in, mla_q_norm, mla_kv_norm, w_uq, w_uk, w_uv, sgu_ln_g, sgu_ln_b, w_spatial, b_spatial, w_out, ln1_g, ln1_b, router_w, router_b, w_gate_up, b_gate_up, w_down, b_down, ln2_g, ln2_b):
    batch, seq, _ = x_prompt.shape
    dec_batch, dec_seq, _ = x_sample.shape
    past = cache_a_k.shape[2]
    n_p, n_s = batch * seq, dec_batch * dec_seq
    assert n_p % ROW_TILE == 0 and n_s == ROW_TILE and seq % ROW_TILE == 0
    n_prompt_tiles = n_p // ROW_TILE
    tab_period = seq // ROW_TILE

    pos_p = jnp.arange(seq, dtype=I32)
    pos_s = past + jnp.arange(dec_seq, dtype=I32)
    tabs = _rope_tables(jnp.concatenate([pos_p, jnp.tile(pos_s, dec_batch)]))

    n_keys_s = past + dec_seq
    tk_s = 384
    l_pad_s = -(-n_keys_s // tk_s) * tk_s
    top_k_p = min(TOPK_MAX, seq // 4)
    top_k_s = min(TOPK_MAX, n_keys_s // 4)

    x = jnp.concatenate([x_prompt.reshape(n_p, D_MODEL), x_sample.reshape(n_s, D_MODEL)], axis=0)
    caches_p = [[] for _ in range(5)]
    caches_s = [[] for _ in range(6)]

    def with_cache(cache, new, width):
        allk = jnp.concatenate([cache.astype(BF16), new.reshape(dec_batch, dec_seq, width)], axis=1)
        allk = jnp.pad(allk, ((0, 0), (0, l_pad_s - n_keys_s), (0, 0)))
        return allk.reshape(dec_batch * l_pad_s, width)

    for l in range(DEPTH):
        w_packed = _pack_w_in(w_in[l])
        wuq = _pack_w_uq(w_uq[l])
        wkc = _pack_w_kc(w_uk[l])
        wuv = w_uv[l].astype(BF16)
        bd, bsb = _sgu_tables(w_spatial[l], b_spatial[l], dec_seq, dec_batch)
        (qa, ka, kab, va, vab, qi, ki, kib, wi, ob, vb, qc, lat, kr, kc, vc) = _proj_call(
            x, w_packed, tabs, mla_q_norm[l].reshape(1, -1), mla_kv_norm[l].reshape(1, -1), wuq, wkc, wuv,
            sgu_ln_g[l].reshape(1, -1), sgu_ln_b[l].reshape(1, -1), bd, bsb, n_prompt_tiles, tab_period)

        oa_p = _dsa_call(qi, wi, qa, kib, kab, vab, n_batch=batch, n_q=seq, q_row0=0, k_rows=seq,
                         tq=128, tk=512, n_valid=seq, q_pos0=0, top_k=top_k_p)
        oc_p = _mla_call(qc, kc, vc, n_batch=batch, n_q=seq, q_row0=0, k_rows=seq,
                         tq=128, tk=512, n_valid=seq, q_pos0=0)

        kc_c, vc_c = _kvup_call(cache_c_latent[l].reshape(dec_batch * past, D_C),
                                cache_c_krope[l].reshape(dec_batch * past, DR_C), wkc, wuv)
        ki_s = with_cache(cache_a_kidx[l], kib[n_p:], D_I)
        ka_s = with_cache(cache_a_k[l].reshape(dec_batch, past, D_A), kab[n_p:], D_A)
        va_s = with_cache(cache_a_v[l].reshape(dec_batch, past, D_A), vab[n_p:], D_A)
        kc_s = with_cache(kc_c.reshape(dec_batch, past, D_QC), kc[n_p:], D_QC)
        vc_s = with_cache(vc_c.reshape(dec_batch, past, D_VC), vc[n_p:], D_VC)
        oa_s = _dsa_call(qi, wi, qa, ki_s, ka_s, va_s, n_batch=dec_batch, n_q=dec_seq, q_row0=n_p, k_rows=l_pad_s,
                         tq=dec_seq, tk=tk_s, n_valid=n_keys_s, q_pos0=past, top_k=top_k_s)
        oc_s = _mla_call(qc, kc_s, vc_s, n_batch=dec_batch, n_q=dec_seq, q_row0=n_p, k_rows=l_pad_s,
                         tq=dec_seq, tk=tk_s, n_valid=n_keys_s, q_pos0=past)

        oa = jnp.concatenate([oa_p, oa_s], axis=0)
        oc = jnp.concatenate([oc_p, oc_s], axis=0)
        x1, x1b, top_idx, gates = _outproj_call(
            oa, ob, oc, x, w_out[l].astype(BF16), ln1_g[l].reshape(1, -1), ln1_b[l].reshape(1, -1),
            router_w[l].astype(BF16), router_b[l].reshape(1, -1))

        row_token, row_gate, pos, block_e, n_used = _route(top_idx, gates)
        xg = x1b[row_token]
        y_rows = _moe_call(block_e, n_used, xg, row_gate.reshape(-1, 1), w_gate_up[l], b_gate_up[l],
                           w_down[l], b_down[l])
        moe_out = y_rows[pos].reshape(n_p + n_s, TOP_K, D_MODEL).sum(axis=1)
        x = _ln2_call(x1, moe_out, ln2_g[l].reshape(1, -1), ln2_b[l].reshape(1, -1))

        for dst, arr, shp in ((caches_p[0], ka, (H_A, DH_A)), (caches_p[1], va, (H_A, DH_A)),
                              (caches_p[2], ki, (D_I,)), (caches_p[3], lat, (D_C,)), (caches_p[4], kr, (DR_C,))):
            dst.append(arr[:n_p].reshape((batch, seq) + shp))
        for dst, arr, shp in ((caches_s[0], ka, (H_A, DH_A)), (caches_s[1], va, (H_A, DH_A)),
                              (caches_s[2], ki, (D_I,)), (caches_s[3], lat, (D_C,)), (caches_s[4], kr, (DR_C,)),
                              (caches_s[5], vb, (W_B,))):
            dst.append(arr[n_p:].reshape((dec_batch, dec_seq) + shp))

    y_prompt = x[:n_p].reshape(batch, seq, D_MODEL)
    y_sample = x[n_p:].reshape(dec_batch, dec_seq, D_MODEL)
    return (y_prompt, y_sample) + tuple(jnp.stack(c) for c in caches_p) + tuple(jnp.stack(c) for c in caches_s)
```

```python
import functools

import numpy as np
import jax
import jax.numpy as jnp
from jax import lax
from jax.experimental import pallas as pl
from jax.experimental.pallas import tpu as pltpu

F32 = jnp.float32
BF16 = jnp.bfloat16
I32 = jnp.int32

D_MODEL = 1024
DEPTH = 2
CHUNK = 64
ROPE_THETA = 10000.0
LN_EPS = 1e-5
RMS_EPS = 1e-6
H_A, DH_A = 6, 64
H_I, D_I = 8, 64
TOPK_MAX = 256
G_B, DG_B = 4, 64
W_B = G_B * DG_B
SGU_CHUNK = 128
H_C, DN_C, DR_C, DV_C = 6, 64, 32, 64
D_CQ, D_C = 256, 256
MLA_SCALE = (DN_C + DR_C) ** -0.5
D_A = H_A * DH_A
D_VC = H_C * DV_C
N_EXPERTS = 32
TOP_K = 4
D_FF = 1024
SWIGLU_LIMIT = 7.0
SWIGLU_ALPHA = 1.702
ALPHA = (2 * DEPTH) ** 0.25

LANES = 128
SUBLANES = 8
ROW_TILE = 256
MOE_TILE = 256
ATT_TQ = 256
ATT_TK = 512
SAMPLE_TQ = 128
SAMPLE_TK = 384
VMEM_LIMIT = 48 * 1024 * 1024

C_QA, C_KA, C_VA, C_QI, C_KIW, C_ZU, C_ZV, C_CQ, C_CKV, C_KR = (
    0, 384, 768, 1152, 1664, 1792, 2048, 2304, 2560, 2816)
IN_PACKED = 2944
QC_SLOT = 128
D_QC = H_C * QC_SLOT

NEG_BIG = -1e30
KEY_NEG_INF = -2139095041
INT_MIN = -2147483648


def _nt_dot(a, b):
    return lax.dot_general(a, b, (((1,), (1,)), ((), ())), preferred_element_type=F32)


def _rope128(x, c, sa, sb, half):
    return x * c + pltpu.roll(x, LANES - half, 1) * sa + pltpu.roll(x, half, 1) * sb


def _proj_kernel(x_ref, w_ref, c64_ref, sa64_ref, sb64_ref, cq_ref, saq_ref, sbq_ref,
                 ck_ref, sak_ref, sbk_ref, qn_ref, kvn_ref, wuq_ref, wkc_ref, wuv_ref,
                 sgug_ref, sgub_ref, bd_ref, bsb_ref,
                 qa_o, ka_o, kab_o, va_o, vab_o, qi_o, ki_o, kib_o, wi_o, ob_o, vb_o,
                 qc_o, lat_o, kr_o, kc_o, vc_o):
    h = jnp.dot(x_ref[...].astype(BF16), w_ref[...], preferred_element_type=F32)
    c64, sa64, sb64 = c64_ref[...], sa64_ref[...], sb64_ref[...]

    def rope64(col):
        return _rope128(h[:, col:col + LANES], c64, sa64, sb64, 32)

    for c in range(D_A // LANES):
        qa_o[:, c * LANES:(c + 1) * LANES] = (rope64(C_QA + c * LANES) * (DH_A ** -0.5)).astype(BF16)
        ka = rope64(C_KA + c * LANES)
        ka_o[:, c * LANES:(c + 1) * LANES] = ka
        kab_o[:, c * LANES:(c + 1) * LANES] = ka.astype(BF16)
    va = h[:, C_VA:C_VA + D_A]
    va_o[...] = va
    vab_o[...] = va.astype(BF16)
    for c in range(H_I * D_I // LANES):
        qi_o[:, c * LANES:(c + 1) * LANES] = (rope64(C_QI + c * LANES) * (D_I ** -0.5)).astype(BF16)
    kiw = rope64(C_KIW)
    ki_o[...] = kiw[:, :D_I]
    kib_o[...] = kiw[:, :D_I].astype(BF16)
    wi_o[...] = h[:, C_KIW:C_KIW + LANES] * (H_I ** -0.5)

    z = h[:, C_ZU:C_ZU + 2 * W_B]
    z = 0.5 * z * (1.0 + jnp.tanh(np.sqrt(2.0 / np.pi) * (z + 0.044715 * (z * z * z))))
    u = z[:, :W_B]
    v = z[:, W_B:]
    mu = jnp.mean(v, axis=-1, keepdims=True)
    var = jnp.mean(jnp.square(v - mu), axis=-1, keepdims=True)
    v = (v - mu) * lax.rsqrt(var + LN_EPS) * sgug_ref[...] + sgub_ref[...]
    vb_o[...] = v
    v16 = v.astype(BF16)
    lane = lax.broadcasted_iota(I32, (v.shape[0], LANES), 1)
    for w in range(W_B // LANES):
        vw = v16[:, w * LANES:(w + 1) * LANES]
        m0 = jnp.dot(bd_ref[0, 2 * w], vw, preferred_element_type=F32)
        m1 = jnp.dot(bd_ref[0, 2 * w + 1], vw, preferred_element_type=F32)
        mixed = jnp.where(lane < DG_B, m0, m1) + bsb_ref[0, :, w * LANES:(w + 1) * LANES]
        ob_o[:, w * LANES:(w + 1) * LANES] = (u[:, w * LANES:(w + 1) * LANES] * mixed).astype(BF16)

    cq = h[:, C_CQ:C_CQ + D_CQ]
    cq = cq * lax.rsqrt(jnp.mean(jnp.square(cq), axis=-1, keepdims=True) + RMS_EPS) * qn_ref[...]
    q = jnp.dot(cq.astype(BF16), wuq_ref[...], preferred_element_type=F32)
    cqt, saq, sbq = cq_ref[...], saq_ref[...], sbq_ref[...]
    for hh in range(H_C):
        qs = _rope128(q[:, hh * QC_SLOT:(hh + 1) * QC_SLOT], cqt, saq, sbq, DR_C // 2)
        qc_o[:, hh * QC_SLOT:(hh + 1) * QC_SLOT] = (qs * MLA_SCALE).astype(BF16)
    ckv = h[:, C_CKV:C_CKV + D_C]
    lat = ckv * lax.rsqrt(jnp.mean(jnp.square(ckv), axis=-1, keepdims=True) + RMS_EPS) * kvn_ref[...]
    lat_o[...] = lat
    kr = _rope128(h[:, C_KR:C_KR + LANES], ck_ref[...], sak_ref[...], sbk_ref[...], DR_C // 2)
    kr_o[...] = kr[:, :DR_C]
    lat16 = lat.astype(BF16)
    kc = (jnp.dot(lat16, wkc_ref[:D_C, :], preferred_element_type=F32)
          + jnp.dot(kr.astype(BF16), wkc_ref[D_C:, :], preferred_element_type=F32))
    kc_o[...] = kc.astype(BF16)
    vc_o[...] = jnp.dot(lat16, wuv_ref[...], preferred_element_type=F32).astype(BF16)


def _proj_call(x, w_packed, tabs, qn, kvn, wuq, wkc, wuv, sgug, sgub, bd, bsb, n_prompt_tiles, tab_period):
    n = x.shape[0]
    tm = ROW_TILE
    grid = (n // tm,)

    def row(i):
        return (i, 0)

    def const2(i):
        return (0, 0)

    def tab(i):
        return (jnp.where(i < n_prompt_tiles, i % tab_period, tab_period), 0)

    def grp(i):
        return (jnp.where(i < n_prompt_tiles, 0, 1), 0, 0, 0)

    def grp3(i):
        return (jnp.where(i < n_prompt_tiles, 0, 1), 0, 0)

    in_specs = [pl.BlockSpec((tm, D_MODEL), row), pl.BlockSpec((D_MODEL, IN_PACKED), const2)]
    in_specs += [pl.BlockSpec((tm, LANES), tab)] * 9
    in_specs += [pl.BlockSpec((1, D_CQ), const2), pl.BlockSpec((1, D_C), const2),
                 pl.BlockSpec((D_CQ, D_QC), const2), pl.BlockSpec((D_C + LANES, D_QC), const2),
                 pl.BlockSpec((D_C, D_VC), const2),
                 pl.BlockSpec((1, W_B), const2), pl.BlockSpec((1, W_B), const2),
                 pl.BlockSpec((1, G_B, tm, tm), grp), pl.BlockSpec((1, tm, W_B), grp3)]
    outs = [(D_A, BF16), (D_A, F32), (D_A, BF16), (D_A, F32), (D_A, BF16), (H_I * D_I, BF16),
            (D_I, F32), (D_I, BF16), (LANES, F32), (W_B, BF16), (W_B, F32),
            (D_QC, BF16), (D_C, F32), (DR_C, F32), (D_QC, BF16), (D_VC, BF16)]
    out_shape = [jax.ShapeDtypeStruct((n, c), d) for c, d in outs]
    out_specs = [pl.BlockSpec((tm, c), row) for c, _ in outs]
    return pl.pallas_call(
        _proj_kernel, grid=grid, in_specs=in_specs, out_specs=out_specs, out_shape=out_shape,
        compiler_params=pltpu.CompilerParams(dimension_semantics=("arbitrary",), vmem_limit_bytes=VMEM_LIMIT),
        name="proj",
    )(x, w_packed, *tabs, qn, kvn, wuq, wkc, wuv, sgug, sgub, bd, bsb)


def _kvup_kernel(lat_ref, kr_ref, wkc_ref, wuv_ref, kc_o, vc_o):
    lat16 = lat_ref[...].astype(BF16)
    kc = (jnp.dot(lat16, wkc_ref[:D_C, :], preferred_element_type=F32)
          + jnp.dot(kr_ref[...].astype(BF16), wkc_ref[D_C:D_C + DR_C, :], preferred_element_type=F32))
    kc_o[...] = kc.astype(BF16)
    vc_o[...] = jnp.dot(lat16, wuv_ref[...], preferred_element_type=F32).astype(BF16)


def _kvup_call(lat, kr, wkc, wuv):
    n = lat.shape[0]
    tm = 1024
    return pl.pallas_call(
        _kvup_kernel, grid=(n // tm,),
        in_specs=[pl.BlockSpec((tm, D_C), lambda i: (i, 0)), pl.BlockSpec((tm, DR_C), lambda i: (i, 0)),
                  pl.BlockSpec((D_C + LANES, D_QC), lambda i: (0, 0)), pl.BlockSpec((D_C, D_VC), lambda i: (0, 0))],
        out_specs=[pl.BlockSpec((tm, D_QC), lambda i: (i, 0)), pl.BlockSpec((tm, D_VC), lambda i: (i, 0))],
        out_shape=[jax.ShapeDtypeStruct((n, D_QC), BF16), jax.ShapeDtypeStruct((n, D_VC), BF16)],
        compiler_params=pltpu.CompilerParams(dimension_semantics=("arbitrary",), vmem_limit_bytes=VMEM_LIMIT),
        name="kvup",
    )(lat, kr, wkc, wuv)


def _key_bounds(j, tq, tk, n_valid, q_pos0):
    q_first = q_pos0 + j * tq
    kmax = jnp.minimum(((q_first + tq - 1) // CHUNK + 1) * CHUNK, n_valid)
    n_tiles = (kmax + tk - 1) // tk
    n_full = jnp.minimum((q_first // CHUNK + 1) * CHUNK, n_valid) // tk
    qpos = q_first + lax.broadcasted_iota(I32, (1, tq), 1)
    bound = jnp.minimum((qpos // CHUNK + 1) * CHUNK, n_valid)
    return n_tiles, n_full, bound


def _flash_init(m_s, l_s, acc_s):
    m_s[...] = jnp.full(m_s.shape, NEG_BIG, F32)
    l_s[...] = jnp.zeros(l_s.shape, F32)
    acc_s[...] = jnp.zeros(acc_s.shape, F32)


def _flash_update(hh, s_t, v_t, m_s, l_s):
    m_old = m_s[hh]
    m_new = jnp.maximum(m_old, jnp.max(s_t, axis=0, keepdims=True))
    p = jnp.exp(s_t - m_new)
    a = jnp.exp(m_old - m_new)
    l_s[hh] = a * l_s[hh] + jnp.sum(p, axis=0, keepdims=True)
    m_s[hh] = m_new
    return a, jnp.dot(v_t, p.astype(BF16), preferred_element_type=F32)


def _flash_pair_accumulate(w, first_rows, upd0, upd1, acc_s):
    (a0, pv0), (a1, pv1) = upd0, upd1
    acc_s[w] = acc_s[w] * jnp.where(first_rows, a0, a1) + jnp.where(first_rows, pv0, pv1)


def _flash_heads(scores, values, n_heads, first_rows, m_s, l_s, acc_s):
    ahead = 3
    pending = [scores(hh) for hh in range(min(ahead, n_heads))]
    upd = []
    for hh in range(n_heads):
        s_cur = pending.pop(0)
        if hh + ahead < n_heads:
            pending.append(scores(hh + ahead))
        upd.append(_flash_update(hh, s_cur, values(hh // 2), m_s, l_s))
        if hh % 2:
            _flash_pair_accumulate(hh // 2, first_rows, upd[hh - 1], upd[hh], acc_s)


def _flash_finish(o_ref, first_rows, l_s, acc_s):
    for w in range(acc_s.shape[0]):
        l_sel = jnp.where(first_rows, l_s[2 * w], l_s[2 * w + 1])
        o_ref[:, w * LANES:(w + 1) * LANES] = (acc_s[w] / l_sel).T.astype(o_ref.dtype)


def _flash_scratch(n_heads, tq):
    return [pltpu.VMEM((n_heads, 1, tq), F32), pltpu.VMEM((n_heads, 1, tq), F32),
            pltpu.VMEM((n_heads // 2, LANES, tq), F32)]


def _dsa_kernel(qi_ref, wi_ref, qa_ref, ki_ref, ka_ref, vat_ref, o_ref, key_ref, m_s, l_s, acc_s, *,
                tq, tk, n_valid, q_pos0, top_k):
    j = pl.program_id(1)
    n_tiles, _, bound = _key_bounds(j, tq, tk, n_valid, q_pos0)
    row_tk = lax.broadcasted_iota(I32, (tk, tq), 0)
    row8 = lax.broadcasted_iota(I32, (SUBLANES, tq), 0)

    qi = qi_ref[...]
    q_stack = jnp.concatenate([qi[:, hh * D_I:(hh + 1) * D_I] for hh in range(H_I)], axis=0)
    wi_t = wi_ref[...].T
    wis = [wi_t[D_I + hh:D_I + hh + 1, :] for hh in range(H_I)]

    def score_tile(kt, carry):
        k0 = pl.multiple_of(kt * tk, tk)
        d = _nt_dot(ki_ref[pl.ds(k0, tk), :], q_stack)
        s = wis[0] * jnp.maximum(d[:, 0:tq], 0.0)
        for hh in range(1, H_I):
            s = s + wis[hh] * jnp.maximum(d[:, hh * tq:(hh + 1) * tq], 0.0)
        bits = pltpu.bitcast(s, I32)
        key = jnp.where(bits < 0, bits ^ 0x7FFFFFFF, bits)
        key = jnp.where(key == -1, 0, key)
        key_ref[kt] = jnp.where(k0 + row_tk < bound, key, KEY_NEG_INF)
        return carry

    lax.fori_loop(0, n_tiles, score_tile, 0)

    def count(pred):
        def body(kt, acc):
            for r in range(tk // SUBLANES):
                kk = key_ref[kt, r * SUBLANES:(r + 1) * SUBLANES, :]
                acc = acc + jnp.where(pred(kk, kt * tk + r * SUBLANES), 1.0, 0.0)
            return acc
        acc = lax.fori_loop(0, n_tiles, body, jnp.zeros((SUBLANES, tq), F32))
        return jnp.sum(acc, axis=0, keepdims=True)

    kf = float(top_k)
    c0 = count(lambda kk, base: kk >= 0)
    nonneg = c0 >= kf
    t0 = jnp.where(nonneg, 0, INT_MIN).astype(I32)
    cnt0 = jnp.where(nonneg, c0, (n_tiles * tk).astype(F32))

    def bit_cond(carry):
        it, _, cnt_t = carry
        return (it < 31) & (jnp.max(jnp.abs(cnt_t - kf)) > 0.5)

    def bit_body(carry):
        it, t, cnt_t = carry
        cand = t | lax.shift_left(jnp.int32(1), 30 - it)
        cnt = count(lambda kk, base: kk >= cand)
        ok = cnt >= kf
        return it + 1, jnp.where(ok, cand, t), jnp.where(ok, cnt, cnt_t)

    _, t, _ = lax.while_loop(bit_cond, bit_body, (jnp.int32(0), t0, cnt0))

    c_gt = count(lambda kk, base: kk > t)
    c_eq = count(lambda kk, base: kk == t)
    need = kf - c_gt
    excess = jnp.max(jnp.where((c_eq > need) & (t > KEY_NEG_INF), 1.0, 0.0)) > 0.5

    def tie_path():
        def jb(it, jj):
            cand = jj | lax.shift_left(jnp.int32(1), 12 - it)
            f = count(lambda kk, base: (kk == t) & (base + row8 < cand))
            return jnp.where(f < need, cand, jj)
        return lax.fori_loop(0, 13, jb, jnp.zeros((1, tq), I32))

    j_last = lax.cond(excess, tie_path, lambda: jnp.full((1, tq), 1 << 30, I32))

    def bias_tile(kt, carry):
        kk = key_ref[kt]
        kpos = kt * tk + row_tk
        sel = ((kk > t) | ((kk == t) & (kpos <= j_last))) & (kk != KEY_NEG_INF)
        key_ref[kt] = pltpu.bitcast(jnp.where(sel, 0.0, NEG_BIG).astype(F32), I32)
        return carry

    lax.fori_loop(0, n_tiles, bias_tile, 0)

    lane = lax.broadcasted_iota(I32, (tq, LANES), 1)
    first_rows = lax.broadcasted_iota(I32, (LANES, tq), 0) < DH_A
    qa = qa_ref[...]
    half_masks = [jnp.where(lane < DH_A, 1.0, 0.0).astype(BF16), jnp.where(lane < DH_A, 0.0, 1.0).astype(BF16)]
    qhs = [qa[:, (hh // 2) * LANES:(hh // 2 + 1) * LANES] * half_masks[hh % 2] for hh in range(H_A)]
    _flash_init(m_s, l_s, acc_s)

    def att_tile(kt, carry):
        k0 = pl.multiple_of(kt * tk, tk)
        bias = pltpu.bitcast(key_ref[kt], F32)

        def scores(hh):
            return _nt_dot(ka_ref[pl.ds(k0, tk), (hh // 2) * LANES:(hh // 2 + 1) * LANES], qhs[hh]) + bias

        _flash_heads(scores, lambda w: vat_ref[kt, w * LANES:(w + 1) * LANES, :], H_A, first_rows, m_s, l_s, acc_s)
        return carry

    lax.fori_loop(0, n_tiles, att_tile, 0)
    _flash_finish(o_ref, first_rows, l_s, acc_s)


def _dsa_call(qi, wi, qa, ki, ka, vat, *, n_batch, n_q, q_row0, k_rows, tq, tk, n_valid, q_pos0, top_k):
    nq_blocks = n_q // tq
    qb0 = q_row0 // tq
    n_kt = k_rows // tk
    kern = functools.partial(_dsa_kernel, tq=tq, tk=tk, n_valid=n_valid, q_pos0=q_pos0, top_k=top_k)

    def qmap(b, j):
        return (qb0 + b * nq_blocks + j, 0)

    def kmap(b, j):
        return (b, 0)

    return pl.pallas_call(
        kern, grid=(n_batch, nq_blocks),
        in_specs=[pl.BlockSpec((tq, H_I * D_I), qmap), pl.BlockSpec((tq, LANES), qmap), pl.BlockSpec((tq, D_A), qmap),
                  pl.BlockSpec((k_rows, D_I), kmap), pl.BlockSpec((k_rows, D_A), kmap),
                  pl.BlockSpec((n_kt, D_A, tk), lambda b, j: (b, 0, 0))],
        out_specs=pl.BlockSpec((tq, D_A), lambda b, j: (b * nq_blocks + j, 0)),
        out_shape=jax.ShapeDtypeStruct((n_batch * n_q, D_A), BF16),
        scratch_shapes=[pltpu.VMEM((n_kt, tk, tq), I32)] + _flash_scratch(H_A, tq),
        compiler_params=pltpu.CompilerParams(dimension_semantics=("arbitrary", "arbitrary"),
                                             vmem_limit_bytes=VMEM_LIMIT),
        name="dsa",
    )(qi, wi, qa, ki, ka, vat)


def _mla_kernel(qc_ref, kc_ref, vct_ref, o_ref, m_s, l_s, acc_s, *, tq, tk, n_valid, q_pos0):
    j = pl.program_id(1)
    n_tiles, n_full, bound = _key_bounds(j, tq, tk, n_valid, q_pos0)
    row_tk = lax.broadcasted_iota(I32, (tk, tq), 0)
    first_rows = lax.broadcasted_iota(I32, (LANES, tq), 0) < DV_C
    qc = qc_ref[...]
    qhs = [qc[:, hh * QC_SLOT:(hh + 1) * QC_SLOT] for hh in range(H_C)]
    _flash_init(m_s, l_s, acc_s)

    def tile(kt, carry, masked):
        k0 = pl.multiple_of(kt * tk, tk)
        if masked:
            bias = jnp.where(k0 + row_tk < bound, 0.0, NEG_BIG)

        def scores(hh):
            s_t = _nt_dot(kc_ref[pl.ds(k0, tk), hh * QC_SLOT:(hh + 1) * QC_SLOT], qhs[hh])
            return s_t + bias if masked else s_t

        _flash_heads(scores, lambda w: vct_ref[kt, w * LANES:(w + 1) * LANES, :], H_C, first_rows, m_s, l_s, acc_s)
        return carry

    lax.fori_loop(0, n_full, functools.partial(tile, masked=False), 0)
    lax.fori_loop(n_full, n_tiles, functools.partial(tile, masked=True), 0)
    _flash_finish(o_ref, first_rows, l_s, acc_s)


def _mla_call(qc, kc, vct, *, n_batch, n_q, q_row0, k_rows, tq, tk, n_valid, q_pos0):
    nq_blocks = n_q // tq
    qb0 = q_row0 // tq
    n_kt = k_rows // tk
    kern = functools.partial(_mla_kernel, tq=tq, tk=tk, n_valid=n_valid, q_pos0=q_pos0)
    return pl.pallas_call(
        kern, grid=(n_batch, nq_blocks),
        in_specs=[pl.BlockSpec((tq, D_QC), lambda b, j: (qb0 + b * nq_blocks + j, 0)),
                  pl.BlockSpec((k_rows, D_QC), lambda b, j: (b, 0)),
                  pl.BlockSpec((n_kt, D_VC, tk), lambda b, j: (b, 0, 0))],
        out_specs=pl.BlockSpec((tq, D_VC), lambda b, j: (b * nq_blocks + j, 0)),
        out_shape=jax.ShapeDtypeStruct((n_batch * n_q, D_VC), BF16),
        scratch_shapes=_flash_scratch(H_C, tq),
        compiler_params=pltpu.CompilerParams(dimension_semantics=("arbitrary", "arbitrary"),
                                             vmem_limit_bytes=VMEM_LIMIT),
        name="mla",
    )(qc, kc, vct)


def _layer_norm(y, g, b):
    mu = jnp.mean(y, axis=-1, keepdims=True)
    var = jnp.mean(jnp.square(y - mu), axis=-1, keepdims=True)
    return (y - mu) * lax.rsqrt(var + LN_EPS) * g + b


def _outproj_kernel(oa_ref, ob_ref, oc_ref, x_ref, w_ref, g_ref, b_ref, rw_ref, rb_ref, tri_ref,
                    x1_o, idx_o, gate_o, rank_o, cnt_o, run_s):
    @pl.when(pl.program_id(0) == 0)
    def _():
        run_s[...] = jnp.zeros_like(run_s)

    mix = (jnp.dot(oa_ref[...], w_ref[0:D_A, :], preferred_element_type=F32)
           + jnp.dot(ob_ref[...], w_ref[D_A:D_A + W_B, :], preferred_element_type=F32)
           + jnp.dot(oc_ref[...], w_ref[D_A + W_B:, :], preferred_element_type=F32))
    x1 = _layer_norm(ALPHA * x_ref[...] + mix, g_ref[...], b_ref[...])
    x1_o[...] = x1
    lg = jnp.dot(x1.astype(BF16), rw_ref[...], preferred_element_type=F32) + rb_ref[...]
    e = lax.broadcasted_iota(I32, lg.shape, 1).astype(F32)
    vals, hots = [], []
    for k in range(TOP_K):
        m = jnp.max(lg, axis=1, keepdims=True)
        idx = jnp.min(jnp.where(lg == m, e, float(N_EXPERTS)), axis=1, keepdims=True)
        vals.append(m)
        hots.append(jnp.where(e == idx, 1.0, 0.0))
        idx_o[:, k:k + 1] = idx.astype(I32)
        lg = jnp.where(e == idx, -jnp.inf, lg)
    ex = [jnp.exp(vv - vals[0]) for vv in vals]
    den = ex[0] + ex[1] + ex[2] + ex[3]
    for k in range(TOP_K):
        gate_o[:, k:k + 1] = ex[k] / den
    per_token = hots[0] + hots[1] + hots[2] + hots[3]
    before = run_s[...] + jnp.dot(tri_ref[...], per_token.astype(BF16), preferred_element_type=F32)
    for k in range(TOP_K):
        rank_o[:, k:k + 1] = jnp.sum(hots[k] * before, axis=1, keepdims=True).astype(I32)
    run_s[...] = run_s[...] + jnp.sum(per_token, axis=0, keepdims=True)
    cnt_o[...] = run_s[...]


def _outproj_call(oa, ob, oc, x, w_out, g, b, rw, rb):
    n = x.shape[0]
    tm = ROW_TILE
    row = lambda i: (i, 0)
    const2 = lambda i: (0, 0)
    tri = jnp.tril(jnp.ones((tm, tm), BF16), -1)
    return pl.pallas_call(
        _outproj_kernel, grid=(n // tm,),
        in_specs=[pl.BlockSpec((tm, D_A), row), pl.BlockSpec((tm, W_B), row), pl.BlockSpec((tm, D_VC), row),
                  pl.BlockSpec((tm, D_MODEL), row), pl.BlockSpec((D_MODEL, D_MODEL), const2),
                  pl.BlockSpec((1, D_MODEL), const2), pl.BlockSpec((1, D_MODEL), const2),
                  pl.BlockSpec((D_MODEL, N_EXPERTS), const2), pl.BlockSpec((1, N_EXPERTS), const2),
                  pl.BlockSpec((tm, tm), const2)],
        out_specs=[pl.BlockSpec((tm, D_MODEL), row), pl.BlockSpec((tm, TOP_K), row), pl.BlockSpec((tm, TOP_K), row),
                   pl.BlockSpec((tm, TOP_K), row), pl.BlockSpec((1, N_EXPERTS), const2)],
        out_shape=[jax.ShapeDtypeStruct((n, D_MODEL), F32), jax.ShapeDtypeStruct((n, TOP_K), I32),
                   jax.ShapeDtypeStruct((n, TOP_K), F32), jax.ShapeDtypeStruct((n, TOP_K), I32),
                   jax.ShapeDtypeStruct((1, N_EXPERTS), F32)],
        scratch_shapes=[pltpu.VMEM((1, N_EXPERTS), F32)],
        compiler_params=pltpu.CompilerParams(dimension_semantics=("arbitrary",), vmem_limit_bytes=VMEM_LIMIT),
        name="outproj",
    )(oa, ob, oc, x, w_out, g, b, rw, rb, tri)


def _dispatch_kernel(pe_ref, x_ref, dest_ref, xg_hbm, zero_s, sem):
    i = pl.program_id(0)
    tm = x_ref.shape[0]

    @pl.when(i == 0)
    def _():
        zero_s[...] = jnp.zeros_like(zero_s)

        def fill(e):
            start = pl.multiple_of(pe_ref[e] - MOE_TILE, MOE_TILE)
            return pltpu.make_async_copy(zero_s, xg_hbm.at[pl.ds(start, MOE_TILE)], sem)

        def nonempty(e):
            return pe_ref[e] > (pe_ref[e - 1] if e else 0)

        n_blocks = xg_hbm.shape[0] // MOE_TILE
        first_tail = (n_blocks * MOE_TILE - N_EXPERTS * (MOE_TILE - 1)) // MOE_TILE

        def tail(b):
            return pltpu.make_async_copy(zero_s, xg_hbm.at[pl.ds(b * MOE_TILE, MOE_TILE)], sem)

        def unused(b):
            return b * MOE_TILE >= pe_ref[N_EXPERTS - 1]

        for e in range(N_EXPERTS):
            pl.when(nonempty(e))(lambda e=e: fill(e).start())
        for b in range(first_tail, n_blocks):
            pl.when(unused(b))(lambda b=b: tail(b).start())
        for e in range(N_EXPERTS):
            pl.when(nonempty(e))(lambda e=e: fill(e).wait())
        for b in range(first_tail, n_blocks):
            pl.when(unused(b))(lambda b=b: tail(b).wait())

    def row_copy(r, k):
        return pltpu.make_async_copy(x_ref.at[pl.ds(r, 1)], xg_hbm.at[pl.ds(dest_ref[0, 0, r * TOP_K + k], 1)], sem)

    def start_row(r, c):
        for k in range(TOP_K):
            row_copy(r, k).start()
        return c

    def wait_row(r, c):
        for k in range(TOP_K):
            row_copy(r, k).wait()
        return c

    lax.fori_loop(0, tm, start_row, 0)
    lax.fori_loop(0, tm, wait_row, 0)


def _dispatch_call(pad_end, x1, dest3, n_pad):
    n = x1.shape[0]
    tm = ROW_TILE
    grid_spec = pltpu.PrefetchScalarGridSpec(
        num_scalar_prefetch=1, grid=(n // tm,),
        in_specs=[pl.BlockSpec((tm, D_MODEL), lambda i, pe: (i, 0)),
                  pl.BlockSpec((1, 1, tm * TOP_K), lambda i, pe: (i, 0, 0), memory_space=pltpu.SMEM)],
        out_specs=pl.BlockSpec(memory_space=pl.ANY),
        scratch_shapes=[pltpu.VMEM((MOE_TILE, D_MODEL), F32), pltpu.SemaphoreType.DMA(())])
    return pl.pallas_call(
        _dispatch_kernel, grid_spec=grid_spec,
        out_shape=jax.ShapeDtypeStruct((n_pad, D_MODEL), F32),
        compiler_params=pltpu.CompilerParams(dimension_semantics=("arbitrary",), vmem_limit_bytes=VMEM_LIMIT,
                                             has_side_effects=True),
        name="dispatch",
    )(pad_end, x1, dest3)


def _moe_kernel(be_ref, nb_ref, xg_ref, wgu_ref, bgu_ref, wdn_ref, bdn_ref, y_ref, wgu_s, wdn_s):
    i = pl.program_id(0)

    @pl.when(i < nb_ref[0])
    def _():
        e = be_ref[i]
        prev = be_ref[jnp.maximum(i - 1, 0)]

        @pl.when((i == 0) | (e != prev))
        def _():
            wgu_s[...] = wgu_ref[0].astype(BF16)
            wdn_s[...] = wdn_ref[0].astype(BF16)

        h = jnp.dot(xg_ref[...].astype(BF16), wgu_s[...], preferred_element_type=F32) + bgu_ref[0]
        gate = jnp.minimum(h[:, :D_FF], SWIGLU_LIMIT)
        lin = jnp.clip(h[:, D_FF:], -SWIGLU_LIMIT, SWIGLU_LIMIT)
        act = (lin + 1.0) * (gate * jax.nn.sigmoid(SWIGLU_ALPHA * gate))
        y_ref[...] = jnp.dot(act.astype(BF16), wdn_s[...], preferred_element_type=F32) + bdn_ref[0]

    @pl.when(i >= nb_ref[0])
    def _():
        y_ref[...] = jnp.zeros_like(y_ref)


def _moe_call(block_e, n_used, xg, w_gu, b_gu, w_dn, b_dn):
    n_pad = xg.shape[0]
    tm = MOE_TILE

    def used(i, be, nb):
        return (jnp.minimum(i, nb[0] - 1), 0)

    grid_spec = pltpu.PrefetchScalarGridSpec(
        num_scalar_prefetch=2, grid=(n_pad // tm,),
        in_specs=[pl.BlockSpec((tm, D_MODEL), used),
                  pl.BlockSpec((1, D_MODEL, 2 * D_FF), lambda i, be, nb: (be[i], 0, 0)),
                  pl.BlockSpec((1, 1, 2 * D_FF), lambda i, be, nb: (be[i], 0, 0)),
                  pl.BlockSpec((1, D_FF, D_MODEL), lambda i, be, nb: (be[i], 0, 0)),
                  pl.BlockSpec((1, 1, D_MODEL), lambda i, be, nb: (be[i], 0, 0))],
        out_specs=pl.BlockSpec((tm, D_MODEL), lambda i, be, nb: (i, 0)),
        scratch_shapes=[pltpu.VMEM((D_MODEL, 2 * D_FF), BF16), pltpu.VMEM((D_FF, D_MODEL), BF16)])
    return pl.pallas_call(
        _moe_kernel, grid_spec=grid_spec,
        out_shape=jax.ShapeDtypeStruct((n_pad, D_MODEL), F32),
        compiler_params=pltpu.CompilerParams(dimension_semantics=("arbitrary",),
                                             vmem_limit_bytes=56 * 1024 * 1024),
        name="moe",
    )(block_e, n_used, xg, w_gu, b_gu.reshape(N_EXPERTS, 1, 2 * D_FF), w_dn, b_dn.reshape(N_EXPERTS, 1, D_MODEL))


def _combine_kernel(x1_ref, gate_ref, dcur_ref, dnext_ref, g_ref, b_ref, y_hbm, o_ref, ybuf, sem):
    i = pl.program_id(0)
    tm = x1_ref.shape[0]
    slot = i % 2

    def row_copy(d_ref, s, r, k):
        return pltpu.make_async_copy(y_hbm.at[pl.ds(d_ref[0, 0, r * TOP_K + k], 1)], ybuf.at[s, k, pl.ds(r, 1)],
                                     sem.at[s])

    def issue(d_ref, s):
        def body(r, c):
            for k in range(TOP_K):
                row_copy(d_ref, s, r, k).start()
            return c
        lax.fori_loop(0, tm, body, 0)

    @pl.when(i == 0)
    def _():
        issue(dcur_ref, 0)

    @pl.when(i + 1 < pl.num_programs(0))
    def _():
        issue(dnext_ref, 1 - slot)

    def wait_row(r, c):
        for k in range(TOP_K):
            row_copy(dcur_ref, slot, r, k).wait()
        return c

    lax.fori_loop(0, tm, wait_row, 0)
    gates = gate_ref[...]
    moe = gates[:, 0:1] * ybuf[slot, 0]
    for k in range(1, TOP_K):
        moe = moe + gates[:, k:k + 1] * ybuf[slot, k]
    o_ref[...] = _layer_norm(ALPHA * x1_ref[...] + moe, g_ref[...], b_ref[...])


def _combine_call(x1, gates, dest3, g, b, y_rows):
    n = x1.shape[0]
    tm = ROW_TILE
    n_tiles = n // tm
    row = lambda i: (i, 0)
    const2 = lambda i: (0, 0)
    return pl.pallas_call(
        _combine_kernel, grid=(n_tiles,),
        in_specs=[pl.BlockSpec((tm, D_MODEL), row), pl.BlockSpec((tm, TOP_K), row),
                  pl.BlockSpec((1, 1, tm * TOP_K), lambda i: (i, 0, 0), memory_space=pltpu.SMEM),
                  pl.BlockSpec((1, 1, tm * TOP_K), lambda i: (jnp.minimum(i + 1, n_tiles - 1), 0, 0),
                               memory_space=pltpu.SMEM),
                  pl.BlockSpec((1, D_MODEL), const2), pl.BlockSpec((1, D_MODEL), const2),
                  pl.BlockSpec(memory_space=pl.ANY)],
        out_specs=pl.BlockSpec((tm, D_MODEL), row),
        out_shape=jax.ShapeDtypeStruct((n, D_MODEL), F32),
        scratch_shapes=[pltpu.VMEM((2, TOP_K, tm, D_MODEL), F32), pltpu.SemaphoreType.DMA((2,))],
        compiler_params=pltpu.CompilerParams(dimension_semantics=("arbitrary",), vmem_limit_bytes=VMEM_LIMIT),
        name="combine",
    )(x1, gates, dest3, dest3, g, b, y_rows)


def _pack_w_in(w):
    sizes = (D_A, D_A, D_A, H_I * D_I, D_I, H_I, 2 * W_B, D_CQ, D_C, DR_C)
    offs = np.cumsum((0,) + sizes)
    seg = [w[:, offs[k]:offs[k + 1]] for k in range(len(sizes))]
    z = lambda c: jnp.zeros((w.shape[0], c), w.dtype)
    return jnp.concatenate(
        [seg[0], seg[1], seg[2], seg[3], seg[4], seg[5], z(LANES - D_I - H_I), seg[6], seg[7], seg[8],
         seg[9], z(LANES - DR_C)], axis=1).astype(BF16)


def _pack_w_uq(w):
    w3 = w.reshape(D_CQ, H_C, DN_C + DR_C)
    w3 = jnp.pad(w3, ((0, 0), (0, 0), (0, QC_SLOT - DN_C - DR_C)))
    return w3.reshape(D_CQ, D_QC).astype(BF16)


def _pack_w_kc(w_uk):
    wk = jnp.pad(w_uk.reshape(D_C, H_C, DN_C), ((0, 0), (0, 0), (0, QC_SLOT - DN_C))).reshape(D_C, D_QC)
    eye = jnp.pad(jnp.eye(DR_C, dtype=w_uk.dtype), ((0, LANES - DR_C), (DN_C, QC_SLOT - DN_C - DR_C)))
    return jnp.concatenate([wk, jnp.tile(eye, (1, H_C))], axis=0).astype(BF16)


def _rope_tables(pos):
    posf = pos.astype(F32)[:, None]
    n = pos.shape[0]

    def cs(half):
        inv = ROPE_THETA ** (-jnp.arange(half, dtype=F32) / half)
        ang = posf * inv[None, :]
        return jnp.cos(ang), jnp.sin(ang)

    c32, s32 = cs(32)
    z32 = jnp.zeros_like(s32)
    c64 = jnp.tile(jnp.concatenate([c32, c32], 1), (1, 2))
    sa64 = jnp.tile(jnp.concatenate([-s32, z32], 1), (1, 2))
    sb64 = jnp.tile(jnp.concatenate([z32, s32], 1), (1, 2))
    c16, s16 = cs(16)
    z16 = jnp.zeros_like(s16)
    one = lambda c: jnp.ones((n, c), F32)
    zero = lambda c: jnp.zeros((n, c), F32)
    cq = jnp.concatenate([one(DN_C), c16, c16, one(32)], 1)
    saq = jnp.concatenate([zero(DN_C), -s16, z16, zero(32)], 1)
    sbq = jnp.concatenate([zero(DN_C), z16, s16, zero(32)], 1)
    ck = jnp.concatenate([c16, c16, one(96)], 1)
    sak = jnp.concatenate([-s16, z16, zero(96)], 1)
    sbk = jnp.concatenate([z16, s16, zero(96)], 1)
    return [c64, sa64, sb64, cq, saq, sbq, ck, sak, sbk]


def _sgu_tables(w_s, b_s, n_sample_seq, n_sample_batch):
    i = jnp.arange(SGU_CHUNK)
    w_m = jnp.where((i[None, :] // CHUNK) <= (i[:, None] // CHUNK), w_s, 0.0)
    reps = ROW_TILE // SGU_CHUNK
    bd_p = jax.vmap(lambda m: jnp.kron(jnp.eye(reps, dtype=m.dtype), m))(w_m)
    ns = n_sample_seq
    bd_s = jax.vmap(lambda m: jnp.kron(jnp.eye(n_sample_batch, dtype=m.dtype), m[:ns, :ns]))(w_m)
    bd = jnp.stack([bd_p, bd_s]).astype(BF16)
    bias_chunk = jnp.repeat(b_s.T, DG_B, axis=1)
    bias = jnp.stack([jnp.tile(bias_chunk, (reps, 1)), jnp.tile(bias_chunk[:ns], (n_sample_batch, 1))])
    return bd, bias


def _route(top_idx, rank, counts):
    n_assign = top_idx.size
    counts = counts.reshape(N_EXPERTS).astype(I32)
    padded = ((counts + MOE_TILE - 1) // MOE_TILE) * MOE_TILE
    pad_end = jnp.cumsum(padded).astype(I32)
    pad_start = pad_end - padded
    dest = jnp.take(pad_start, top_idx) + rank
    n_blocks = -(-(n_assign + N_EXPERTS * (MOE_TILE - 1)) // MOE_TILE)
    n_used = pad_end[-1] // MOE_TILE
    blk = jnp.minimum(jnp.arange(n_blocks, dtype=I32), n_used - 1) * MOE_TILE
    block_e = jnp.minimum(jnp.searchsorted(pad_end, blk, side='right'), N_EXPERTS - 1).astype(I32)
    return dest, pad_end, block_e, n_used.reshape(1), n_blocks * MOE_TILE


def _key_tiles_t(v, n_batch, k_rows, tk):
    c = v.shape[1]
    return v.reshape(n_batch, k_rows // tk, tk, c).transpose(0, 1, 3, 2).reshape(n_batch * (k_rows // tk), c, tk)


def kernel(x_prompt, x_sample, cache_a_k, cache_a_v, cache_a_kidx, cache_c_latent, cache_c_krope, w_in, mla_q_norm, mla_kv_norm, w_uq, w_uk, w_uv, sgu_ln_g, sgu_ln_b, w_spatial, b_spatial, w_out, ln1_g, ln1_b, router_w, router_b, w_gate_up, b_gate_up, w_down, b_down, ln2_g, ln2_b):
    batch, seq, _ = x_prompt.shape
    dec_batch, dec_seq, _ = x_sample.shape
    past = cache_a_k.shape[2]
    n_p, n_s = batch * seq, dec_batch * dec_seq
    assert n_p % ROW_TILE == 0 and n_s == ROW_TILE and seq % ROW_TILE == 0
    assert seq % ATT_TK == 0 and seq % ATT_TQ == 0 and dec_seq <= SAMPLE_TQ
    n_all = n_p + n_s
    n_prompt_tiles = n_p // ROW_TILE
    tab_period = seq // ROW_TILE

    pos_p = jnp.arange(seq, dtype=I32)
    pos_s = past + jnp.arange(dec_seq, dtype=I32)
    tabs = _rope_tables(jnp.concatenate([pos_p, jnp.tile(pos_s, dec_batch)]))

    n_keys_s = past + dec_seq
    l_pad_s = -(-n_keys_s // SAMPLE_TK) * SAMPLE_TK
    top_k_p = min(TOPK_MAX, seq // 4)
    top_k_s = min(TOPK_MAX, n_keys_s // 4)

    x = jnp.concatenate([x_prompt.reshape(n_p, D_MODEL), x_sample.reshape(n_s, D_MODEL)], axis=0)
    caches_p = [[] for _ in range(5)]
    caches_s = [[] for _ in range(6)]

    def with_cache(cache, new, width):
        allk = jnp.concatenate([cache.astype(BF16), new.reshape(dec_batch, dec_seq, width)], axis=1)
        allk = jnp.pad(allk, ((0, 0), (0, l_pad_s - n_keys_s), (0, 0)))
        return allk.reshape(dec_batch * l_pad_s, width)

    def sample_queries(arr):
        q = arr[n_p:].reshape(dec_batch, dec_seq, -1)
        return jnp.pad(q, ((0, 0), (0, SAMPLE_TQ - dec_seq), (0, 0))).reshape(dec_batch * SAMPLE_TQ, -1)

    def sample_rows(o):
        return o.reshape(dec_batch, SAMPLE_TQ, -1)[:, :dec_seq].reshape(n_s, -1)

    for l in range(DEPTH):
        w_packed = _pack_w_in(w_in[l])
        wuq = _pack_w_uq(w_uq[l])
        wkc = _pack_w_kc(w_uk[l])
        wuv = w_uv[l].astype(BF16)
        bd, bsb = _sgu_tables(w_spatial[l], b_spatial[l], dec_seq, dec_batch)
        (qa, ka, kab, va, vab, qi, ki, kib, wi, ob, vb, qc, lat, kr, kc, vc) = _proj_call(
            x, w_packed, tabs, mla_q_norm[l].reshape(1, -1), mla_kv_norm[l].reshape(1, -1), wuq, wkc, wuv,
            sgu_ln_g[l].reshape(1, -1), sgu_ln_b[l].reshape(1, -1), bd, bsb, n_prompt_tiles, tab_period)

        oa_p = _dsa_call(qi, wi, qa, kib, kab, _key_tiles_t(vab[:n_p], batch, seq, ATT_TK),
                         n_batch=batch, n_q=seq, q_row0=0, k_rows=seq,
                         tq=ATT_TQ, tk=ATT_TK, n_valid=seq, q_pos0=0, top_k=top_k_p)
        oc_p = _mla_call(qc, kc, _key_tiles_t(vc[:n_p], batch, seq, ATT_TK),
                         n_batch=batch, n_q=seq, q_row0=0, k_rows=seq,
                         tq=ATT_TQ, tk=ATT_TK, n_valid=seq, q_pos0=0)

        kc_c, vc_c = _kvup_call(cache_c_latent[l].reshape(dec_batch * past, D_C),
                                cache_c_krope[l].reshape(dec_batch * past, DR_C), wkc, wuv)
        ki_s = with_cache(cache_a_kidx[l], kib[n_p:], D_I)
        ka_s = with_cache(cache_a_k[l].reshape(dec_batch, past, D_A), kab[n_p:], D_A)
        va_s = with_cache(cache_a_v[l].reshape(dec_batch, past, D_A), vab[n_p:], D_A)
        kc_s = with_cache(kc_c.reshape(dec_batch, past, D_QC), kc[n_p:], D_QC)
        vc_s = with_cache(vc_c.reshape(dec_batch, past, D_VC), vc[n_p:], D_VC)
        oa_s = _dsa_call(sample_queries(qi), sample_queries(wi), sample_queries(qa), ki_s, ka_s,
                         _key_tiles_t(va_s, dec_batch, l_pad_s, SAMPLE_TK),
                         n_batch=dec_batch, n_q=SAMPLE_TQ, q_row0=0, k_rows=l_pad_s,
                         tq=SAMPLE_TQ, tk=SAMPLE_TK, n_valid=n_keys_s, q_pos0=past, top_k=top_k_s)
        oc_s = _mla_call(sample_queries(qc), kc_s, _key_tiles_t(vc_s, dec_batch, l_pad_s, SAMPLE_TK),
                         n_batch=dec_batch, n_q=SAMPLE_TQ, q_row0=0, k_rows=l_pad_s,
                         tq=SAMPLE_TQ, tk=SAMPLE_TK, n_valid=n_keys_s, q_pos0=past)

        oa = jnp.concatenate([oa_p, sample_rows(oa_s)], axis=0)
        oc = jnp.concatenate([oc_p, sample_rows(oc_s)], axis=0)
        x1, top_idx, gates, rank, counts = _outproj_call(
            oa, ob, oc, x, w_out[l].astype(BF16), ln1_g[l].reshape(1, -1), ln1_b[l].reshape(1, -1),
            router_w[l].astype(BF16), router_b[l].reshape(1, -1))

        dest, pad_end, block_e, n_used, n_pad = _route(top_idx, rank, counts)
        dest3 = dest.reshape(n_all // ROW_TILE, 1, ROW_TILE * TOP_K)
        xg = _dispatch_call(pad_end, x1, dest3, n_pad)
        y_rows = _moe_call(block_e, n_used, xg, w_gate_up[l], b_gate_up[l], w_down[l], b_down[l])
        x = _combine_call(x1, gates, dest3, ln2_g[l].reshape(1, -1), ln2_b[l].reshape(1, -1), y_rows)

        for dst, arr, shp in ((caches_p[0], ka, (H_A, DH_A)), (caches_p[1], va, (H_A, DH_A)),
                              (caches_p[2], ki, (D_I,)), (caches_p[3], lat, (D_C,)), (caches_p[4], kr, (DR_C,))):
            dst.append(arr[:n_p].reshape((batch, seq) + shp))
        for dst, arr, shp in ((caches_s[0], ka, (H_A, DH_A)), (caches_s[1], va, (H_A, DH_A)),
                              (caches_s[2], ki, (D_I,)), (caches_s[3], lat, (D_C,)), (caches_s[4], kr, (DR_C,)),
                              (caches_s[5], vb, (W_B,))):
            dst.append(arr[n_p:].reshape((dec_batch, dec_seq) + shp))

    y_prompt = x[:n_p].reshape(batch, seq, D_MODEL)
    y_sample = x[n_p:].reshape(dec_batch, dec_seq, D_MODEL)
    return (y_prompt, y_sample) + tuple(jnp.stack(c) for c in caches_p) + tuple(jnp.stack(c) for c in caches_s)
```

```python
import functools

import numpy as np
import jax
import jax.numpy as jnp
from jax import lax
from jax.experimental import pallas as pl
from jax.experimental.pallas import tpu as pltpu

F32 = jnp.float32
BF16 = jnp.bfloat16
I32 = jnp.int32
I16 = jnp.int16

D_MODEL = 1024
DEPTH = 2
CHUNK = 64
ROPE_THETA = 10000.0
LN_EPS = 1e-5
RMS_EPS = 1e-6
H_A, DH_A = 6, 64
H_I, D_I = 8, 64
TOPK_MAX = 256
G_B, DG_B = 4, 64
W_B = G_B * DG_B
SGU_CHUNK = 128
H_C, DN_C, DR_C, DV_C = 6, 64, 32, 64
D_CQ, D_C = 256, 256
MLA_SCALE = (DN_C + DR_C) ** -0.5
LOG2E = float(np.log2(np.e))
D_A = H_A * DH_A
D_VC = H_C * DV_C
N_EXPERTS = 32
TOP_K = 4
D_FF = 1024
SWIGLU_LIMIT = 7.0
SWIGLU_ALPHA = 1.702
ALPHA = (2 * DEPTH) ** 0.25

LANES = 128
SUBLANES = 8
PACKED_ROWS = 16
HALF_RANGE = 1 << 15
ROW_TILE = 256
MOE_TILE = 256
ATT_TQ = 256
ATT_TK = 512
SAMPLE_TQ = 128
SAMPLE_TK = 384
VMEM_LIMIT = 48 * 1024 * 1024

C_QA, C_KA, C_VA, C_QI, C_KIW, C_ZU, C_ZV, C_CQ, C_CKV, C_KR = (
    0, 384, 768, 1152, 1664, 1792, 2048, 2304, 2560, 2816)
IN_PACKED = 2944
QC_SLOT = 128
D_QC = H_C * QC_SLOT

NEG_BIG = -1e30
KEY_NEG_INF = -2139095041
INT_MIN = -2147483648


def _nt_dot(a, b):
    return lax.dot_general(a, b, (((1,), (1,)), ((), ())), preferred_element_type=F32)


def _rope128(x, c, sa, sb, half):
    return x * c + pltpu.roll(x, LANES - half, 1) * sa + pltpu.roll(x, half, 1) * sb


def _proj_kernel(x_ref, w_ref, c64_ref, sa64_ref, sb64_ref, cq_ref, saq_ref, sbq_ref,
                 ck_ref, sak_ref, sbk_ref, qn_ref, kvn_ref, wuq_ref, wkc_ref, wuv_ref,
                 sgug_ref, sgub_ref, bd_ref, bsb_ref,
                 qa_o, ka_o, kab_o, va_o, vab_o, qi_o, ki_o, kib_o, wi_o, ob_o, vb_o,
                 qc_o, lat_o, kr_o, kc_o, vc_o):
    h = jnp.dot(x_ref[...].astype(BF16), w_ref[...], preferred_element_type=F32)
    c64, sa64, sb64 = c64_ref[...], sa64_ref[...], sb64_ref[...]

    def rope64(col):
        return _rope128(h[:, col:col + LANES], c64, sa64, sb64, 32)

    for c in range(D_A // LANES):
        qa_o[:, c * LANES:(c + 1) * LANES] = (rope64(C_QA + c * LANES) * (DH_A ** -0.5 * LOG2E)).astype(BF16)
        ka = rope64(C_KA + c * LANES)
        ka_o[:, c * LANES:(c + 1) * LANES] = ka
        kab_o[:, c * LANES:(c + 1) * LANES] = ka.astype(BF16)
    va = h[:, C_VA:C_VA + D_A]
    va_o[...] = va
    vab_o[...] = va.astype(BF16)
    for c in range(H_I * D_I // LANES):
        qi_o[:, c * LANES:(c + 1) * LANES] = (rope64(C_QI + c * LANES) * (D_I ** -0.5)).astype(BF16)
    kiw = rope64(C_KIW)
    ki_o[...] = kiw[:, :D_I]
    kib_o[...] = kiw[:, :D_I].astype(BF16)
    wi_o[...] = h[:, C_KIW:C_KIW + LANES] * (H_I ** -0.5)

    z = h[:, C_ZU:C_ZU + 2 * W_B]
    z = 0.5 * z * (1.0 + jnp.tanh(np.sqrt(2.0 / np.pi) * (z + 0.044715 * (z * z * z))))
    u = z[:, :W_B]
    v = z[:, W_B:]
    mu = jnp.mean(v, axis=-1, keepdims=True)
    var = jnp.mean(jnp.square(v - mu), axis=-1, keepdims=True)
    v = (v - mu) * lax.rsqrt(var + LN_EPS) * sgug_ref[...] + sgub_ref[...]
    vb_o[...] = v
    v16 = v.astype(BF16)
    lane = lax.broadcasted_iota(I32, (v.shape[0], LANES), 1)
    for w in range(W_B // LANES):
        vw = v16[:, w * LANES:(w + 1) * LANES]
        m0 = jnp.dot(bd_ref[0, 2 * w], vw, preferred_element_type=F32)
        m1 = jnp.dot(bd_ref[0, 2 * w + 1], vw, preferred_element_type=F32)
        mixed = jnp.where(lane < DG_B, m0, m1) + bsb_ref[0, :, w * LANES:(w + 1) * LANES]
        ob_o[:, w * LANES:(w + 1) * LANES] = (u[:, w * LANES:(w + 1) * LANES] * mixed).astype(BF16)

    cq = h[:, C_CQ:C_CQ + D_CQ]
    cq = cq * lax.rsqrt(jnp.mean(jnp.square(cq), axis=-1, keepdims=True) + RMS_EPS) * qn_ref[...]
    q = jnp.dot(cq.astype(BF16), wuq_ref[...], preferred_element_type=F32)
    cqt, saq, sbq = cq_ref[...], saq_ref[...], sbq_ref[...]
    for hh in range(H_C):
        qs = _rope128(q[:, hh * QC_SLOT:(hh + 1) * QC_SLOT], cqt, saq, sbq, DR_C // 2)
        qc_o[:, hh * QC_SLOT:(hh + 1) * QC_SLOT] = (qs * (MLA_SCALE * LOG2E)).astype(BF16)
    ckv = h[:, C_CKV:C_CKV + D_C]
    lat = ckv * lax.rsqrt(jnp.mean(jnp.square(ckv), axis=-1, keepdims=True) + RMS_EPS) * kvn_ref[...]
    lat_o[...] = lat
    kr = _rope128(h[:, C_KR:C_KR + LANES], ck_ref[...], sak_ref[...], sbk_ref[...], DR_C // 2)
    kr_o[...] = kr[:, :DR_C]
    lat16 = lat.astype(BF16)
    kc = (jnp.dot(lat16, wkc_ref[:D_C, :], preferred_element_type=F32)
          + jnp.dot(kr.astype(BF16), wkc_ref[D_C:, :], preferred_element_type=F32))
    kc_o[...] = kc.astype(BF16)
    vc_o[...] = jnp.dot(lat16, wuv_ref[...], preferred_element_type=F32).astype(BF16)


def _proj_call(x, w_packed, tabs, qn, kvn, wuq, wkc, wuv, sgug, sgub, bd, bsb, n_prompt_tiles, tab_period):
    n = x.shape[0]
    tm = ROW_TILE
    grid = (n // tm,)

    def row(i):
        return (i, 0)

    def const2(i):
        return (0, 0)

    def tab(i):
        return (jnp.where(i < n_prompt_tiles, i % tab_period, tab_period), 0)

    def grp(i):
        return (jnp.where(i < n_prompt_tiles, 0, 1), 0, 0, 0)

    def grp3(i):
        return (jnp.where(i < n_prompt_tiles, 0, 1), 0, 0)

    in_specs = [pl.BlockSpec((tm, D_MODEL), row), pl.BlockSpec((D_MODEL, IN_PACKED), const2)]
    in_specs += [pl.BlockSpec((tm, LANES), tab)] * 9
    in_specs += [pl.BlockSpec((1, D_CQ), const2), pl.BlockSpec((1, D_C), const2),
                 pl.BlockSpec((D_CQ, D_QC), const2), pl.BlockSpec((D_C + LANES, D_QC), const2),
                 pl.BlockSpec((D_C, D_VC), const2),
                 pl.BlockSpec((1, W_B), const2), pl.BlockSpec((1, W_B), const2),
                 pl.BlockSpec((1, G_B, tm, tm), grp), pl.BlockSpec((1, tm, W_B), grp3)]
    outs = [(D_A, BF16), (D_A, F32), (D_A, BF16), (D_A, F32), (D_A, BF16), (H_I * D_I, BF16),
            (D_I, F32), (D_I, BF16), (LANES, F32), (W_B, BF16), (W_B, F32),
            (D_QC, BF16), (D_C, F32), (DR_C, F32), (D_QC, BF16), (D_VC, BF16)]
    out_shape = [jax.ShapeDtypeStruct((n, c), d) for c, d in outs]
    out_specs = [pl.BlockSpec((tm, c), row) for c, _ in outs]
    return pl.pallas_call(
        _proj_kernel, grid=grid, in_specs=in_specs, out_specs=out_specs, out_shape=out_shape,
        compiler_params=pltpu.CompilerParams(dimension_semantics=("arbitrary",), vmem_limit_bytes=VMEM_LIMIT),
        name="proj",
    )(x, w_packed, *tabs, qn, kvn, wuq, wkc, wuv, sgug, sgub, bd, bsb)


def _kvup_kernel(lat_ref, kr_ref, wkc_ref, wuv_ref, kc_o, vc_o):
    lat16 = lat_ref[...].astype(BF16)
    kc = (jnp.dot(lat16, wkc_ref[:D_C, :], preferred_element_type=F32)
          + jnp.dot(kr_ref[...].astype(BF16), wkc_ref[D_C:D_C + DR_C, :], preferred_element_type=F32))
    kc_o[...] = kc.astype(BF16)
    vc_o[...] = jnp.dot(lat16, wuv_ref[...], preferred_element_type=F32).astype(BF16)


def _kvup_call(lat, kr, wkc, wuv):
    n = lat.shape[0]
    tm = 1024
    return pl.pallas_call(
        _kvup_kernel, grid=(n // tm,),
        in_specs=[pl.BlockSpec((tm, D_C), lambda i: (i, 0)), pl.BlockSpec((tm, DR_C), lambda i: (i, 0)),
                  pl.BlockSpec((D_C + LANES, D_QC), lambda i: (0, 0)), pl.BlockSpec((D_C, D_VC), lambda i: (0, 0))],
        out_specs=[pl.BlockSpec((tm, D_QC), lambda i: (i, 0)), pl.BlockSpec((tm, D_VC), lambda i: (i, 0))],
        out_shape=[jax.ShapeDtypeStruct((n, D_QC), BF16), jax.ShapeDtypeStruct((n, D_VC), BF16)],
        compiler_params=pltpu.CompilerParams(dimension_semantics=("arbitrary",), vmem_limit_bytes=VMEM_LIMIT),
        name="kvup",
    )(lat, kr, wkc, wuv)


def _key_bounds(j, tq, tk, n_valid, q_pos0):
    q_first = q_pos0 + j * tq
    kmax = jnp.minimum(((q_first + tq - 1) // CHUNK + 1) * CHUNK, n_valid)
    n_tiles = (kmax + tk - 1) // tk
    n_full = jnp.minimum((q_first // CHUNK + 1) * CHUNK, n_valid) // tk
    qpos = q_first + lax.broadcasted_iota(I32, (1, tq), 1)
    bound = jnp.minimum((qpos // CHUNK + 1) * CHUNK, n_valid)
    return n_tiles, n_full, bound


def _flash_init(m_s, l_s, acc_s):
    m_s[...] = jnp.full(m_s.shape, NEG_BIG, F32)
    l_s[...] = jnp.zeros(l_s.shape, F32)
    acc_s[...] = jnp.zeros(acc_s.shape, F32)


def _flash_update(hh, s_t, v_t, m_s, l_s):
    m_old = m_s[hh]
    m_new = jnp.maximum(m_old, jnp.max(s_t, axis=0, keepdims=True))
    p = jnp.exp2(s_t - m_new)
    a = jnp.exp2(m_old - m_new)
    l_s[hh] = a * l_s[hh] + jnp.sum(p, axis=0, keepdims=True)
    m_s[hh] = m_new
    return a, jnp.dot(v_t, p.astype(BF16), preferred_element_type=F32)


def _flash_pair_accumulate(w, first_rows, upd0, upd1, acc_s):
    (a0, pv0), (a1, pv1) = upd0, upd1
    acc_s[w] = acc_s[w] * jnp.where(first_rows, a0, a1) + jnp.where(first_rows, pv0, pv1)


def _flash_heads(scores, values, n_heads, first_rows, m_s, l_s, acc_s):
    ahead = 3
    pending = [scores(hh) for hh in range(min(ahead, n_heads))]
    upd = []
    for hh in range(n_heads):
        s_cur = pending.pop(0)
        if hh + ahead < n_heads:
            pending.append(scores(hh + ahead))
        upd.append(_flash_update(hh, s_cur, values(hh // 2), m_s, l_s))
        if hh % 2:
            _flash_pair_accumulate(hh // 2, first_rows, upd[hh - 1], upd[hh], acc_s)


def _flash_finish(o_ref, first_rows, l_s, acc_s):
    for w in range(acc_s.shape[0]):
        l_sel = jnp.where(first_rows, l_s[2 * w], l_s[2 * w + 1])
        o_ref[:, w * LANES:(w + 1) * LANES] = (acc_s[w] / l_sel).T.astype(o_ref.dtype)


def _flash_scratch(n_heads, tq):
    return [pltpu.VMEM((n_heads, 1, tq), F32), pltpu.VMEM((n_heads, 1, tq), F32),
            pltpu.VMEM((n_heads // 2, LANES, tq), F32)]


def _dsa_kernel(qi_ref, wi_ref, qa_ref, ki_ref, ka_ref, vat_ref, o_ref, key_ref, k16_ref, m_s, l_s, acc_s, *,
                tq, tk, n_valid, q_pos0, top_k):
    j = pl.program_id(1)
    n_tiles, _, bound = _key_bounds(j, tq, tk, n_valid, q_pos0)
    row_tk = lax.broadcasted_iota(I32, (tk, tq), 0)
    row8 = lax.broadcasted_iota(I32, (SUBLANES, tq), 0)

    qi = qi_ref[...]
    q_stack = jnp.concatenate([qi[:, hh * D_I:(hh + 1) * D_I] for hh in range(H_I)], axis=0)
    wi_t = wi_ref[...].T
    wis = [wi_t[D_I + hh:D_I + hh + 1, :] for hh in range(H_I)]

    def score_tile(kt, carry):
        k0 = pl.multiple_of(kt * tk, tk)
        d = _nt_dot(ki_ref[pl.ds(k0, tk), :], q_stack)
        s = wis[0] * jnp.maximum(d[:, 0:tq], 0.0)
        for hh in range(1, H_I):
            s = s + wis[hh] * jnp.maximum(d[:, hh * tq:(hh + 1) * tq], 0.0)
        bits = pltpu.bitcast(s, I32)
        key = jnp.where(bits < 0, bits ^ 0x7FFFFFFF, bits)
        key = jnp.where(key == -1, 0, key)
        key = jnp.where(k0 + row_tk < bound, key, KEY_NEG_INF)
        key_ref[kt] = key
        k16_ref[kt] = (key >> 16).astype(I16)
        return carry

    lax.fori_loop(0, n_tiles, score_tile, 0)

    def count(pred):
        def body(kt, acc):
            for r in range(tk // SUBLANES):
                kk = key_ref[kt, r * SUBLANES:(r + 1) * SUBLANES, :]
                acc = acc + jnp.where(pred(kk, kt * tk + r * SUBLANES), 1.0, 0.0)
            return acc
        acc = lax.fori_loop(0, n_tiles, body, jnp.zeros((SUBLANES, tq), F32))
        return jnp.sum(acc, axis=0, keepdims=True)

    def count16_ge(cand):
        c16 = cand.astype(I16)

        def body(kt, acc):
            for r in range(tk // PACKED_ROWS):
                kk = k16_ref[kt, r * PACKED_ROWS:(r + 1) * PACKED_ROWS, :]
                acc = acc + jnp.where(kk >= c16, jnp.int16(1), jnp.int16(0))
            return acc
        acc = lax.fori_loop(0, n_tiles, body, jnp.zeros((PACKED_ROWS, tq), I16))
        return jnp.sum(acc.astype(F32), axis=0, keepdims=True)

    kf = float(top_k)
    c0 = count16_ge(jnp.zeros((1, tq), I32))
    nonneg = c0 >= kf
    t_hi0 = jnp.where(nonneg, 0, -HALF_RANGE).astype(I32)
    cnt0 = jnp.where(nonneg, c0, (n_tiles * tk).astype(F32))

    def hi_body(it, carry):
        t_hi, cnt_t = carry
        cand = t_hi | lax.shift_left(jnp.int32(1), 14 - it)
        cnt = count16_ge(cand)
        ok = cnt >= kf
        return jnp.where(ok, cand, t_hi), jnp.where(ok, cnt, cnt_t)

    t_hi, cnt_hi = lax.fori_loop(0, 15, hi_body, (t_hi0, cnt0))
    need_lo = kf - count16_ge(t_hi + 1)
    cnt_class = cnt_hi - (kf - need_lo)

    def lo_tile(kt, carry):
        kk = key_ref[kt]
        lo = (kk & 0xFFFF) - HALF_RANGE
        k16_ref[kt] = jnp.where((kk >> 16) == t_hi, lo, -HALF_RANGE).astype(I16)
        return carry

    lax.fori_loop(0, n_tiles, lo_tile, 0)

    def lo_cond(carry):
        it, _, cnt_t = carry
        return (it < 16) & (jnp.max(jnp.abs(cnt_t - need_lo)) > 0.5)

    def lo_body(carry):
        it, t_lo, cnt_t = carry
        cand = t_lo | lax.shift_left(jnp.int32(1), 15 - it)
        cnt = count16_ge(cand - HALF_RANGE)
        ok = cnt >= need_lo
        return it + 1, jnp.where(ok, cand, t_lo), jnp.where(ok, cnt, cnt_t)

    _, t_lo, cnt_lo = lax.while_loop(lo_cond, lo_body, (jnp.int32(0), jnp.zeros((1, tq), I32), cnt_class))
    t = lax.shift_left(t_hi, 16) | t_lo
    inexact = jnp.max(jnp.abs(cnt_lo - need_lo)) > 0.5

    def tie_path():
        c_gt = count(lambda kk, base: kk > t)
        c_eq = count(lambda kk, base: kk == t)
        need = kf - c_gt
        no_limit = jnp.full((1, tq), 1 << 30, I32)

        def search():
            def jb(it, jj):
                cand = jj | lax.shift_left(jnp.int32(1), 12 - it)
                f = count(lambda kk, base: (kk == t) & (base + row8 < cand))
                return jnp.where(f < need, cand, jj)
            return lax.fori_loop(0, 13, jb, jnp.zeros((1, tq), I32))

        excess = jnp.max(jnp.where((c_eq > need) & (t > KEY_NEG_INF), 1.0, 0.0)) > 0.5
        return lax.cond(excess, search, lambda: no_limit)

    j_last = lax.cond(inexact, tie_path, lambda: jnp.full((1, tq), 1 << 30, I32))

    def bias_tile(kt, carry):
        kk = key_ref[kt]
        kpos = kt * tk + row_tk
        sel = ((kk > t) | ((kk == t) & (kpos <= j_last))) & (kk != KEY_NEG_INF)
        key_ref[kt] = pltpu.bitcast(jnp.where(sel, 0.0, NEG_BIG).astype(F32), I32)
        return carry

    lax.fori_loop(0, n_tiles, bias_tile, 0)

    lane = lax.broadcasted_iota(I32, (tq, LANES), 1)
    first_rows = lax.broadcasted_iota(I32, (LANES, tq), 0) < DH_A
    qa = qa_ref[...]
    half_masks = [jnp.where(lane < DH_A, 1.0, 0.0).astype(BF16), jnp.where(lane < DH_A, 0.0, 1.0).astype(BF16)]
    qhs = [qa[:, (hh // 2) * LANES:(hh // 2 + 1) * LANES] * half_masks[hh % 2] for hh in range(H_A)]
    _flash_init(m_s, l_s, acc_s)

    def att_tile(kt, carry):
        k0 = pl.multiple_of(kt * tk, tk)
        bias = pltpu.bitcast(key_ref[kt], F32)

        def scores(hh):
            return _nt_dot(ka_ref[pl.ds(k0, tk), (hh // 2) * LANES:(hh // 2 + 1) * LANES], qhs[hh]) + bias

        _flash_heads(scores, lambda w: vat_ref[kt, w * LANES:(w + 1) * LANES, :], H_A, first_rows, m_s, l_s, acc_s)
        return carry

    lax.fori_loop(0, n_tiles, att_tile, 0)
    _flash_finish(o_ref, first_rows, l_s, acc_s)


def _dsa_call(qi, wi, qa, ki, ka, vat, *, n_batch, n_q, q_row0, k_rows, tq, tk, n_valid, q_pos0, top_k):
    nq_blocks = n_q // tq
    qb0 = q_row0 // tq
    n_kt = k_rows // tk
    kern = functools.partial(_dsa_kernel, tq=tq, tk=tk, n_valid=n_valid, q_pos0=q_pos0, top_k=top_k)

    def qmap(b, j):
        return (qb0 + b * nq_blocks + j, 0)

    def kmap(b, j):
        return (b, 0)

    return pl.pallas_call(
        kern, grid=(n_batch, nq_blocks),
        in_specs=[pl.BlockSpec((tq, H_I * D_I), qmap), pl.BlockSpec((tq, LANES), qmap), pl.BlockSpec((tq, D_A), qmap),
                  pl.BlockSpec((k_rows, D_I), kmap), pl.BlockSpec((k_rows, D_A), kmap),
                  pl.BlockSpec((n_kt, D_A, tk), lambda b, j: (b, 0, 0))],
        out_specs=pl.BlockSpec((tq, D_A), lambda b, j: (b * nq_blocks + j, 0)),
        out_shape=jax.ShapeDtypeStruct((n_batch * n_q, D_A), BF16),
        scratch_shapes=[pltpu.VMEM((n_kt, tk, tq), I32), pltpu.VMEM((n_kt, tk, tq), I16)] + _flash_scratch(H_A, tq),
        compiler_params=pltpu.CompilerParams(dimension_semantics=("arbitrary", "arbitrary"),
                                             vmem_limit_bytes=VMEM_LIMIT),
        name="dsa",
    )(qi, wi, qa, ki, ka, vat)


def _mla_kernel(qc_ref, kc_ref, vct_ref, o_ref, m_s, l_s, acc_s, *, tq, tk, n_valid, q_pos0):
    j = pl.program_id(1)
    n_tiles, n_full, bound = _key_bounds(j, tq, tk, n_valid, q_pos0)
    row_tk = lax.broadcasted_iota(I32, (tk, tq), 0)
    first_rows = lax.broadcasted_iota(I32, (LANES, tq), 0) < DV_C
    qc = qc_ref[...]
    qhs = [qc[:, hh * QC_SLOT:(hh + 1) * QC_SLOT] for hh in range(H_C)]
    _flash_init(m_s, l_s, acc_s)

    def tile(kt, carry, masked):
        k0 = pl.multiple_of(kt * tk, tk)
        if masked:
            bias = jnp.where(k0 + row_tk < bound, 0.0, NEG_BIG)

        def scores(hh):
            s_t = _nt_dot(kc_ref[pl.ds(k0, tk), hh * QC_SLOT:(hh + 1) * QC_SLOT], qhs[hh])
            return s_t + bias if masked else s_t

        _flash_heads(scores, lambda w: vct_ref[kt, w * LANES:(w + 1) * LANES, :], H_C, first_rows, m_s, l_s, acc_s)
        return carry

    lax.fori_loop(0, n_full, functools.partial(tile, masked=False), 0)
    lax.fori_loop(n_full, n_tiles, functools.partial(tile, masked=True), 0)
    _flash_finish(o_ref, first_rows, l_s, acc_s)


def _mla_call(qc, kc, vct, *, n_batch, n_q, q_row0, k_rows, tq, tk, n_valid, q_pos0):
    nq_blocks = n_q // tq
    qb0 = q_row0 // tq
    n_kt = k_rows // tk
    kern = functools.partial(_mla_kernel, tq=tq, tk=tk, n_valid=n_valid, q_pos0=q_pos0)
    return pl.pallas_call(
        kern, grid=(n_batch, nq_blocks),
        in_specs=[pl.BlockSpec((tq, D_QC), lambda b, j: (qb0 + b * nq_blocks + j, 0)),
                  pl.BlockSpec((k_rows, D_QC), lambda b, j: (b, 0)),
                  pl.BlockSpec((n_kt, D_VC, tk), lambda b, j: (b, 0, 0))],
        out_specs=pl.BlockSpec((tq, D_VC), lambda b, j: (b * nq_blocks + j, 0)),
        out_shape=jax.ShapeDtypeStruct((n_batch * n_q, D_VC), BF16),
        scratch_shapes=_flash_scratch(H_C, tq),
        compiler_params=pltpu.CompilerParams(dimension_semantics=("arbitrary", "arbitrary"),
                                             vmem_limit_bytes=VMEM_LIMIT),
        name="mla",
    )(qc, kc, vct)


def _layer_norm(y, g, b):
    mu = jnp.mean(y, axis=-1, keepdims=True)
    var = jnp.mean(jnp.square(y - mu), axis=-1, keepdims=True)
    return (y - mu) * lax.rsqrt(var + LN_EPS) * g + b


def _outproj_kernel(oa_ref, ob_ref, oc_ref, x_ref, w_ref, g_ref, b_ref, rw_ref, rb_ref, tri_ref,
                    x1_o, idx_o, gate_o, rank_o, cnt_o, run_s):
    @pl.when(pl.program_id(0) == 0)
    def _():
        run_s[...] = jnp.zeros_like(run_s)

    mix = (jnp.dot(oa_ref[...], w_ref[0:D_A, :], preferred_element_type=F32)
           + jnp.dot(ob_ref[...], w_ref[D_A:D_A + W_B, :], preferred_element_type=F32)
           + jnp.dot(oc_ref[...], w_ref[D_A + W_B:, :], preferred_element_type=F32))
    x1 = _layer_norm(ALPHA * x_ref[...] + mix, g_ref[...], b_ref[...])
    x1_o[...] = x1
    lg = jnp.dot(x1.astype(BF16), rw_ref[...], preferred_element_type=F32) + rb_ref[...]
    e = lax.broadcasted_iota(I32, lg.shape, 1).astype(F32)
    vals, hots = [], []
    for k in range(TOP_K):
        m = jnp.max(lg, axis=1, keepdims=True)
        idx = jnp.min(jnp.where(lg == m, e, float(N_EXPERTS)), axis=1, keepdims=True)
        vals.append(m)
        hots.append(jnp.where(e == idx, 1.0, 0.0))
        idx_o[:, k:k + 1] = idx.astype(I32)
        lg = jnp.where(e == idx, -jnp.inf, lg)
    ex = [jnp.exp(vv - vals[0]) for vv in vals]
    den = ex[0] + ex[1] + ex[2] + ex[3]
    for k in range(TOP_K):
        gate_o[:, k:k + 1] = ex[k] / den
    per_token = hots[0] + hots[1] + hots[2] + hots[3]
    before = run_s[...] + jnp.dot(tri_ref[...], per_token.astype(BF16), preferred_element_type=F32)
    for k in range(TOP_K):
        rank_o[:, k:k + 1] = jnp.sum(hots[k] * before, axis=1, keepdims=True).astype(I32)
    run_s[...] = run_s[...] + jnp.sum(per_token, axis=0, keepdims=True)
    cnt_o[...] = run_s[...]


def _outproj_call(oa, ob, oc, x, w_out, g, b, rw, rb):
    n = x.shape[0]
    tm = ROW_TILE
    row = lambda i: (i, 0)
    const2 = lambda i: (0, 0)
    tri = jnp.tril(jnp.ones((tm, tm), BF16), -1)
    return pl.pallas_call(
        _outproj_kernel, grid=(n // tm,),
        in_specs=[pl.BlockSpec((tm, D_A), row), pl.BlockSpec((tm, W_B), row), pl.BlockSpec((tm, D_VC), row),
                  pl.BlockSpec((tm, D_MODEL), row), pl.BlockSpec((D_MODEL, D_MODEL), const2),
                  pl.BlockSpec((1, D_MODEL), const2), pl.BlockSpec((1, D_MODEL), const2),
                  pl.BlockSpec((D_MODEL, N_EXPERTS), const2), pl.BlockSpec((1, N_EXPERTS), const2),
                  pl.BlockSpec((tm, tm), const2)],
        out_specs=[pl.BlockSpec((tm, D_MODEL), row), pl.BlockSpec((tm, TOP_K), row), pl.BlockSpec((tm, TOP_K), row),
                   pl.BlockSpec((tm, TOP_K), row), pl.BlockSpec((1, N_EXPERTS), const2)],
        out_shape=[jax.ShapeDtypeStruct((n, D_MODEL), F32), jax.ShapeDtypeStruct((n, TOP_K), I32),
                   jax.ShapeDtypeStruct((n, TOP_K), F32), jax.ShapeDtypeStruct((n, TOP_K), I32),
                   jax.ShapeDtypeStruct((1, N_EXPERTS), F32)],
        scratch_shapes=[pltpu.VMEM((1, N_EXPERTS), F32)],
        compiler_params=pltpu.CompilerParams(dimension_semantics=("arbitrary",), vmem_limit_bytes=VMEM_LIMIT),
        name="outproj",
    )(oa, ob, oc, x, w_out, g, b, rw, rb, tri)


def _dispatch_kernel(pe_ref, x_ref, dest_ref, xg_hbm, zero_s, sem):
    i = pl.program_id(0)
    tm = x_ref.shape[0]

    @pl.when(i == 0)
    def _():
        zero_s[...] = jnp.zeros_like(zero_s)

        def fill(e):
            start = pl.multiple_of(pe_ref[e] - MOE_TILE, MOE_TILE)
            return pltpu.make_async_copy(zero_s, xg_hbm.at[pl.ds(start, MOE_TILE)], sem)

        def nonempty(e):
            return pe_ref[e] > (pe_ref[e - 1] if e else 0)

        n_blocks = xg_hbm.shape[0] // MOE_TILE
        first_tail = (n_blocks * MOE_TILE - N_EXPERTS * (MOE_TILE - 1)) // MOE_TILE

        def tail(b):
            return pltpu.make_async_copy(zero_s, xg_hbm.at[pl.ds(b * MOE_TILE, MOE_TILE)], sem)

        def unused(b):
            return b * MOE_TILE >= pe_ref[N_EXPERTS - 1]

        for e in range(N_EXPERTS):
            pl.when(nonempty(e))(lambda e=e: fill(e).start())
        for b in range(first_tail, n_blocks):
            pl.when(unused(b))(lambda b=b: tail(b).start())
        for e in range(N_EXPERTS):
            pl.when(nonempty(e))(lambda e=e: fill(e).wait())
        for b in range(first_tail, n_blocks):
            pl.when(unused(b))(lambda b=b: tail(b).wait())

    def row_copy(r, k):
        return pltpu.make_async_copy(x_ref.at[pl.ds(r, 1)], xg_hbm.at[pl.ds(dest_ref[0, 0, r * TOP_K + k], 1)], sem)

    def start_row(r, c):
        for k in range(TOP_K):
            row_copy(r, k).start()
        return c

    def wait_row(r, c):
        for k in range(TOP_K):
            row_copy(r, k).wait()
        return c

    lax.fori_loop(0, tm, start_row, 0)
    lax.fori_loop(0, tm, wait_row, 0)


def _dispatch_call(pad_end, x1, dest3, n_pad):
    n = x1.shape[0]
    tm = ROW_TILE
    grid_spec = pltpu.PrefetchScalarGridSpec(
        num_scalar_prefetch=1, grid=(n // tm,),
        in_specs=[pl.BlockSpec((tm, D_MODEL), lambda i, pe: (i, 0)),
                  pl.BlockSpec((1, 1, tm * TOP_K), lambda i, pe: (i, 0, 0), memory_space=pltpu.SMEM)],
        out_specs=pl.BlockSpec(memory_space=pl.ANY),
        scratch_shapes=[pltpu.VMEM((MOE_TILE, D_MODEL), F32), pltpu.SemaphoreType.DMA(())])
    return pl.pallas_call(
        _dispatch_kernel, grid_spec=grid_spec,
        out_shape=jax.ShapeDtypeStruct((n_pad, D_MODEL), F32),
        compiler_params=pltpu.CompilerParams(dimension_semantics=("arbitrary",), vmem_limit_bytes=VMEM_LIMIT,
                                             has_side_effects=True),
        name="dispatch",
    )(pad_end, x1, dest3)


def _moe_kernel(be_ref, nb_ref, xg_ref, wgu_ref, bgu_ref, wdn_ref, bdn_ref, y_ref, wgu_s, wdn_s):
    i = pl.program_id(0)

    @pl.when(i < nb_ref[0])
    def _():
        e = be_ref[i]
        prev = be_ref[jnp.maximum(i - 1, 0)]

        @pl.when((i == 0) | (e != prev))
        def _():
            wgu_s[...] = wgu_ref[0, 0].astype(BF16)
            wdn_s[...] = wdn_ref[0, 0].astype(BF16)

        h = jnp.dot(xg_ref[...].astype(BF16), wgu_s[...], preferred_element_type=F32) + bgu_ref[0, 0]
        gate = jnp.minimum(h[:, :D_FF], SWIGLU_LIMIT)
        lin = jnp.clip(h[:, D_FF:], -SWIGLU_LIMIT, SWIGLU_LIMIT)
        act = (lin + 1.0) * (gate * jax.nn.sigmoid(SWIGLU_ALPHA * gate))
        y_ref[...] = jnp.dot(act.astype(BF16), wdn_s[...], preferred_element_type=F32) + bdn_ref[0, 0]

    @pl.when(i >= nb_ref[0])
    def _():
        y_ref[...] = jnp.zeros_like(y_ref)


def _moe_call(block_e, n_used, xg, w_gu, b_gu, w_dn, b_dn, layer):
    n_pad = xg.shape[0]
    tm = MOE_TILE

    def used(i, be, nb):
        return (jnp.minimum(i, nb[0] - 1), 0)

    def expert(i, be, nb):
        return (layer, be[i], 0, 0)

    grid_spec = pltpu.PrefetchScalarGridSpec(
        num_scalar_prefetch=2, grid=(n_pad // tm,),
        in_specs=[pl.BlockSpec((tm, D_MODEL), used),
                  pl.BlockSpec((1, 1, D_MODEL, 2 * D_FF), expert), pl.BlockSpec((1, 1, 1, 2 * D_FF), expert),
                  pl.BlockSpec((1, 1, D_FF, D_MODEL), expert), pl.BlockSpec((1, 1, 1, D_MODEL), expert)],
        out_specs=pl.BlockSpec((tm, D_MODEL), lambda i, be, nb: (i, 0)),
        scratch_shapes=[pltpu.VMEM((D_MODEL, 2 * D_FF), BF16), pltpu.VMEM((D_FF, D_MODEL), BF16)])
    return pl.pallas_call(
        _moe_kernel, grid_spec=grid_spec,
        out_shape=jax.ShapeDtypeStruct((n_pad, D_MODEL), F32),
        compiler_params=pltpu.CompilerParams(dimension_semantics=("arbitrary",),
                                             vmem_limit_bytes=56 * 1024 * 1024),
        name="moe",
    )(block_e, n_used, xg, w_gu, b_gu.reshape(DEPTH, N_EXPERTS, 1, 2 * D_FF), w_dn,
      b_dn.reshape(DEPTH, N_EXPERTS, 1, D_MODEL))


def _combine_kernel(x1_ref, gate_ref, dcur_ref, dnext_ref, g_ref, b_ref, y_hbm, o_ref, ybuf, sem):
    i = pl.program_id(0)
    tm = x1_ref.shape[0]
    slot = i % 2

    def row_copy(d_ref, s, r, k):
        return pltpu.make_async_copy(y_hbm.at[pl.ds(d_ref[0, 0, r * TOP_K + k], 1)], ybuf.at[s, k, pl.ds(r, 1)],
                                     sem.at[s])

    def issue(d_ref, s):
        def body(r, c):
            for k in range(TOP_K):
                row_copy(d_ref, s, r, k).start()
            return c
        lax.fori_loop(0, tm, body, 0)

    @pl.when(i == 0)
    def _():
        issue(dcur_ref, 0)

    @pl.when(i + 1 < pl.num_programs(0))
    def _():
        issue(dnext_ref, 1 - slot)

    def wait_row(r, c):
        for k in range(TOP_K):
            row_copy(dcur_ref, slot, r, k).wait()
        return c

    lax.fori_loop(0, tm, wait_row, 0)
    gates = gate_ref[...]
    moe = gates[:, 0:1] * ybuf[slot, 0]
    for k in range(1, TOP_K):
        moe = moe + gates[:, k:k + 1] * ybuf[slot, k]
    o_ref[...] = _layer_norm(ALPHA * x1_ref[...] + moe, g_ref[...], b_ref[...])


def _combine_call(x1, gates, dest3, g, b, y_rows):
    n = x1.shape[0]
    tm = ROW_TILE
    n_tiles = n // tm
    row = lambda i: (i, 0)
    const2 = lambda i: (0, 0)
    return pl.pallas_call(
        _combine_kernel, grid=(n_tiles,),
        in_specs=[pl.BlockSpec((tm, D_MODEL), row), pl.BlockSpec((tm, TOP_K), row),
                  pl.BlockSpec((1, 1, tm * TOP_K), lambda i: (i, 0, 0), memory_space=pltpu.SMEM),
                  pl.BlockSpec((1, 1, tm * TOP_K), lambda i: (jnp.minimum(i + 1, n_tiles - 1), 0, 0),
                               memory_space=pltpu.SMEM),
                  pl.BlockSpec((1, D_MODEL), const2), pl.BlockSpec((1, D_MODEL), const2),
                  pl.BlockSpec(memory_space=pl.ANY)],
        out_specs=pl.BlockSpec((tm, D_MODEL), row),
        out_shape=jax.ShapeDtypeStruct((n, D_MODEL), F32),
        scratch_shapes=[pltpu.VMEM((2, TOP_K, tm, D_MODEL), F32), pltpu.SemaphoreType.DMA((2,))],
        compiler_params=pltpu.CompilerParams(dimension_semantics=("arbitrary",), vmem_limit_bytes=VMEM_LIMIT),
        name="combine",
    )(x1, gates, dest3, dest3, g, b, y_rows)


def _pack_w_in(w):
    sizes = (D_A, D_A, D_A, H_I * D_I, D_I, H_I, 2 * W_B, D_CQ, D_C, DR_C)
    offs = np.cumsum((0,) + sizes)
    seg = [w[:, offs[k]:offs[k + 1]] for k in range(len(sizes))]
    z = lambda c: jnp.zeros((w.shape[0], c), w.dtype)
    return jnp.concatenate(
        [seg[0], seg[1], seg[2], seg[3], seg[4], seg[5], z(LANES - D_I - H_I), seg[6], seg[7], seg[8],
         seg[9], z(LANES - DR_C)], axis=1).astype(BF16)


def _pack_w_uq(w):
    w3 = w.reshape(D_CQ, H_C, DN_C + DR_C)
    w3 = jnp.pad(w3, ((0, 0), (0, 0), (0, QC_SLOT - DN_C - DR_C)))
    return w3.reshape(D_CQ, D_QC).astype(BF16)


def _pack_w_kc(w_uk):
    wk = jnp.pad(w_uk.reshape(D_C, H_C, DN_C), ((0, 0), (0, 0), (0, QC_SLOT - DN_C))).reshape(D_C, D_QC)
    eye = jnp.pad(jnp.eye(DR_C, dtype=w_uk.dtype), ((0, LANES - DR_C), (DN_C, QC_SLOT - DN_C - DR_C)))
    return jnp.concatenate([wk, jnp.tile(eye, (1, H_C))], axis=0).astype(BF16)


def _rope_tables(pos):
    posf = pos.astype(F32)[:, None]
    n = pos.shape[0]

    def cs(half):
        inv = ROPE_THETA ** (-jnp.arange(half, dtype=F32) / half)
        ang = posf * inv[None, :]
        return jnp.cos(ang), jnp.sin(ang)

    c32, s32 = cs(32)
    z32 = jnp.zeros_like(s32)
    c64 = jnp.tile(jnp.concatenate([c32, c32], 1), (1, 2))
    sa64 = jnp.tile(jnp.concatenate([-s32, z32], 1), (1, 2))
    sb64 = jnp.tile(jnp.concatenate([z32, s32], 1), (1, 2))
    c16, s16 = cs(16)
    z16 = jnp.zeros_like(s16)
    one = lambda c: jnp.ones((n, c), F32)
    zero = lambda c: jnp.zeros((n, c), F32)
    cq = jnp.concatenate([one(DN_C), c16, c16, one(32)], 1)
    saq = jnp.concatenate([zero(DN_C), -s16, z16, zero(32)], 1)
    sbq = jnp.concatenate([zero(DN_C), z16, s16, zero(32)], 1)
    ck = jnp.concatenate([c16, c16, one(96)], 1)
    sak = jnp.concatenate([-s16, z16, zero(96)], 1)
    sbk = jnp.concatenate([z16, s16, zero(96)], 1)
    return [c64, sa64, sb64, cq, saq, sbq, ck, sak, sbk]


def _sgu_tables(w_s, b_s, n_sample_seq, n_sample_batch):
    i = jnp.arange(SGU_CHUNK)
    w_m = jnp.where((i[None, :] // CHUNK) <= (i[:, None] // CHUNK), w_s, 0.0)
    reps = ROW_TILE // SGU_CHUNK
    bd_p = jax.vmap(lambda m: jnp.kron(jnp.eye(reps, dtype=m.dtype), m))(w_m)
    ns = n_sample_seq
    bd_s = jax.vmap(lambda m: jnp.kron(jnp.eye(n_sample_batch, dtype=m.dtype), m[:ns, :ns]))(w_m)
    bd = jnp.stack([bd_p, bd_s]).astype(BF16)
    bias_chunk = jnp.repeat(b_s.T, DG_B, axis=1)
    bias = jnp.stack([jnp.tile(bias_chunk, (reps, 1)), jnp.tile(bias_chunk[:ns], (n_sample_batch, 1))])
    return bd, bias


def _route(top_idx, rank, counts):
    n_assign = top_idx.size
    counts = counts.reshape(N_EXPERTS).astype(I32)
    padded = ((counts + MOE_TILE - 1) // MOE_TILE) * MOE_TILE
    pad_end = jnp.cumsum(padded).astype(I32)
    pad_start = pad_end - padded
    dest = jnp.take(pad_start, top_idx) + rank
    n_blocks = -(-(n_assign + N_EXPERTS * (MOE_TILE - 1)) // MOE_TILE)
    n_used = pad_end[-1] // MOE_TILE
    blk = jnp.minimum(jnp.arange(n_blocks, dtype=I32), n_used - 1) * MOE_TILE
    block_e = jnp.minimum(jnp.sum(pad_end[None, :] <= blk[:, None], axis=1), N_EXPERTS - 1).astype(I32)
    return dest, pad_end, block_e, n_used.reshape(1), n_blocks * MOE_TILE


def _key_tiles_t(v, n_batch, k_rows, tk):
    c = v.shape[1]
    return v.reshape(n_batch, k_rows // tk, tk, c).transpose(0, 1, 3, 2).reshape(n_batch * (k_rows // tk), c, tk)


def kernel(x_prompt, x_sample, cache_a_k, cache_a_v, cache_a_kidx, cache_c_latent, cache_c_krope, w_in, mla_q_norm, mla_kv_norm, w_uq, w_uk, w_uv, sgu_ln_g, sgu_ln_b, w_spatial, b_spatial, w_out, ln1_g, ln1_b, router_w, router_b, w_gate_up, b_gate_up, w_down, b_down, ln2_g, ln2_b):
    batch, seq, _ = x_prompt.shape
    dec_batch, dec_seq, _ = x_sample.shape
    past = cache_a_k.shape[2]
    n_p, n_s = batch * seq, dec_batch * dec_seq
    assert n_p % ROW_TILE == 0 and n_s == ROW_TILE and seq % ROW_TILE == 0
    assert seq % ATT_TK == 0 and seq % ATT_TQ == 0 and dec_seq <= SAMPLE_TQ
    n_all = n_p + n_s
    n_prompt_tiles = n_p // ROW_TILE
    tab_period = seq // ROW_TILE

    pos_p = jnp.arange(seq, dtype=I32)
    pos_s = past + jnp.arange(dec_seq, dtype=I32)
    tabs = _rope_tables(jnp.concatenate([pos_p, jnp.tile(pos_s, dec_batch)]))

    n_keys_s = past + dec_seq
    l_pad_s = -(-n_keys_s // SAMPLE_TK) * SAMPLE_TK
    top_k_p = min(TOPK_MAX, seq // 4)
    top_k_s = min(TOPK_MAX, n_keys_s // 4)

    x = jnp.concatenate([x_prompt.reshape(n_p, D_MODEL), x_sample.reshape(n_s, D_MODEL)], axis=0)
    caches_p = [[] for _ in range(5)]
    caches_s = [[] for _ in range(6)]

    def with_cache(cache, new, width):
        allk = jnp.concatenate([cache.astype(BF16), new.reshape(dec_batch, dec_seq, width)], axis=1)
        allk = jnp.pad(allk, ((0, 0), (0, l_pad_s - n_keys_s), (0, 0)))
        return allk.reshape(dec_batch * l_pad_s, width)

    def sample_queries(arr):
        q = arr[n_p:].reshape(dec_batch, dec_seq, -1)
        return jnp.pad(q, ((0, 0), (0, SAMPLE_TQ - dec_seq), (0, 0))).reshape(dec_batch * SAMPLE_TQ, -1)

    def sample_rows(o):
        return o.reshape(dec_batch, SAMPLE_TQ, -1)[:, :dec_seq].reshape(n_s, -1)

    for l in range(DEPTH):
        w_packed = _pack_w_in(w_in[l])
        wuq = _pack_w_uq(w_uq[l])
        wkc = _pack_w_kc(w_uk[l])
        wuv = w_uv[l].astype(BF16)
        bd, bsb = _sgu_tables(w_spatial[l], b_spatial[l], dec_seq, dec_batch)
        (qa, ka, kab, va, vab, qi, ki, kib, wi, ob, vb, qc, lat, kr, kc, vc) = _proj_call(
            x, w_packed, tabs, mla_q_norm[l].reshape(1, -1), mla_kv_norm[l].reshape(1, -1), wuq, wkc, wuv,
            sgu_ln_g[l].reshape(1, -1), sgu_ln_b[l].reshape(1, -1), bd, bsb, n_prompt_tiles, tab_period)

        oa_p = _dsa_call(qi, wi, qa, kib, kab, _key_tiles_t(vab[:n_p], batch, seq, ATT_TK),
                         n_batch=batch, n_q=seq, q_row0=0, k_rows=seq,
                         tq=ATT_TQ, tk=ATT_TK, n_valid=seq, q_pos0=0, top_k=top_k_p)
        oc_p = _mla_call(qc, kc, _key_tiles_t(vc[:n_p], batch, seq, ATT_TK),
                         n_batch=batch, n_q=seq, q_row0=0, k_rows=seq,
                         tq=ATT_TQ, tk=ATT_TK, n_valid=seq, q_pos0=0)

        kc_c, vc_c = _kvup_call(cache_c_latent[l].reshape(dec_batch * past, D_C),
                                cache_c_krope[l].reshape(dec_batch * past, DR_C), wkc, wuv)
        ki_s = with_cache(cache_a_kidx[l], kib[n_p:], D_I)
        ka_s = with_cache(cache_a_k[l].reshape(dec_batch, past, D_A), kab[n_p:], D_A)
        va_s = with_cache(cache_a_v[l].reshape(dec_batch, past, D_A), vab[n_p:], D_A)
        kc_s = with_cache(kc_c.reshape(dec_batch, past, D_QC), kc[n_p:], D_QC)
        vc_s = with_cache(vc_c.reshape(dec_batch, past, D_VC), vc[n_p:], D_VC)
        oa_s = _dsa_call(sample_queries(qi), sample_queries(wi), sample_queries(qa), ki_s, ka_s,
                         _key_tiles_t(va_s, dec_batch, l_pad_s, SAMPLE_TK),
                         n_batch=dec_batch, n_q=SAMPLE_TQ, q_row0=0, k_rows=l_pad_s,
                         tq=SAMPLE_TQ, tk=SAMPLE_TK, n_valid=n_keys_s, q_pos0=past, top_k=top_k_s)
        oc_s = _mla_call(sample_queries(qc), kc_s, _key_tiles_t(vc_s, dec_batch, l_pad_s, SAMPLE_TK),
                         n_batch=dec_batch, n_q=SAMPLE_TQ, q_row0=0, k_rows=l_pad_s,
                         tq=SAMPLE_TQ, tk=SAMPLE_TK, n_valid=n_keys_s, q_pos0=past)

        oa = jnp.concatenate([oa_p, sample_rows(oa_s)], axis=0)
        oc = jnp.concatenate([oc_p, sample_rows(oc_s)], axis=0)
        x1, top_idx, gates, rank, counts = _outproj_call(
            oa, ob, oc, x, w_out[l].astype(BF16), ln1_g[l].reshape(1, -1), ln1_b[l].reshape(1, -1),
            router_w[l].astype(BF16), router_b[l].reshape(1, -1))

        dest, pad_end, block_e, n_used, n_pad = _route(top_idx, rank, counts)
        dest3 = dest.reshape(n_all // ROW_TILE, 1, ROW_TILE * TOP_K)
        xg = _dispatch_call(pad_end, x1, dest3, n_pad)
        y_rows = _moe_call(block_e, n_used, xg, w_gate_up, b_gate_up, w_down, b_down, l)
        x = _combine_call(x1, gates, dest3, ln2_g[l].reshape(1, -1), ln2_b[l].reshape(1, -1), y_rows)

        for dst, arr, shp in ((caches_p[0], ka, (H_A, DH_A)), (caches_p[1], va, (H_A, DH_A)),
                              (caches_p[2], ki, (D_I,)), (caches_p[3], lat, (D_C,)), (caches_p[4], kr, (DR_C,))):
            dst.append(arr[:n_p].reshape((batch, seq) + shp))
        for dst, arr, shp in ((caches_s[0], ka, (H_A, DH_A)), (caches_s[1], va, (H_A, DH_A)),
                              (caches_s[2], ki, (D_I,)), (caches_s[3], lat, (D_C,)), (caches_s[4], kr, (DR_C,)),
                              (caches_s[5], vb, (W_B,))):
            dst.append(arr[n_p:].reshape((dec_batch, dec_seq) + shp))

    y_prompt = x[:n_p].reshape(batch, seq, D_MODEL)
    y_sample = x[n_p:].reshape(dec_batch, dec_seq, D_MODEL)
    return (y_prompt, y_sample) + tuple(jnp.stack(c) for c in caches_p) + tuple(jnp.stack(c) for c in caches_s)
```

```python
import functools

import numpy as np
import jax
import jax.numpy as jnp
from jax import lax
from jax.experimental import pallas as pl
from jax.experimental.pallas import tpu as pltpu

F32 = jnp.float32
BF16 = jnp.bfloat16
I32 = jnp.int32
I16 = jnp.int16

D_MODEL = 1024
DEPTH = 2
CHUNK = 64
ROPE_THETA = 10000.0
LN_EPS = 1e-5
RMS_EPS = 1e-6
H_A, DH_A = 6, 64
H_I, D_I = 8, 64
TOPK_MAX = 256
G_B, DG_B = 4, 64
W_B = G_B * DG_B
SGU_CHUNK = 128
H_C, DN_C, DR_C, DV_C = 6, 64, 32, 64
D_CQ, D_C = 256, 256
MLA_SCALE = (DN_C + DR_C) ** -0.5
LOG2E = float(np.log2(np.e))
D_A = H_A * DH_A
D_VC = H_C * DV_C
N_EXPERTS = 32
TOP_K = 4
D_FF = 1024
SWIGLU_LIMIT = 7.0
SWIGLU_ALPHA = 1.702
ALPHA = (2 * DEPTH) ** 0.25

LANES = 128
SUBLANES = 8
PACKED_ROWS = 16
HALF_RANGE = 1 << 15
ROW_TILE = 256
MOE_TILE = 256
ATT_TQ = 256
ATT_TK = 512
SAMPLE_TQ = 128
SAMPLE_TK = 384
VMEM_LIMIT = 48 * 1024 * 1024

C_QA, C_KA, C_VA, C_QI, C_KIW, C_ZU, C_ZV, C_CQ, C_CKV, C_KR = (
    0, 384, 768, 1152, 1664, 1792, 2048, 2304, 2560, 2816)
IN_PACKED = 2944
QC_SLOT = 128
D_QC = H_C * QC_SLOT

NEG_BIG = -1e30
KEY_NEG_INF = -2139095041
INT_MIN = -2147483648


def _nt_dot(a, b):
    return lax.dot_general(a, b, (((1,), (1,)), ((), ())), preferred_element_type=F32)


def _rope128(x, c, sa, sb, half):
    return x * c + pltpu.roll(x, LANES - half, 1) * sa + pltpu.roll(x, half, 1) * sb


def _proj_kernel(x_ref, w_ref, c64_ref, sa64_ref, sb64_ref, cq_ref, saq_ref, sbq_ref,
                 ck_ref, sak_ref, sbk_ref, qn_ref, kvn_ref, wuq_ref, wkc_ref, wuv_ref,
                 sgug_ref, sgub_ref, bd_ref, bsb_ref,
                 qa_o, ka_o, kab_o, va_o, vab_o, qi_o, ki_o, kib_o, wi_o, ob_o, vb_o,
                 qc_o, lat_o, kr_o, kc_o, vc_o):
    h = jnp.dot(x_ref[...].astype(BF16), w_ref[...], preferred_element_type=F32)
    c64, sa64, sb64 = c64_ref[...], sa64_ref[...], sb64_ref[...]

    def rope64(col):
        return _rope128(h[:, col:col + LANES], c64, sa64, sb64, 32)

    for c in range(D_A // LANES):
        qa_o[:, c * LANES:(c + 1) * LANES] = (rope64(C_QA + c * LANES) * (DH_A ** -0.5 * LOG2E)).astype(BF16)
        ka = rope64(C_KA + c * LANES)
        ka_o[:, c * LANES:(c + 1) * LANES] = ka
        kab_o[:, c * LANES:(c + 1) * LANES] = ka.astype(BF16)
    va = h[:, C_VA:C_VA + D_A]
    va_o[...] = va
    vab_o[...] = va.astype(BF16)
    for c in range(H_I * D_I // LANES):
        qi_o[:, c * LANES:(c + 1) * LANES] = (rope64(C_QI + c * LANES) * (D_I ** -0.5)).astype(BF16)
    kiw = rope64(C_KIW)
    ki_o[...] = kiw[:, :D_I]
    kib_o[...] = kiw[:, :D_I].astype(BF16)
    wi_o[...] = h[:, C_KIW:C_KIW + LANES] * (H_I ** -0.5)

    z = h[:, C_ZU:C_ZU + 2 * W_B]
    z = 0.5 * z * (1.0 + jnp.tanh(np.sqrt(2.0 / np.pi) * (z + 0.044715 * (z * z * z))))
    u = z[:, :W_B]
    v = z[:, W_B:]
    mu = jnp.mean(v, axis=-1, keepdims=True)
    var = jnp.mean(jnp.square(v - mu), axis=-1, keepdims=True)
    v = (v - mu) * lax.rsqrt(var + LN_EPS) * sgug_ref[...] + sgub_ref[...]
    vb_o[...] = v
    v16 = v.astype(BF16)
    lane = lax.broadcasted_iota(I32, (v.shape[0], LANES), 1)
    for w in range(W_B // LANES):
        vw = v16[:, w * LANES:(w + 1) * LANES]
        m0 = jnp.dot(bd_ref[0, 2 * w], vw, preferred_element_type=F32)
        m1 = jnp.dot(bd_ref[0, 2 * w + 1], vw, preferred_element_type=F32)
        mixed = jnp.where(lane < DG_B, m0, m1) + bsb_ref[0, :, w * LANES:(w + 1) * LANES]
        ob_o[:, w * LANES:(w + 1) * LANES] = (u[:, w * LANES:(w + 1) * LANES] * mixed).astype(BF16)

    cq = h[:, C_CQ:C_CQ + D_CQ]
    cq = cq * lax.rsqrt(jnp.mean(jnp.square(cq), axis=-1, keepdims=True) + RMS_EPS) * qn_ref[...]
    q = jnp.dot(cq.astype(BF16), wuq_ref[...], preferred_element_type=F32)
    cqt, saq, sbq = cq_ref[...], saq_ref[...], sbq_ref[...]
    for hh in range(H_C):
        qs = _rope128(q[:, hh * QC_SLOT:(hh + 1) * QC_SLOT], cqt, saq, sbq, DR_C // 2)
        qc_o[:, hh * QC_SLOT:(hh + 1) * QC_SLOT] = (qs * (MLA_SCALE * LOG2E)).astype(BF16)
    ckv = h[:, C_CKV:C_CKV + D_C]
    lat = ckv * lax.rsqrt(jnp.mean(jnp.square(ckv), axis=-1, keepdims=True) + RMS_EPS) * kvn_ref[...]
    lat_o[...] = lat
    kr = _rope128(h[:, C_KR:C_KR + LANES], ck_ref[...], sak_ref[...], sbk_ref[...], DR_C // 2)
    kr_o[...] = kr[:, :DR_C]
    lat16 = lat.astype(BF16)
    kc = (jnp.dot(lat16, wkc_ref[:D_C, :], preferred_element_type=F32)
          + jnp.dot(kr.astype(BF16), wkc_ref[D_C:, :], preferred_element_type=F32))
    kc_o[...] = kc.astype(BF16)
    vc_o[...] = jnp.dot(lat16, wuv_ref[...], preferred_element_type=F32).astype(BF16)


def _proj_call(x, w_packed, tabs, qn, kvn, wuq, wkc, wuv, sgug, sgub, bd, bsb, n_prompt_tiles, tab_period):
    n = x.shape[0]
    tm = ROW_TILE
    grid = (n // tm,)

    def row(i):
        return (i, 0)

    def const2(i):
        return (0, 0)

    def tab(i):
        return (jnp.where(i < n_prompt_tiles, i % tab_period, tab_period), 0)

    def grp(i):
        return (jnp.where(i < n_prompt_tiles, 0, 1), 0, 0, 0)

    def grp3(i):
        return (jnp.where(i < n_prompt_tiles, 0, 1), 0, 0)

    in_specs = [pl.BlockSpec((tm, D_MODEL), row), pl.BlockSpec((D_MODEL, IN_PACKED), const2)]
    in_specs += [pl.BlockSpec((tm, LANES), tab)] * 9
    in_specs += [pl.BlockSpec((1, D_CQ), const2), pl.BlockSpec((1, D_C), const2),
                 pl.BlockSpec((D_CQ, D_QC), const2), pl.BlockSpec((D_C + LANES, D_QC), const2),
                 pl.BlockSpec((D_C, D_VC), const2),
                 pl.BlockSpec((1, W_B), const2), pl.BlockSpec((1, W_B), const2),
                 pl.BlockSpec((1, G_B, tm, tm), grp), pl.BlockSpec((1, tm, W_B), grp3)]
    outs = [(D_A, BF16), (D_A, F32), (D_A, BF16), (D_A, F32), (D_A, BF16), (H_I * D_I, BF16),
            (D_I, F32), (D_I, BF16), (LANES, F32), (W_B, BF16), (W_B, F32),
            (D_QC, BF16), (D_C, F32), (DR_C, F32), (D_QC, BF16), (D_VC, BF16)]
    out_shape = [jax.ShapeDtypeStruct((n, c), d) for c, d in outs]
    out_specs = [pl.BlockSpec((tm, c), row) for c, _ in outs]
    return pl.pallas_call(
        _proj_kernel, grid=grid, in_specs=in_specs, out_specs=out_specs, out_shape=out_shape,
        compiler_params=pltpu.CompilerParams(dimension_semantics=("arbitrary",), vmem_limit_bytes=VMEM_LIMIT),
        name="proj",
    )(x, w_packed, *tabs, qn, kvn, wuq, wkc, wuv, sgug, sgub, bd, bsb)


def _kvup_kernel(lat_ref, kr_ref, wkc_ref, wuv_ref, kc_o, vc_o):
    lat16 = lat_ref[...].astype(BF16)
    kc = (jnp.dot(lat16, wkc_ref[:D_C, :], preferred_element_type=F32)
          + jnp.dot(kr_ref[...].astype(BF16), wkc_ref[D_C:D_C + DR_C, :], preferred_element_type=F32))
    kc_o[...] = kc.astype(BF16)
    vc_o[...] = jnp.dot(lat16, wuv_ref[...], preferred_element_type=F32).astype(BF16)


def _kvup_call(lat, kr, wkc, wuv):
    n = lat.shape[0]
    tm = 1024
    return pl.pallas_call(
        _kvup_kernel, grid=(n // tm,),
        in_specs=[pl.BlockSpec((tm, D_C), lambda i: (i, 0)), pl.BlockSpec((tm, DR_C), lambda i: (i, 0)),
                  pl.BlockSpec((D_C + LANES, D_QC), lambda i: (0, 0)), pl.BlockSpec((D_C, D_VC), lambda i: (0, 0))],
        out_specs=[pl.BlockSpec((tm, D_QC), lambda i: (i, 0)), pl.BlockSpec((tm, D_VC), lambda i: (i, 0))],
        out_shape=[jax.ShapeDtypeStruct((n, D_QC), BF16), jax.ShapeDtypeStruct((n, D_VC), BF16)],
        compiler_params=pltpu.CompilerParams(dimension_semantics=("arbitrary",), vmem_limit_bytes=VMEM_LIMIT),
        name="kvup",
    )(lat, kr, wkc, wuv)


def _key_bounds(j, tq, tk, n_valid, q_pos0):
    q_first = q_pos0 + j * tq
    kmax = jnp.minimum(((q_first + tq - 1) // CHUNK + 1) * CHUNK, n_valid)
    n_tiles = (kmax + tk - 1) // tk
    n_full = jnp.minimum((q_first // CHUNK + 1) * CHUNK, n_valid) // tk
    qpos = q_first + lax.broadcasted_iota(I32, (1, tq), 1)
    bound = jnp.minimum((qpos // CHUNK + 1) * CHUNK, n_valid)
    return n_tiles, n_full, bound


def _flash_init(m_s, l_s, acc_s):
    m_s[...] = jnp.full(m_s.shape, NEG_BIG, F32)
    l_s[...] = jnp.zeros(l_s.shape, F32)
    acc_s[...] = jnp.zeros(acc_s.shape, F32)


def _flash_update(hh, s_t, v_t, m_s, l_s):
    m_old = m_s[hh]
    m_new = jnp.maximum(m_old, jnp.max(s_t, axis=0, keepdims=True))
    p = jnp.exp2(s_t - m_new)
    a = jnp.exp2(m_old - m_new)
    l_s[hh] = a * l_s[hh] + jnp.sum(p, axis=0, keepdims=True)
    m_s[hh] = m_new
    return a, jnp.dot(v_t, p.astype(BF16), preferred_element_type=F32)


def _flash_pair_accumulate(w, first_rows, upd0, upd1, acc_s):
    (a0, pv0), (a1, pv1) = upd0, upd1
    acc_s[w] = acc_s[w] * jnp.where(first_rows, a0, a1) + jnp.where(first_rows, pv0, pv1)


def _flash_heads(scores, values, n_heads, first_rows, m_s, l_s, acc_s):
    s_all = [scores(hh) for hh in range(n_heads)]
    upd = []
    for hh in range(n_heads):
        upd.append(_flash_update(hh, s_all[hh], values(hh // 2), m_s, l_s))
        if hh % 2:
            _flash_pair_accumulate(hh // 2, first_rows, upd[hh - 1], upd[hh], acc_s)


def _flash_finish(o_ref, first_rows, l_s, acc_s):
    for w in range(acc_s.shape[0]):
        l_sel = jnp.where(first_rows, l_s[2 * w], l_s[2 * w + 1])
        o_ref[:, w * LANES:(w + 1) * LANES] = (acc_s[w] / l_sel).T.astype(o_ref.dtype)


def _flash_scratch(n_heads, tq):
    return [pltpu.VMEM((n_heads, 1, tq), F32), pltpu.VMEM((n_heads, 1, tq), F32),
            pltpu.VMEM((n_heads // 2, LANES, tq), F32)]


def _dsa_kernel(qi_ref, wi_ref, qa_ref, ki_ref, ka_ref, vat_ref, o_ref, key_ref, k16_ref, m_s, l_s, acc_s, *,
                tq, tk, n_valid, q_pos0, top_k):
    j = pl.program_id(1)
    n_tiles, _, bound = _key_bounds(j, tq, tk, n_valid, q_pos0)
    row_tk = lax.broadcasted_iota(I32, (tk, tq), 0)
    row8 = lax.broadcasted_iota(I32, (SUBLANES, tq), 0)

    qi = qi_ref[...]
    q_stack = jnp.concatenate([qi[:, hh * D_I:(hh + 1) * D_I] for hh in range(H_I)], axis=0)
    wi_t = wi_ref[...].T
    wis = [wi_t[D_I + hh:D_I + hh + 1, :] for hh in range(H_I)]

    def score_tile(kt, carry):
        k0 = pl.multiple_of(kt * tk, tk)
        d = _nt_dot(ki_ref[pl.ds(k0, tk), :], q_stack)
        s = wis[0] * jnp.maximum(d[:, 0:tq], 0.0)
        for hh in range(1, H_I):
            s = s + wis[hh] * jnp.maximum(d[:, hh * tq:(hh + 1) * tq], 0.0)
        bits = pltpu.bitcast(s, I32)
        key = jnp.where(bits < 0, bits ^ 0x7FFFFFFF, bits)
        key = jnp.where(key == -1, 0, key)
        key = jnp.where(k0 + row_tk < bound, key, KEY_NEG_INF)
        key_ref[kt] = key
        k16_ref[kt] = (key >> 16).astype(I16)
        return carry

    lax.fori_loop(0, n_tiles, score_tile, 0)

    def count(pred):
        def body(kt, acc):
            for r in range(tk // SUBLANES):
                kk = key_ref[kt, r * SUBLANES:(r + 1) * SUBLANES, :]
                acc = acc + jnp.where(pred(kk, kt * tk + r * SUBLANES), 1.0, 0.0)
            return acc
        acc = lax.fori_loop(0, n_tiles, body, jnp.zeros((SUBLANES, tq), F32))
        return jnp.sum(acc, axis=0, keepdims=True)

    def count16_ge(cand):
        c16 = cand.astype(I16)

        def body(kt, acc):
            for r in range(tk // PACKED_ROWS):
                kk = k16_ref[kt, r * PACKED_ROWS:(r + 1) * PACKED_ROWS, :]
                acc = acc + jnp.where(kk >= c16, jnp.int16(1), jnp.int16(0))
            return acc
        acc = lax.fori_loop(0, n_tiles, body, jnp.zeros((PACKED_ROWS, tq), I16))
        return jnp.sum(acc.astype(F32), axis=0, keepdims=True)

    kf = float(top_k)
    c0 = count16_ge(jnp.zeros((1, tq), I32))
    nonneg = c0 >= kf
    t_hi0 = jnp.where(nonneg, 0, -HALF_RANGE).astype(I32)
    cnt0 = jnp.where(nonneg, c0, (n_tiles * tk).astype(F32))

    def hi_body(it, carry):
        t_hi, cnt_t = carry
        cand = t_hi | lax.shift_left(jnp.int32(1), 14 - it)
        cnt = count16_ge(cand)
        ok = cnt >= kf
        return jnp.where(ok, cand, t_hi), jnp.where(ok, cnt, cnt_t)

    t_hi, cnt_hi = lax.fori_loop(0, 15, hi_body, (t_hi0, cnt0))
    need_lo = kf - count16_ge(t_hi + 1)
    cnt_class = cnt_hi - (kf - need_lo)

    def lo_tile(kt, carry):
        kk = key_ref[kt]
        lo = (kk & 0xFFFF) - HALF_RANGE
        k16_ref[kt] = jnp.where((kk >> 16) == t_hi, lo, -HALF_RANGE).astype(I16)
        return carry

    lax.fori_loop(0, n_tiles, lo_tile, 0)

    def lo_cond(carry):
        it, _, cnt_t = carry
        return (it < 16) & (jnp.max(jnp.abs(cnt_t - need_lo)) > 0.5)

    def lo_body(carry):
        it, t_lo, cnt_t = carry
        cand = t_lo | lax.shift_left(jnp.int32(1), 15 - it)
        cnt = count16_ge(cand - HALF_RANGE)
        ok = cnt >= need_lo
        return it + 1, jnp.where(ok, cand, t_lo), jnp.where(ok, cnt, cnt_t)

    _, t_lo, cnt_lo = lax.while_loop(lo_cond, lo_body, (jnp.int32(0), jnp.zeros((1, tq), I32), cnt_class))
    t = lax.shift_left(t_hi, 16) | t_lo
    inexact = jnp.max(jnp.abs(cnt_lo - need_lo)) > 0.5

    @pl.when(inexact)
    def _():
        c_gt = count(lambda kk, base: kk > t)
        c_eq = count(lambda kk, base: kk == t)
        need = kf - c_gt

        @pl.when(jnp.max(jnp.where((c_eq > need) & (t > KEY_NEG_INF), 1.0, 0.0)) > 0.5)
        def _():
            def jb(it, jj):
                cand = jj | lax.shift_left(jnp.int32(1), 12 - it)
                f = count(lambda kk, base: (kk == t) & (base + row8 < cand))
                return jnp.where(f < need, cand, jj)
            j_last = lax.fori_loop(0, 13, jb, jnp.zeros((1, tq), I32))

            def drop_tile(kt, carry):
                kk = key_ref[kt]
                key_ref[kt] = jnp.where((kk == t) & (kt * tk + row_tk > j_last), KEY_NEG_INF, kk)
                return carry

            lax.fori_loop(0, n_tiles, drop_tile, 0)

    t_member = jnp.maximum(t, KEY_NEG_INF + 1)

    lane = lax.broadcasted_iota(I32, (tq, LANES), 1)
    first_rows = lax.broadcasted_iota(I32, (LANES, tq), 0) < DH_A
    qa = qa_ref[...]
    half_masks = [jnp.where(lane < DH_A, 1.0, 0.0).astype(BF16), jnp.where(lane < DH_A, 0.0, 1.0).astype(BF16)]
    qhs = [qa[:, (hh // 2) * LANES:(hh // 2 + 1) * LANES] * half_masks[hh % 2] for hh in range(H_A)]
    _flash_init(m_s, l_s, acc_s)

    def att_tile(kt, carry):
        k0 = pl.multiple_of(kt * tk, tk)
        bias = jnp.where(key_ref[kt] >= t_member, 0.0, NEG_BIG)

        def scores(hh):
            return _nt_dot(ka_ref[pl.ds(k0, tk), (hh // 2) * LANES:(hh // 2 + 1) * LANES], qhs[hh]) + bias

        _flash_heads(scores, lambda w: vat_ref[kt, w * LANES:(w + 1) * LANES, :], H_A, first_rows, m_s, l_s, acc_s)
        return carry

    lax.fori_loop(0, n_tiles, att_tile, 0)
    _flash_finish(o_ref, first_rows, l_s, acc_s)


def _dsa_call(qi, wi, qa, ki, ka, vat, *, n_batch, n_q, q_row0, k_rows, tq, tk, n_valid, q_pos0, top_k):
    nq_blocks = n_q // tq
    qb0 = q_row0 // tq
    n_kt = k_rows // tk
    kern = functools.partial(_dsa_kernel, tq=tq, tk=tk, n_valid=n_valid, q_pos0=q_pos0, top_k=top_k)

    def qmap(b, j):
        return (qb0 + b * nq_blocks + j, 0)

    def kmap(b, j):
        return (b, 0)

    return pl.pallas_call(
        kern, grid=(n_batch, nq_blocks),
        in_specs=[pl.BlockSpec((tq, H_I * D_I), qmap), pl.BlockSpec((tq, LANES), qmap), pl.BlockSpec((tq, D_A), qmap),
                  pl.BlockSpec((k_rows, D_I), kmap), pl.BlockSpec((k_rows, D_A), kmap),
                  pl.BlockSpec((n_kt, D_A, tk), lambda b, j: (b, 0, 0))],
        out_specs=pl.BlockSpec((tq, D_A), lambda b, j: (b * nq_blocks + j, 0)),
        out_shape=jax.ShapeDtypeStruct((n_batch * n_q, D_A), BF16),
        scratch_shapes=[pltpu.VMEM((n_kt, tk, tq), I32), pltpu.VMEM((n_kt, tk, tq), I16)] + _flash_scratch(H_A, tq),
        compiler_params=pltpu.CompilerParams(dimension_semantics=("arbitrary", "arbitrary"),
                                             vmem_limit_bytes=VMEM_LIMIT),
        name="dsa",
    )(qi, wi, qa, ki, ka, vat)


def _mla_kernel(qc_ref, kc_ref, vct_ref, o_ref, m_s, l_s, acc_s, *, tq, tk, n_valid, q_pos0):
    j = pl.program_id(1)
    n_tiles, n_full, bound = _key_bounds(j, tq, tk, n_valid, q_pos0)
    row_tk = lax.broadcasted_iota(I32, (tk, tq), 0)
    first_rows = lax.broadcasted_iota(I32, (LANES, tq), 0) < DV_C
    qc = qc_ref[...]
    qhs = [qc[:, hh * QC_SLOT:(hh + 1) * QC_SLOT] for hh in range(H_C)]
    _flash_init(m_s, l_s, acc_s)

    def tile(kt, carry, masked):
        k0 = pl.multiple_of(kt * tk, tk)
        if masked:
            bias = jnp.where(k0 + row_tk < bound, 0.0, NEG_BIG)

        def scores(hh):
            s_t = _nt_dot(kc_ref[pl.ds(k0, tk), hh * QC_SLOT:(hh + 1) * QC_SLOT], qhs[hh])
            return s_t + bias if masked else s_t

        _flash_heads(scores, lambda w: vct_ref[kt, w * LANES:(w + 1) * LANES, :], H_C, first_rows, m_s, l_s, acc_s)
        return carry

    lax.fori_loop(0, n_full, functools.partial(tile, masked=False), 0)
    lax.fori_loop(n_full, n_tiles, functools.partial(tile, masked=True), 0)
    _flash_finish(o_ref, first_rows, l_s, acc_s)


def _mla_call(qc, kc, vct, *, n_batch, n_q, q_row0, k_rows, tq, tk, n_valid, q_pos0):
    nq_blocks = n_q // tq
    qb0 = q_row0 // tq
    n_kt = k_rows // tk
    kern = functools.partial(_mla_kernel, tq=tq, tk=tk, n_valid=n_valid, q_pos0=q_pos0)
    return pl.pallas_call(
        kern, grid=(n_batch, nq_blocks),
        in_specs=[pl.BlockSpec((tq, D_QC), lambda b, j: (qb0 + b * nq_blocks + j, 0)),
                  pl.BlockSpec((k_rows, D_QC), lambda b, j: (b, 0)),
                  pl.BlockSpec((n_kt, D_VC, tk), lambda b, j: (b, 0, 0))],
        out_specs=pl.BlockSpec((tq, D_VC), lambda b, j: (b * nq_blocks + j, 0)),
        out_shape=jax.ShapeDtypeStruct((n_batch * n_q, D_VC), BF16),
        scratch_shapes=_flash_scratch(H_C, tq),
        compiler_params=pltpu.CompilerParams(dimension_semantics=("arbitrary", "arbitrary"),
                                             vmem_limit_bytes=VMEM_LIMIT),
        name="mla",
    )(qc, kc, vct)


def _layer_norm(y, g, b):
    mu = jnp.mean(y, axis=-1, keepdims=True)
    var = jnp.mean(jnp.square(y - mu), axis=-1, keepdims=True)
    return (y - mu) * lax.rsqrt(var + LN_EPS) * g + b


def _outproj_kernel(oa_ref, ob_ref, oc_ref, x_ref, w_ref, g_ref, b_ref, rw_ref, rb_ref, tri_ref,
                    x1_o, idx_o, gate_o, rank_o, cnt_o, run_s):
    @pl.when(pl.program_id(0) == 0)
    def _():
        run_s[...] = jnp.zeros_like(run_s)

    mix = (jnp.dot(oa_ref[...], w_ref[0:D_A, :], preferred_element_type=F32)
           + jnp.dot(ob_ref[...], w_ref[D_A:D_A + W_B, :], preferred_element_type=F32)
           + jnp.dot(oc_ref[...], w_ref[D_A + W_B:, :], preferred_element_type=F32))
    x1 = _layer_norm(ALPHA * x_ref[...] + mix, g_ref[...], b_ref[...])
    x1_o[...] = x1
    lg = jnp.dot(x1.astype(BF16), rw_ref[...], preferred_element_type=F32) + rb_ref[...]
    e = lax.broadcasted_iota(I32, lg.shape, 1).astype(F32)
    vals, hots = [], []
    for k in range(TOP_K):
        m = jnp.max(lg, axis=1, keepdims=True)
        idx = jnp.min(jnp.where(lg == m, e, float(N_EXPERTS)), axis=1, keepdims=True)
        vals.append(m)
        hots.append(jnp.where(e == idx, 1.0, 0.0))
        idx_o[:, k:k + 1] = idx.astype(I32)
        lg = jnp.where(e == idx, -jnp.inf, lg)
    ex = [jnp.exp(vv - vals[0]) for vv in vals]
    den = ex[0] + ex[1] + ex[2] + ex[3]
    for k in range(TOP_K):
        gate_o[:, k:k + 1] = ex[k] / den
    per_token = hots[0] + hots[1] + hots[2] + hots[3]
    before = run_s[...] + jnp.dot(tri_ref[...], per_token.astype(BF16), preferred_element_type=F32)
    for k in range(TOP_K):
        rank_o[:, k:k + 1] = jnp.sum(hots[k] * before, axis=1, keepdims=True).astype(I32)
    run_s[...] = run_s[...] + jnp.sum(per_token, axis=0, keepdims=True)
    cnt_o[...] = run_s[...]


def _outproj_call(oa, ob, oc, x, w_out, g, b, rw, rb):
    n = x.shape[0]
    tm = ROW_TILE
    row = lambda i: (i, 0)
    const2 = lambda i: (0, 0)
    tri = jnp.tril(jnp.ones((tm, tm), BF16), -1)
    return pl.pallas_call(
        _outproj_kernel, grid=(n // tm,),
        in_specs=[pl.BlockSpec((tm, D_A), row), pl.BlockSpec((tm, W_B), row), pl.BlockSpec((tm, D_VC), row),
                  pl.BlockSpec((tm, D_MODEL), row), pl.BlockSpec((D_MODEL, D_MODEL), const2),
                  pl.BlockSpec((1, D_MODEL), const2), pl.BlockSpec((1, D_MODEL), const2),
                  pl.BlockSpec((D_MODEL, N_EXPERTS), const2), pl.BlockSpec((1, N_EXPERTS), const2),
                  pl.BlockSpec((tm, tm), const2)],
        out_specs=[pl.BlockSpec((tm, D_MODEL), row), pl.BlockSpec((tm, TOP_K), row), pl.BlockSpec((tm, TOP_K), row),
                   pl.BlockSpec((tm, TOP_K), row), pl.BlockSpec((1, N_EXPERTS), const2)],
        out_shape=[jax.ShapeDtypeStruct((n, D_MODEL), F32), jax.ShapeDtypeStruct((n, TOP_K), I32),
                   jax.ShapeDtypeStruct((n, TOP_K), F32), jax.ShapeDtypeStruct((n, TOP_K), I32),
                   jax.ShapeDtypeStruct((1, N_EXPERTS), F32)],
        scratch_shapes=[pltpu.VMEM((1, N_EXPERTS), F32)],
        compiler_params=pltpu.CompilerParams(dimension_semantics=("arbitrary",), vmem_limit_bytes=VMEM_LIMIT),
        name="outproj",
    )(oa, ob, oc, x, w_out, g, b, rw, rb, tri)


def _dispatch_kernel(pe_ref, x_ref, dest_ref, xg_hbm, zero_s, sem):
    i = pl.program_id(0)

    @pl.when(i == 0)
    def _():
        zero_s[...] = jnp.zeros_like(zero_s)

        def fill(e):
            start = pl.multiple_of(pe_ref[e] - MOE_TILE, MOE_TILE)
            return pltpu.make_async_copy(zero_s, xg_hbm.at[pl.ds(start, MOE_TILE)], sem)

        def nonempty(e):
            return pe_ref[e] > (pe_ref[e - 1] if e else 0)

        n_blocks = xg_hbm.shape[0] // MOE_TILE
        first_tail = (n_blocks * MOE_TILE - N_EXPERTS * (MOE_TILE - 1)) // MOE_TILE

        def tail(b):
            return pltpu.make_async_copy(zero_s, xg_hbm.at[pl.ds(b * MOE_TILE, MOE_TILE)], sem)

        def unused(b):
            return b * MOE_TILE >= pe_ref[N_EXPERTS - 1]

        for e in range(N_EXPERTS):
            pl.when(nonempty(e))(lambda e=e: fill(e).start())
        for b in range(first_tail, n_blocks):
            pl.when(unused(b))(lambda b=b: tail(b).start())
        for e in range(N_EXPERTS):
            pl.when(nonempty(e))(lambda e=e: fill(e).wait())
        for b in range(first_tail, n_blocks):
            pl.when(unused(b))(lambda b=b: tail(b).wait())

    def row_copy(g, i, k):
        d = dest_ref[0, 0, (g * SUBLANES + i) * TOP_K + k]
        return pltpu.make_async_copy(x_ref.at[g, pl.ds(i, 1)], xg_hbm.at[pl.ds(d, 1)], sem)

    def for_rows(fn):
        def body(g, c):
            for i in range(SUBLANES):
                for k in range(TOP_K):
                    fn(row_copy(g, i, k))
            return c
        lax.fori_loop(0, x_ref.shape[0], body, 0)

    for_rows(lambda cp: cp.start())
    for_rows(lambda cp: cp.wait())


def _dispatch_call(pad_end, x1, dest3, n_pad):
    n = x1.shape[0]
    tm = ROW_TILE
    grid_spec = pltpu.PrefetchScalarGridSpec(
        num_scalar_prefetch=1, grid=(n // tm,),
        in_specs=[pl.BlockSpec((tm // SUBLANES, SUBLANES, D_MODEL), lambda i, pe: (i, 0, 0)),
                  pl.BlockSpec((1, 1, tm * TOP_K), lambda i, pe: (i, 0, 0), memory_space=pltpu.SMEM)],
        out_specs=pl.BlockSpec(memory_space=pl.ANY),
        scratch_shapes=[pltpu.VMEM((MOE_TILE, D_MODEL), F32), pltpu.SemaphoreType.DMA(())])
    return pl.pallas_call(
        _dispatch_kernel, grid_spec=grid_spec,
        out_shape=jax.ShapeDtypeStruct((n_pad, D_MODEL), F32),
        compiler_params=pltpu.CompilerParams(dimension_semantics=("arbitrary",), vmem_limit_bytes=VMEM_LIMIT),
        name="dispatch",
    )(pad_end, x1.reshape(n // SUBLANES, SUBLANES, D_MODEL), dest3)


def _moe_kernel(be_ref, nb_ref, xg_ref, wgu_ref, bgu_ref, wdn_ref, bdn_ref, y_ref, wgu_s, wdn_s):
    i = pl.program_id(0)

    @pl.when(i < nb_ref[0])
    def _():
        e = be_ref[i]
        prev = be_ref[jnp.maximum(i - 1, 0)]

        @pl.when((i == 0) | (e != prev))
        def _():
            wgu_s[...] = wgu_ref[0, 0].astype(BF16)
            wdn_s[...] = wdn_ref[0, 0].astype(BF16)

        h = jnp.dot(xg_ref[...].astype(BF16), wgu_s[...], preferred_element_type=F32) + bgu_ref[0, 0]
        gate = jnp.minimum(h[:, :D_FF], SWIGLU_LIMIT)
        lin = jnp.clip(h[:, D_FF:], -SWIGLU_LIMIT, SWIGLU_LIMIT)
        act = (lin + 1.0) * (gate * jax.nn.sigmoid(SWIGLU_ALPHA * gate))
        y_ref[...] = jnp.dot(act.astype(BF16), wdn_s[...], preferred_element_type=F32) + bdn_ref[0, 0]

    @pl.when(i >= nb_ref[0])
    def _():
        y_ref[...] = jnp.zeros_like(y_ref)


def _moe_call(block_e, n_used, xg, w_gu, b_gu, w_dn, b_dn, layer):
    n_pad = xg.shape[0]
    tm = MOE_TILE

    def used(i, be, nb):
        return (jnp.minimum(i, nb[0] - 1), 0)

    def expert(i, be, nb):
        return (layer, be[i], 0, 0)

    grid_spec = pltpu.PrefetchScalarGridSpec(
        num_scalar_prefetch=2, grid=(n_pad // tm,),
        in_specs=[pl.BlockSpec((tm, D_MODEL), used),
                  pl.BlockSpec((1, 1, D_MODEL, 2 * D_FF), expert), pl.BlockSpec((1, 1, 1, 2 * D_FF), expert),
                  pl.BlockSpec((1, 1, D_FF, D_MODEL), expert), pl.BlockSpec((1, 1, 1, D_MODEL), expert)],
        out_specs=pl.BlockSpec((tm, D_MODEL), lambda i, be, nb: (i, 0)),
        scratch_shapes=[pltpu.VMEM((D_MODEL, 2 * D_FF), BF16), pltpu.VMEM((D_FF, D_MODEL), BF16)])
    return pl.pallas_call(
        _moe_kernel, grid_spec=grid_spec,
        out_shape=jax.ShapeDtypeStruct((n_pad, D_MODEL), F32),
        compiler_params=pltpu.CompilerParams(dimension_semantics=("arbitrary",),
                                             vmem_limit_bytes=56 * 1024 * 1024),
        name="moe",
    )(block_e, n_used, xg, w_gu, b_gu.reshape(DEPTH, N_EXPERTS, 1, 2 * D_FF), w_dn,
      b_dn.reshape(DEPTH, N_EXPERTS, 1, D_MODEL))


def _combine_kernel(x1_ref, gate_ref, dcur_ref, dnext_ref, g_ref, b_ref, y_hbm, o_ref, ybuf, sem):
    i = pl.program_id(0)
    tm = x1_ref.shape[0]
    slot = i % 2

    def row_copy(d_ref, s, g, i, k):
        d = d_ref[0, 0, (g * SUBLANES + i) * TOP_K + k]
        return pltpu.make_async_copy(y_hbm.at[pl.ds(d, 1)], ybuf.at[s, k, g, pl.ds(i, 1)], sem.at[s])

    def for_rows(d_ref, s, fn):
        def body(g, c):
            for i in range(SUBLANES):
                for k in range(TOP_K):
                    fn(row_copy(d_ref, s, g, i, k))
            return c
        lax.fori_loop(0, tm // SUBLANES, body, 0)

    @pl.when(i == 0)
    def _():
        for_rows(dcur_ref, 0, lambda cp: cp.start())

    @pl.when(i + 1 < pl.num_programs(0))
    def _():
        for_rows(dnext_ref, 1 - slot, lambda cp: cp.start())

    for_rows(dcur_ref, slot, lambda cp: cp.wait())
    gates = gate_ref[...]
    moe = gates[:, 0:1] * ybuf[slot, 0].reshape(tm, D_MODEL)
    for k in range(1, TOP_K):
        moe = moe + gates[:, k:k + 1] * ybuf[slot, k].reshape(tm, D_MODEL)
    o_ref[...] = _layer_norm(ALPHA * x1_ref[...] + moe, g_ref[...], b_ref[...])


def _combine_call(x1, gates, dest3, g, b, y_rows):
    n = x1.shape[0]
    tm = ROW_TILE
    n_tiles = n // tm
    row = lambda i: (i, 0)
    const2 = lambda i: (0, 0)
    return pl.pallas_call(
        _combine_kernel, grid=(n_tiles,),
        in_specs=[pl.BlockSpec((tm, D_MODEL), row), pl.BlockSpec((tm, TOP_K), row),
                  pl.BlockSpec((1, 1, tm * TOP_K), lambda i: (i, 0, 0), memory_space=pltpu.SMEM),
                  pl.BlockSpec((1, 1, tm * TOP_K), lambda i: (jnp.minimum(i + 1, n_tiles - 1), 0, 0),
                               memory_space=pltpu.SMEM),
                  pl.BlockSpec((1, D_MODEL), const2), pl.BlockSpec((1, D_MODEL), const2),
                  pl.BlockSpec(memory_space=pl.ANY)],
        out_specs=pl.BlockSpec((tm, D_MODEL), row),
        out_shape=jax.ShapeDtypeStruct((n, D_MODEL), F32),
        scratch_shapes=[pltpu.VMEM((2, TOP_K, tm // SUBLANES, SUBLANES, D_MODEL), F32),
                        pltpu.SemaphoreType.DMA((2,))],
        compiler_params=pltpu.CompilerParams(dimension_semantics=("arbitrary",), vmem_limit_bytes=VMEM_LIMIT),
        name="combine",
    )(x1, gates, dest3, dest3, g, b, y_rows)


def _pack_w_in(w):
    sizes = (D_A, D_A, D_A, H_I * D_I, D_I, H_I, 2 * W_B, D_CQ, D_C, DR_C)
    offs = np.cumsum((0,) + sizes)
    seg = [w[:, offs[k]:offs[k + 1]] for k in range(len(sizes))]
    z = lambda c: jnp.zeros((w.shape[0], c), w.dtype)
    return jnp.concatenate(
        [seg[0], seg[1], seg[2], seg[3], seg[4], seg[5], z(LANES - D_I - H_I), seg[6], seg[7], seg[8],
         seg[9], z(LANES - DR_C)], axis=1).astype(BF16)


def _pack_w_uq(w):
    w3 = w.reshape(D_CQ, H_C, DN_C + DR_C)
    w3 = jnp.pad(w3, ((0, 0), (0, 0), (0, QC_SLOT - DN_C - DR_C)))
    return w3.reshape(D_CQ, D_QC).astype(BF16)


def _pack_w_kc(w_uk):
    wk = jnp.pad(w_uk.reshape(D_C, H_C, DN_C), ((0, 0), (0, 0), (0, QC_SLOT - DN_C))).reshape(D_C, D_QC)
    eye = jnp.pad(jnp.eye(DR_C, dtype=w_uk.dtype), ((0, LANES - DR_C), (DN_C, QC_SLOT - DN_C - DR_C)))
    return jnp.concatenate([wk, jnp.tile(eye, (1, H_C))], axis=0).astype(BF16)


def _rope_tables(pos):
    posf = pos.astype(F32)[:, None]
    n = pos.shape[0]

    def cs(half):
        inv = ROPE_THETA ** (-jnp.arange(half, dtype=F32) / half)
        ang = posf * inv[None, :]
        return jnp.cos(ang), jnp.sin(ang)

    c32, s32 = cs(32)
    z32 = jnp.zeros_like(s32)
    c64 = jnp.tile(jnp.concatenate([c32, c32], 1), (1, 2))
    sa64 = jnp.tile(jnp.concatenate([-s32, z32], 1), (1, 2))
    sb64 = jnp.tile(jnp.concatenate([z32, s32], 1), (1, 2))
    c16, s16 = cs(16)
    z16 = jnp.zeros_like(s16)
    one = lambda c: jnp.ones((n, c), F32)
    zero = lambda c: jnp.zeros((n, c), F32)
    cq = jnp.concatenate([one(DN_C), c16, c16, one(32)], 1)
    saq = jnp.concatenate([zero(DN_C), -s16, z16, zero(32)], 1)
    sbq = jnp.concatenate([zero(DN_C), z16, s16, zero(32)], 1)
    ck = jnp.concatenate([c16, c16, one(96)], 1)
    sak = jnp.concatenate([-s16, z16, zero(96)], 1)
    sbk = jnp.concatenate([z16, s16, zero(96)], 1)
    return [c64, sa64, sb64, cq, saq, sbq, ck, sak, sbk]


def _sgu_tables(w_s, b_s, n_sample_seq, n_sample_batch):
    i = jnp.arange(SGU_CHUNK)
    w_m = jnp.where((i[None, :] // CHUNK) <= (i[:, None] // CHUNK), w_s, 0.0)
    reps = ROW_TILE // SGU_CHUNK
    bd_p = jax.vmap(lambda m: jnp.kron(jnp.eye(reps, dtype=m.dtype), m))(w_m)
    ns = n_sample_seq
    bd_s = jax.vmap(lambda m: jnp.kron(jnp.eye(n_sample_batch, dtype=m.dtype), m[:ns, :ns]))(w_m)
    bd = jnp.stack([bd_p, bd_s]).astype(BF16)
    bias_chunk = jnp.repeat(b_s.T, DG_B, axis=1)
    bias = jnp.stack([jnp.tile(bias_chunk, (reps, 1)), jnp.tile(bias_chunk[:ns], (n_sample_batch, 1))])
    return bd, bias


def _route(top_idx, rank, counts):
    n_assign = top_idx.size
    counts = counts.reshape(N_EXPERTS).astype(I32)
    padded = ((counts + MOE_TILE - 1) // MOE_TILE) * MOE_TILE
    pad_end = jnp.cumsum(padded).astype(I32)
    pad_start = pad_end - padded
    dest = jnp.take(pad_start, top_idx) + rank
    n_blocks = -(-(n_assign + N_EXPERTS * (MOE_TILE - 1)) // MOE_TILE)
    n_used = pad_end[-1] // MOE_TILE
    blk = jnp.minimum(jnp.arange(n_blocks, dtype=I32), n_used - 1) * MOE_TILE
    block_e = jnp.minimum(jnp.sum(pad_end[None, :] <= blk[:, None], axis=1), N_EXPERTS - 1).astype(I32)
    return dest, pad_end, block_e, n_used.reshape(1), n_blocks * MOE_TILE


def _key_tiles_t(v, n_batch, k_rows, tk):
    c = v.shape[1]
    return v.reshape(n_batch, k_rows // tk, tk, c).transpose(0, 1, 3, 2).reshape(n_batch * (k_rows // tk), c, tk)


def kernel(x_prompt, x_sample, cache_a_k, cache_a_v, cache_a_kidx, cache_c_latent, cache_c_krope, w_in, mla_q_norm, mla_kv_norm, w_uq, w_uk, w_uv, sgu_ln_g, sgu_ln_b, w_spatial, b_spatial, w_out, ln1_g, ln1_b, router_w, router_b, w_gate_up, b_gate_up, w_down, b_down, ln2_g, ln2_b):
    batch, seq, _ = x_prompt.shape
    dec_batch, dec_seq, _ = x_sample.shape
    past = cache_a_k.shape[2]
    n_p, n_s = batch * seq, dec_batch * dec_seq
    assert n_p % ROW_TILE == 0 and n_s == ROW_TILE and seq % ROW_TILE == 0
    assert seq % ATT_TK == 0 and seq % ATT_TQ == 0 and dec_seq <= SAMPLE_TQ
    n_all = n_p + n_s
    n_prompt_tiles = n_p // ROW_TILE
    tab_period = seq // ROW_TILE

    pos_p = jnp.arange(seq, dtype=I32)
    pos_s = past + jnp.arange(dec_seq, dtype=I32)
    tabs = _rope_tables(jnp.concatenate([pos_p, jnp.tile(pos_s, dec_batch)]))

    n_keys_s = past + dec_seq
    l_pad_s = -(-n_keys_s // SAMPLE_TK) * SAMPLE_TK
    top_k_p = min(TOPK_MAX, seq // 4)
    top_k_s = min(TOPK_MAX, n_keys_s // 4)

    x = jnp.concatenate([x_prompt.reshape(n_p, D_MODEL), x_sample.reshape(n_s, D_MODEL)], axis=0)
    caches_p = [[] for _ in range(5)]
    caches_s = [[] for _ in range(6)]

    def with_cache(cache, new, width):
        allk = jnp.concatenate([cache.astype(BF16), new.reshape(dec_batch, dec_seq, width)], axis=1)
        allk = jnp.pad(allk, ((0, 0), (0, l_pad_s - n_keys_s), (0, 0)))
        return allk.reshape(dec_batch * l_pad_s, width)

    def sample_queries(arr):
        q = arr[n_p:].reshape(dec_batch, dec_seq, -1)
        return jnp.pad(q, ((0, 0), (0, SAMPLE_TQ - dec_seq), (0, 0))).reshape(dec_batch * SAMPLE_TQ, -1)

    def sample_rows(o):
        return o.reshape(dec_batch, SAMPLE_TQ, -1)[:, :dec_seq].reshape(n_s, -1)

    for l in range(DEPTH):
        w_packed = _pack_w_in(w_in[l])
        wuq = _pack_w_uq(w_uq[l])
        wkc = _pack_w_kc(w_uk[l])
        wuv = w_uv[l].astype(BF16)
        bd, bsb = _sgu_tables(w_spatial[l], b_spatial[l], dec_seq, dec_batch)
        (qa, ka, kab, va, vab, qi, ki, kib, wi, ob, vb, qc, lat, kr, kc, vc) = _proj_call(
            x, w_packed, tabs, mla_q_norm[l].reshape(1, -1), mla_kv_norm[l].reshape(1, -1), wuq, wkc, wuv,
            sgu_ln_g[l].reshape(1, -1), sgu_ln_b[l].reshape(1, -1), bd, bsb, n_prompt_tiles, tab_period)

        oa_p = _dsa_call(qi, wi, qa, kib, kab, _key_tiles_t(vab[:n_p], batch, seq, ATT_TK),
                         n_batch=batch, n_q=seq, q_row0=0, k_rows=seq,
                         tq=ATT_TQ, tk=ATT_TK, n_valid=seq, q_pos0=0, top_k=top_k_p)
        oc_p = _mla_call(qc, kc, _key_tiles_t(vc[:n_p], batch, seq, ATT_TK),
                         n_batch=batch, n_q=seq, q_row0=0, k_rows=seq,
                         tq=ATT_TQ, tk=ATT_TK, n_valid=seq, q_pos0=0)

        kc_c, vc_c = _kvup_call(cache_c_latent[l].reshape(dec_batch * past, D_C),
                                cache_c_krope[l].reshape(dec_batch * past, DR_C), wkc, wuv)
        ki_s = with_cache(cache_a_kidx[l], kib[n_p:], D_I)
        ka_s = with_cache(cache_a_k[l].reshape(dec_batch, past, D_A), kab[n_p:], D_A)
        va_s = with_cache(cache_a_v[l].reshape(dec_batch, past, D_A), vab[n_p:], D_A)
        kc_s = with_cache(kc_c.reshape(dec_batch, past, D_QC), kc[n_p:], D_QC)
        vc_s = with_cache(vc_c.reshape(dec_batch, past, D_VC), vc[n_p:], D_VC)
        oa_s = _dsa_call(sample_queries(qi), sample_queries(wi), sample_queries(qa), ki_s, ka_s,
                         _key_tiles_t(va_s, dec_batch, l_pad_s, SAMPLE_TK),
                         n_batch=dec_batch, n_q=SAMPLE_TQ, q_row0=0, k_rows=l_pad_s,
                         tq=SAMPLE_TQ, tk=SAMPLE_TK, n_valid=n_keys_s, q_pos0=past, top_k=top_k_s)
        oc_s = _mla_call(sample_queries(qc), kc_s, _key_tiles_t(vc_s, dec_batch, l_pad_s, SAMPLE_TK),
                         n_batch=dec_batch, n_q=SAMPLE_TQ, q_row0=0, k_rows=l_pad_s,
                         tq=SAMPLE_TQ, tk=SAMPLE_TK, n_valid=n_keys_s, q_pos0=past)

        oa = jnp.concatenate([oa_p, sample_rows(oa_s)], axis=0)
        oc = jnp.concatenate([oc_p, sample_rows(oc_s)], axis=0)
        x1, top_idx, gates, rank, counts = _outproj_call(
            oa, ob, oc, x, w_out[l].astype(BF16), ln1_g[l].reshape(1, -1), ln1_b[l].reshape(1, -1),
            router_w[l].astype(BF16), router_b[l].reshape(1, -1))

        dest, pad_end, block_e, n_used, n_pad = _route(top_idx, rank, counts)
        dest3 = dest.reshape(n_all // ROW_TILE, 1, ROW_TILE * TOP_K)
        xg = _dispatch_call(pad_end, x1, dest3, n_pad)
        y_rows = _moe_call(block_e, n_used, xg, w_gate_up, b_gate_up, w_down, b_down, l)
        x = _combine_call(x1, gates, dest3, ln2_g[l].reshape(1, -1), ln2_b[l].reshape(1, -1), y_rows)

        for dst, arr, shp in ((caches_p[0], ka, (H_A, DH_A)), (caches_p[1], va, (H_A, DH_A)),
                              (caches_p[2], ki, (D_I,)), (caches_p[3], lat, (D_C,)), (caches_p[4], kr, (DR_C,))):
            dst.append(arr[:n_p].reshape((batch, seq) + shp))
        for dst, arr, shp in ((caches_s[0], ka, (H_A, DH_A)), (caches_s[1], va, (H_A, DH_A)),
                              (caches_s[2], ki, (D_I,)), (caches_s[3], lat, (D_C,)), (caches_s[4], kr, (DR_C,)),
                              (caches_s[5], vb, (W_B,))):
            dst.append(arr[n_p:].reshape((dec_batch, dec_seq) + shp))

    y_prompt = x[:n_p].reshape(batch, seq, D_MODEL)
    y_sample = x[n_p:].reshape(dec_batch, dec_seq, D_MODEL)
    return (y_prompt, y_sample) + tuple(jnp.stack(c) for c in caches_p) + tuple(jnp.stack(c) for c in caches_s)
```

```python
import functools

import numpy as np
import jax
import jax.numpy as jnp
from jax import lax
from jax.experimental import pallas as pl
from jax.experimental.pallas import tpu as pltpu

F32 = jnp.float32
BF16 = jnp.bfloat16
I32 = jnp.int32
I16 = jnp.int16

D_MODEL = 1024
DEPTH = 2
CHUNK = 64
ROPE_THETA = 10000.0
LN_EPS = 1e-5
RMS_EPS = 1e-6
H_A, DH_A = 6, 64
H_I, D_I = 8, 64
TOPK_MAX = 256
G_B, DG_B = 4, 64
W_B = G_B * DG_B
SGU_CHUNK = 128
H_C, DN_C, DR_C, DV_C = 6, 64, 32, 64
D_CQ, D_C = 256, 256
MLA_SCALE = (DN_C + DR_C) ** -0.5
LOG2E = float(np.log2(np.e))
D_A = H_A * DH_A
D_VC = H_C * DV_C
N_EXPERTS = 32
TOP_K = 4
D_FF = 1024
SWIGLU_LIMIT = 7.0
SWIGLU_ALPHA = 1.702
ALPHA = (2 * DEPTH) ** 0.25

LANES = 128
SUBLANES = 8
PACKED_ROWS = 16
HALF_RANGE = 1 << 15
ROW_TILE = 256
MOE_TILE = 256
ATT_TQ = 256
ATT_TK = 512
SAMPLE_TQ = 128
SAMPLE_TK = 384
VMEM_LIMIT = 48 * 1024 * 1024

C_QA, C_KA, C_VA, C_QI, C_KIW, C_ZU, C_ZV, C_CQ, C_CKV, C_KR = (
    0, 384, 768, 1152, 1664, 1792, 2048, 2304, 2560, 2816)
IN_PACKED = 2944
QC_SLOT = 128
D_QC = H_C * QC_SLOT

NEG_BIG = -1e30
KEY_NEG_INF = -2139095041
INT_MIN = -2147483648


def _nt_dot(a, b):
    return lax.dot_general(a, b, (((1,), (1,)), ((), ())), preferred_element_type=F32)


def _rope128(x, c, sa, sb, half):
    return x * c + pltpu.roll(x, LANES - half, 1) * sa + pltpu.roll(x, half, 1) * sb


def _proj_kernel(xp_ref, xs_ref, w_ref, c64_ref, sa64_ref, sb64_ref, cq_ref, saq_ref, sbq_ref,
                 ck_ref, sak_ref, sbk_ref, qn_ref, kvn_ref, wuq_ref, wkc_ref, wuv_ref,
                 sgug_ref, sgub_ref, bd_ref, bsb_ref,
                 qa_o, kab_o, qi_o, kib_o, wi_o, ob_o, qc_o, kc_o,
                 kap_o, vap_o, kip_o, latp_o, krp_o, vat_o, vct_o,
                 kas_o, vas_o, kis_o, lats_o, krs_o, vbs_o, vabs_o, vcs_o, *, n_prompt_tiles):
    is_prompt = pl.program_id(0) < n_prompt_tiles
    x = jnp.where(is_prompt, xp_ref[...], xs_ref[...])
    h = jnp.dot(x.astype(BF16), w_ref[...], preferred_element_type=F32)
    c64, sa64, sb64 = c64_ref[...], sa64_ref[...], sb64_ref[...]

    def rope64(col):
        return _rope128(h[:, col:col + LANES], c64, sa64, sb64, 32)

    kas = []
    for c in range(D_A // LANES):
        qa_o[:, c * LANES:(c + 1) * LANES] = (rope64(C_QA + c * LANES) * (DH_A ** -0.5 * LOG2E)).astype(BF16)
        kas.append(rope64(C_KA + c * LANES))
        kab_o[:, c * LANES:(c + 1) * LANES] = kas[c].astype(BF16)
    va = h[:, C_VA:C_VA + D_A]
    for c in range(H_I * D_I // LANES):
        qi_o[:, c * LANES:(c + 1) * LANES] = (rope64(C_QI + c * LANES) * (D_I ** -0.5)).astype(BF16)
    ki = rope64(C_KIW)[:, :D_I]
    kib_o[...] = ki.astype(BF16)
    wi_o[...] = h[:, C_KIW:C_KIW + LANES] * (H_I ** -0.5)

    z = h[:, C_ZU:C_ZU + 2 * W_B]
    z = 0.5 * z * (1.0 + jnp.tanh(np.sqrt(2.0 / np.pi) * (z + 0.044715 * (z * z * z))))
    u = z[:, :W_B]
    v = z[:, W_B:]
    mu = jnp.mean(v, axis=-1, keepdims=True)
    var = jnp.mean(jnp.square(v - mu), axis=-1, keepdims=True)
    v = (v - mu) * lax.rsqrt(var + LN_EPS) * sgug_ref[...] + sgub_ref[...]
    v16 = v.astype(BF16)
    lane = lax.broadcasted_iota(I32, (v.shape[0], LANES), 1)
    for w in range(W_B // LANES):
        vw = v16[:, w * LANES:(w + 1) * LANES]
        m0 = jnp.dot(bd_ref[0, 2 * w], vw, preferred_element_type=F32)
        m1 = jnp.dot(bd_ref[0, 2 * w + 1], vw, preferred_element_type=F32)
        mixed = jnp.where(lane < DG_B, m0, m1) + bsb_ref[0, :, w * LANES:(w + 1) * LANES]
        ob_o[:, w * LANES:(w + 1) * LANES] = (u[:, w * LANES:(w + 1) * LANES] * mixed).astype(BF16)

    cq = h[:, C_CQ:C_CQ + D_CQ]
    cq = cq * lax.rsqrt(jnp.mean(jnp.square(cq), axis=-1, keepdims=True) + RMS_EPS) * qn_ref[...]
    q = jnp.dot(cq.astype(BF16), wuq_ref[...], preferred_element_type=F32)
    cqt, saq, sbq = cq_ref[...], saq_ref[...], sbq_ref[...]
    for hh in range(H_C):
        qs = _rope128(q[:, hh * QC_SLOT:(hh + 1) * QC_SLOT], cqt, saq, sbq, DR_C // 2)
        qc_o[:, hh * QC_SLOT:(hh + 1) * QC_SLOT] = (qs * (MLA_SCALE * LOG2E)).astype(BF16)
    ckv = h[:, C_CKV:C_CKV + D_C]
    lat = ckv * lax.rsqrt(jnp.mean(jnp.square(ckv), axis=-1, keepdims=True) + RMS_EPS) * kvn_ref[...]
    kr = _rope128(h[:, C_KR:C_KR + LANES], ck_ref[...], sak_ref[...], sbk_ref[...], DR_C // 2)
    lat16 = lat.astype(BF16)
    kc = (jnp.dot(lat16, wkc_ref[:D_C, :], preferred_element_type=F32)
          + jnp.dot(kr.astype(BF16), wkc_ref[D_C:, :], preferred_element_type=F32))
    kc_o[...] = kc.astype(BF16)
    vc = jnp.dot(lat16, wuv_ref[...], preferred_element_type=F32)

    @pl.when(is_prompt)
    def _():
        for c in range(D_A // LANES):
            kap_o[:, c * LANES:(c + 1) * LANES] = kas[c]
            vat_o[0, c * LANES:(c + 1) * LANES, :] = va[:, c * LANES:(c + 1) * LANES].T.astype(BF16)
            vct_o[0, c * LANES:(c + 1) * LANES, :] = vc[:, c * LANES:(c + 1) * LANES].T.astype(BF16)
        vap_o[...] = va
        kip_o[...] = ki
        latp_o[...] = lat
        krp_o[...] = kr[:, :DR_C]

    @pl.when(jnp.logical_not(is_prompt))
    def _():
        for c in range(D_A // LANES):
            kas_o[:, c * LANES:(c + 1) * LANES] = kas[c]
        vas_o[...] = va
        kis_o[...] = ki
        lats_o[...] = lat
        krs_o[...] = kr[:, :DR_C]
        vbs_o[...] = v
        vabs_o[...] = va.astype(BF16)
        vcs_o[...] = vc.astype(BF16)


def _proj_call(xp, xs, w_packed, tabs, qn, kvn, wuq, wkc, wuv, sgug, sgub, bd, bsb, tab_period):
    n_p = xp.shape[0]
    tm = ROW_TILE
    n_prompt_tiles = n_p // tm
    n = n_p + tm
    grid = (n_prompt_tiles + 1,)
    per_kt = ATT_TK // tm

    def row(i):
        return (i, 0)

    def prow(i):
        return (jnp.minimum(i, n_prompt_tiles - 1), 0)

    def ptile(i):
        ip = jnp.minimum(i, n_prompt_tiles - 1)
        return (ip // per_kt, 0, ip % per_kt)

    def const2(i):
        return (0, 0)

    def tab(i):
        return (jnp.where(i < n_prompt_tiles, i % tab_period, tab_period), 0)

    def grp(i):
        return (jnp.where(i < n_prompt_tiles, 0, 1), 0, 0, 0)

    def grp3(i):
        return (jnp.where(i < n_prompt_tiles, 0, 1), 0, 0)

    in_specs = [pl.BlockSpec((tm, D_MODEL), prow), pl.BlockSpec((tm, D_MODEL), const2),
                pl.BlockSpec((D_MODEL, IN_PACKED), const2)]
    in_specs += [pl.BlockSpec((tm, LANES), tab)] * 9
    in_specs += [pl.BlockSpec((1, D_CQ), const2), pl.BlockSpec((1, D_C), const2),
                 pl.BlockSpec((D_CQ, D_QC), const2), pl.BlockSpec((D_C + LANES, D_QC), const2),
                 pl.BlockSpec((D_C, D_VC), const2),
                 pl.BlockSpec((1, W_B), const2), pl.BlockSpec((1, W_B), const2),
                 pl.BlockSpec((1, G_B, tm, tm), grp), pl.BlockSpec((1, tm, W_B), grp3)]
    all_rows = [(D_A, BF16), (D_A, BF16), (H_I * D_I, BF16), (D_I, BF16), (LANES, F32), (W_B, BF16),
                (D_QC, BF16), (D_QC, BF16)]
    caches = [(D_A, F32), (D_A, F32), (D_I, F32), (D_C, F32), (DR_C, F32)]
    sample_only = caches + [(W_B, F32), (D_A, BF16), (D_VC, BF16)]
    out_shape = ([jax.ShapeDtypeStruct((n, c), d) for c, d in all_rows]
                 + [jax.ShapeDtypeStruct((n_p, c), d) for c, d in caches]
                 + [jax.ShapeDtypeStruct((n_p // ATT_TK, c, ATT_TK), BF16) for c in (D_A, D_VC)]
                 + [jax.ShapeDtypeStruct((tm, c), d) for c, d in sample_only])
    out_specs = ([pl.BlockSpec((tm, c), row) for c, _ in all_rows]
                 + [pl.BlockSpec((tm, c), prow) for c, _ in caches]
                 + [pl.BlockSpec((1, c, tm), ptile) for c in (D_A, D_VC)]
                 + [pl.BlockSpec((tm, c), const2) for c, _ in sample_only])
    return pl.pallas_call(
        functools.partial(_proj_kernel, n_prompt_tiles=n_prompt_tiles),
        grid=grid, in_specs=in_specs, out_specs=out_specs, out_shape=out_shape,
        compiler_params=pltpu.CompilerParams(dimension_semantics=("arbitrary",), vmem_limit_bytes=VMEM_LIMIT),
        name="proj",
    )(xp, xs, w_packed, *tabs, qn, kvn, wuq, wkc, wuv, sgug, sgub, bd, bsb)


def _kvup_kernel(lat_ref, kr_ref, wkc_ref, wuv_ref, kc_o, vc_o):
    lat16 = lat_ref[...].astype(BF16)
    kc = (jnp.dot(lat16, wkc_ref[:D_C, :], preferred_element_type=F32)
          + jnp.dot(kr_ref[...].astype(BF16), wkc_ref[D_C:D_C + DR_C, :], preferred_element_type=F32))
    kc_o[...] = kc.astype(BF16)
    vc_o[...] = jnp.dot(lat16, wuv_ref[...], preferred_element_type=F32).astype(BF16)


def _kvup_call(lat, kr, wkc, wuv):
    n = lat.shape[0]
    tm = 1024
    return pl.pallas_call(
        _kvup_kernel, grid=(n // tm,),
        in_specs=[pl.BlockSpec((tm, D_C), lambda i: (i, 0)), pl.BlockSpec((tm, DR_C), lambda i: (i, 0)),
                  pl.BlockSpec((D_C + LANES, D_QC), lambda i: (0, 0)), pl.BlockSpec((D_C, D_VC), lambda i: (0, 0))],
        out_specs=[pl.BlockSpec((tm, D_QC), lambda i: (i, 0)), pl.BlockSpec((tm, D_VC), lambda i: (i, 0))],
        out_shape=[jax.ShapeDtypeStruct((n, D_QC), BF16), jax.ShapeDtypeStruct((n, D_VC), BF16)],
        compiler_params=pltpu.CompilerParams(dimension_semantics=("arbitrary",), vmem_limit_bytes=VMEM_LIMIT),
        name="kvup",
    )(lat, kr, wkc, wuv)


def _key_bounds(j, tq, tk, n_valid, q_pos0):
    q_first = q_pos0 + j * tq
    kmax = jnp.minimum(((q_first + tq - 1) // CHUNK + 1) * CHUNK, n_valid)
    n_tiles = (kmax + tk - 1) // tk
    n_full = jnp.minimum((q_first // CHUNK + 1) * CHUNK, n_valid) // tk
    qpos = q_first + lax.broadcasted_iota(I32, (1, tq), 1)
    bound = jnp.minimum((qpos // CHUNK + 1) * CHUNK, n_valid)
    return n_tiles, n_full, bound


def _flash_init(m_s, l_s, acc_s):
    m_s[...] = jnp.full(m_s.shape, NEG_BIG, F32)
    l_s[...] = jnp.zeros(l_s.shape, F32)
    acc_s[...] = jnp.zeros(acc_s.shape, F32)


def _flash_update(hh, s_t, v_t, m_s, l_s):
    m_old = m_s[hh]
    m_new = jnp.maximum(m_old, jnp.max(s_t, axis=0, keepdims=True))
    p = jnp.exp2(s_t - m_new)
    a = jnp.exp2(m_old - m_new)
    l_s[hh] = a * l_s[hh] + jnp.sum(p, axis=0, keepdims=True)
    m_s[hh] = m_new
    return a, jnp.dot(v_t, p.astype(BF16), preferred_element_type=F32)


def _flash_pair_accumulate(w, first_rows, upd0, upd1, acc_s):
    (a0, pv0), (a1, pv1) = upd0, upd1
    acc_s[w] = acc_s[w] * jnp.where(first_rows, a0, a1) + jnp.where(first_rows, pv0, pv1)


def _flash_heads(scores, values, n_heads, first_rows, m_s, l_s, acc_s):
    s_all = [scores(hh) for hh in range(n_heads)]
    upd = []
    for hh in range(n_heads):
        upd.append(_flash_update(hh, s_all[hh], values(hh // 2), m_s, l_s))
        if hh % 2:
            _flash_pair_accumulate(hh // 2, first_rows, upd[hh - 1], upd[hh], acc_s)


def _flash_finish(o_ref, first_rows, l_s, acc_s):
    for w in range(acc_s.shape[0]):
        l_sel = jnp.where(first_rows, l_s[2 * w], l_s[2 * w + 1])
        o_ref[:, w * LANES:(w + 1) * LANES] = (acc_s[w] / l_sel).T.astype(o_ref.dtype)


def _flash_scratch(n_heads, tq):
    return [pltpu.VMEM((n_heads, 1, tq), F32), pltpu.VMEM((n_heads, 1, tq), F32),
            pltpu.VMEM((n_heads // 2, LANES, tq), F32)]


def _dsa_kernel(qi_ref, wi_ref, qa_ref, ki_ref, ka_ref, vat_ref, o_ref, key_ref, k16_ref, m_s, l_s, acc_s, *,
                tq, tk, n_valid, q_pos0, top_k):
    j = pl.program_id(1)
    n_tiles, _, bound = _key_bounds(j, tq, tk, n_valid, q_pos0)
    row_tk = lax.broadcasted_iota(I32, (tk, tq), 0)
    row8 = lax.broadcasted_iota(I32, (SUBLANES, tq), 0)

    qi = qi_ref[...]
    q_stack = jnp.concatenate([qi[:, hh * D_I:(hh + 1) * D_I] for hh in range(H_I)], axis=0)
    wi_t = wi_ref[...].T
    wis = [wi_t[D_I + hh:D_I + hh + 1, :] for hh in range(H_I)]

    def score_tile(kt, carry):
        k0 = pl.multiple_of(kt * tk, tk)
        d = _nt_dot(ki_ref[pl.ds(k0, tk), :], q_stack)
        s = wis[0] * jnp.maximum(d[:, 0:tq], 0.0)
        for hh in range(1, H_I):
            s = s + wis[hh] * jnp.maximum(d[:, hh * tq:(hh + 1) * tq], 0.0)
        bits = pltpu.bitcast(s, I32)
        key = jnp.where(bits < 0, bits ^ 0x7FFFFFFF, bits)
        key = jnp.where(key == -1, 0, key)
        key = jnp.where(k0 + row_tk < bound, key, KEY_NEG_INF)
        key_ref[kt] = key
        k16_ref[kt] = (key >> 16).astype(I16)
        return carry

    lax.fori_loop(0, n_tiles, score_tile, 0)

    def count(pred):
        def body(kt, acc):
            for r in range(tk // SUBLANES):
                kk = key_ref[kt, r * SUBLANES:(r + 1) * SUBLANES, :]
                acc = acc + jnp.where(pred(kk, kt * tk + r * SUBLANES), 1.0, 0.0)
            return acc
        acc = lax.fori_loop(0, n_tiles, body, jnp.zeros((SUBLANES, tq), F32))
        return jnp.sum(acc, axis=0, keepdims=True)

    def count16_ge(cand):
        c16 = cand.astype(I16)

        def body(kt, acc):
            for r in range(tk // PACKED_ROWS):
                kk = k16_ref[kt, r * PACKED_ROWS:(r + 1) * PACKED_ROWS, :]
                acc = acc + jnp.where(kk >= c16, jnp.int16(1), jnp.int16(0))
            return acc
        acc = lax.fori_loop(0, n_tiles, body, jnp.zeros((PACKED_ROWS, tq), I16))
        return jnp.sum(acc.astype(F32), axis=0, keepdims=True)

    kf = float(top_k)
    c0 = count16_ge(jnp.zeros((1, tq), I32))
    nonneg = c0 >= kf
    t_hi0 = jnp.where(nonneg, 0, -HALF_RANGE).astype(I32)
    cnt0 = jnp.where(nonneg, c0, (n_tiles * tk).astype(F32))

    def hi_body(it, carry):
        t_hi, cnt_t = carry
        cand = t_hi | lax.shift_left(jnp.int32(1), 14 - it)
        cnt = count16_ge(cand)
        ok = cnt >= kf
        return jnp.where(ok, cand, t_hi), jnp.where(ok, cnt, cnt_t)

    t_hi, cnt_hi = lax.fori_loop(0, 15, hi_body, (t_hi0, cnt0))
    need_lo = kf - count16_ge(t_hi + 1)
    cnt_class = cnt_hi - (kf - need_lo)

    def lo_tile(kt, carry):
        kk = key_ref[kt]
        lo = (kk & 0xFFFF) - HALF_RANGE
        k16_ref[kt] = jnp.where((kk >> 16) == t_hi, lo, -HALF_RANGE).astype(I16)
        return carry

    lax.fori_loop(0, n_tiles, lo_tile, 0)

    def lo_cond(carry):
        it, _, cnt_t = carry
        return (it < 16) & (jnp.max(jnp.abs(cnt_t - need_lo)) > 0.5)

    def lo_body(carry):
        it, t_lo, cnt_t = carry
        cand = t_lo | lax.shift_left(jnp.int32(1), 15 - it)
        cnt = count16_ge(cand - HALF_RANGE)
        ok = cnt >= need_lo
        return it + 1, jnp.where(ok, cand, t_lo), jnp.where(ok, cnt, cnt_t)

    _, t_lo, cnt_lo = lax.while_loop(lo_cond, lo_body, (jnp.int32(0), jnp.zeros((1, tq), I32), cnt_class))
    t = lax.shift_left(t_hi, 16) | t_lo
    inexact = jnp.max(jnp.abs(cnt_lo - need_lo)) > 0.5

    @pl.when(inexact)
    def _():
        c_gt = count(lambda kk, base: kk > t)
        c_eq = count(lambda kk, base: kk == t)
        need = kf - c_gt

        @pl.when(jnp.max(jnp.where((c_eq > need) & (t > KEY_NEG_INF), 1.0, 0.0)) > 0.5)
        def _():
            def jb(it, jj):
                cand = jj | lax.shift_left(jnp.int32(1), 12 - it)
                f = count(lambda kk, base: (kk == t) & (base + row8 < cand))
                return jnp.where(f < need, cand, jj)
            j_last = lax.fori_loop(0, 13, jb, jnp.zeros((1, tq), I32))

            def drop_tile(kt, carry):
                kk = key_ref[kt]
                key_ref[kt] = jnp.where((kk == t) & (kt * tk + row_tk > j_last), KEY_NEG_INF, kk)
                return carry

            lax.fori_loop(0, n_tiles, drop_tile, 0)

    t_member = jnp.maximum(t, KEY_NEG_INF + 1)

    lane = lax.broadcasted_iota(I32, (tq, LANES), 1)
    first_rows = lax.broadcasted_iota(I32, (LANES, tq), 0) < DH_A
    qa = qa_ref[...]
    half_masks = [jnp.where(lane < DH_A, 1.0, 0.0).astype(BF16), jnp.where(lane < DH_A, 0.0, 1.0).astype(BF16)]
    qhs = [qa[:, (hh // 2) * LANES:(hh // 2 + 1) * LANES] * half_masks[hh % 2] for hh in range(H_A)]
    _flash_init(m_s, l_s, acc_s)

    def att_tile(kt, carry):
        k0 = pl.multiple_of(kt * tk, tk)
        bias = jnp.where(key_ref[kt] >= t_member, 0.0, NEG_BIG)

        def scores(hh):
            return _nt_dot(ka_ref[pl.ds(k0, tk), (hh // 2) * LANES:(hh // 2 + 1) * LANES], qhs[hh]) + bias

        _flash_heads(scores, lambda w: vat_ref[kt, w * LANES:(w + 1) * LANES, :], H_A, first_rows, m_s, l_s, acc_s)
        return carry

    lax.fori_loop(0, n_tiles, att_tile, 0)
    _flash_finish(o_ref, first_rows, l_s, acc_s)


def _dsa_call(qi, wi, qa, ki, ka, vat, *, n_batch, n_q, q_row0, k_rows, tq, tk, n_valid, q_pos0, top_k):
    nq_blocks = n_q // tq
    qb0 = q_row0 // tq
    n_kt = k_rows // tk
    kern = functools.partial(_dsa_kernel, tq=tq, tk=tk, n_valid=n_valid, q_pos0=q_pos0, top_k=top_k)

    def qmap(b, j):
        return (qb0 + b * nq_blocks + j, 0)

    def kmap(b, j):
        return (b, 0)

    return pl.pallas_call(
        kern, grid=(n_batch, nq_blocks),
        in_specs=[pl.BlockSpec((tq, H_I * D_I), qmap), pl.BlockSpec((tq, LANES), qmap), pl.BlockSpec((tq, D_A), qmap),
                  pl.BlockSpec((k_rows, D_I), kmap), pl.BlockSpec((k_rows, D_A), kmap),
                  pl.BlockSpec((n_kt, D_A, tk), lambda b, j: (b, 0, 0))],
        out_specs=pl.BlockSpec((tq, D_A), lambda b, j: (b * nq_blocks + j, 0)),
        out_shape=jax.ShapeDtypeStruct((n_batch * n_q, D_A), BF16),
        scratch_shapes=[pltpu.VMEM((n_kt, tk, tq), I32), pltpu.VMEM((n_kt, tk, tq), I16)] + _flash_scratch(H_A, tq),
        compiler_params=pltpu.CompilerParams(dimension_semantics=("arbitrary", "arbitrary"),
                                             vmem_limit_bytes=VMEM_LIMIT),
        name="dsa",
    )(qi, wi, qa, ki, ka, vat)


def _mla_kernel(qc_ref, kc_ref, vct_ref, o_ref, m_s, l_s, acc_s, *, tq, tk, n_valid, q_pos0):
    j = pl.program_id(1)
    n_tiles, n_full, bound = _key_bounds(j, tq, tk, n_valid, q_pos0)
    row_tk = lax.broadcasted_iota(I32, (tk, tq), 0)
    first_rows = lax.broadcasted_iota(I32, (LANES, tq), 0) < DV_C
    qc = qc_ref[...]
    qhs = [qc[:, hh * QC_SLOT:(hh + 1) * QC_SLOT] for hh in range(H_C)]
    _flash_init(m_s, l_s, acc_s)

    def tile(kt, carry, masked):
        k0 = pl.multiple_of(kt * tk, tk)
        if masked:
            bias = jnp.where(k0 + row_tk < bound, 0.0, NEG_BIG)

        def scores(hh):
            s_t = _nt_dot(kc_ref[pl.ds(k0, tk), hh * QC_SLOT:(hh + 1) * QC_SLOT], qhs[hh])
            return s_t + bias if masked else s_t

        _flash_heads(scores, lambda w: vct_ref[kt, w * LANES:(w + 1) * LANES, :], H_C, first_rows, m_s, l_s, acc_s)
        return carry

    lax.fori_loop(0, n_full, functools.partial(tile, masked=False), 0)
    lax.fori_loop(n_full, n_tiles, functools.partial(tile, masked=True), 0)
    _flash_finish(o_ref, first_rows, l_s, acc_s)


def _mla_call(qc, kc, vct, *, n_batch, n_q, q_row0, k_rows, tq, tk, n_valid, q_pos0):
    nq_blocks = n_q // tq
    qb0 = q_row0 // tq
    n_kt = k_rows // tk
    kern = functools.partial(_mla_kernel, tq=tq, tk=tk, n_valid=n_valid, q_pos0=q_pos0)
    return pl.pallas_call(
        kern, grid=(n_batch, nq_blocks),
        in_specs=[pl.BlockSpec((tq, D_QC), lambda b, j: (qb0 + b * nq_blocks + j, 0)),
                  pl.BlockSpec((k_rows, D_QC), lambda b, j: (b, 0)),
                  pl.BlockSpec((n_kt, D_VC, tk), lambda b, j: (b, 0, 0))],
        out_specs=pl.BlockSpec((tq, D_VC), lambda b, j: (b * nq_blocks + j, 0)),
        out_shape=jax.ShapeDtypeStruct((n_batch * n_q, D_VC), BF16),
        scratch_shapes=_flash_scratch(H_C, tq),
        compiler_params=pltpu.CompilerParams(dimension_semantics=("arbitrary", "arbitrary"),
                                             vmem_limit_bytes=VMEM_LIMIT),
        name="mla",
    )(qc, kc, vct)


def _layer_norm(y, g, b):
    mu = jnp.mean(y, axis=-1, keepdims=True)
    var = jnp.mean(jnp.square(y - mu), axis=-1, keepdims=True)
    return (y - mu) * lax.rsqrt(var + LN_EPS) * g + b


def _outproj_kernel(oap_ref, oas_ref, ob_ref, ocp_ref, ocs_ref, xp_ref, xs_ref, w_ref, g_ref, b_ref, rw_ref, rb_ref,
                    tri_ref, x1_o, idx_o, gate_o, rank_o, cnt_o, run_s, *, n_prompt_tiles):
    @pl.when(pl.program_id(0) == 0)
    def _():
        run_s[...] = jnp.zeros_like(run_s)

    is_prompt = pl.program_id(0) < n_prompt_tiles
    oa = jnp.where(is_prompt, oap_ref[...], oas_ref[...])
    oc = jnp.where(is_prompt, ocp_ref[...], ocs_ref[...])
    x = jnp.where(is_prompt, xp_ref[...], xs_ref[...])
    mix = (jnp.dot(oa, w_ref[0:D_A, :], preferred_element_type=F32)
           + jnp.dot(ob_ref[...], w_ref[D_A:D_A + W_B, :], preferred_element_type=F32)
           + jnp.dot(oc, w_ref[D_A + W_B:, :], preferred_element_type=F32))
    x1 = _layer_norm(ALPHA * x + mix, g_ref[...], b_ref[...])
    x1_o[...] = x1
    lg = jnp.dot(x1.astype(BF16), rw_ref[...], preferred_element_type=F32) + rb_ref[...]
    e = lax.broadcasted_iota(I32, lg.shape, 1).astype(F32)
    vals, hots = [], []
    for k in range(TOP_K):
        m = jnp.max(lg, axis=1, keepdims=True)
        idx = jnp.min(jnp.where(lg == m, e, float(N_EXPERTS)), axis=1, keepdims=True)
        vals.append(m)
        hots.append(jnp.where(e == idx, 1.0, 0.0))
        idx_o[:, k:k + 1] = idx.astype(I32)
        lg = jnp.where(e == idx, -jnp.inf, lg)
    ex = [jnp.exp(vv - vals[0]) for vv in vals]
    den = ex[0] + ex[1] + ex[2] + ex[3]
    for k in range(TOP_K):
        gate_o[:, k:k + 1] = ex[k] / den
    per_token = hots[0] + hots[1] + hots[2] + hots[3]
    before = run_s[...] + jnp.dot(tri_ref[...], per_token.astype(BF16), preferred_element_type=F32)
    for k in range(TOP_K):
        rank_o[:, k:k + 1] = jnp.sum(hots[k] * before, axis=1, keepdims=True).astype(I32)
    run_s[...] = run_s[...] + jnp.sum(per_token, axis=0, keepdims=True)
    cnt_o[...] = run_s[...]


def _outproj_call(oa_p, oa_s, ob, oc_p, oc_s, xp, xs, w_out, g, b, rw, rb):
    tm = ROW_TILE
    n_prompt_tiles = xp.shape[0] // tm
    n = xp.shape[0] + tm
    row = lambda i: (i, 0)
    prow = lambda i: (jnp.minimum(i, n_prompt_tiles - 1), 0)
    const2 = lambda i: (0, 0)
    tri = jnp.tril(jnp.ones((tm, tm), BF16), -1)
    return pl.pallas_call(
        functools.partial(_outproj_kernel, n_prompt_tiles=n_prompt_tiles), grid=(n // tm,),
        in_specs=[pl.BlockSpec((tm, D_A), prow), pl.BlockSpec((tm, D_A), const2), pl.BlockSpec((tm, W_B), row),
                  pl.BlockSpec((tm, D_VC), prow), pl.BlockSpec((tm, D_VC), const2),
                  pl.BlockSpec((tm, D_MODEL), prow), pl.BlockSpec((tm, D_MODEL), const2),
                  pl.BlockSpec((D_MODEL, D_MODEL), const2),
                  pl.BlockSpec((1, D_MODEL), const2), pl.BlockSpec((1, D_MODEL), const2),
                  pl.BlockSpec((D_MODEL, N_EXPERTS), const2), pl.BlockSpec((1, N_EXPERTS), const2),
                  pl.BlockSpec((tm, tm), const2)],
        out_specs=[pl.BlockSpec((tm, D_MODEL), row), pl.BlockSpec((tm, TOP_K), row), pl.BlockSpec((tm, TOP_K), row),
                   pl.BlockSpec((tm, TOP_K), row), pl.BlockSpec((1, N_EXPERTS), const2)],
        out_shape=[jax.ShapeDtypeStruct((n, D_MODEL), F32), jax.ShapeDtypeStruct((n, TOP_K), I32),
                   jax.ShapeDtypeStruct((n, TOP_K), F32), jax.ShapeDtypeStruct((n, TOP_K), I32),
                   jax.ShapeDtypeStruct((1, N_EXPERTS), F32)],
        scratch_shapes=[pltpu.VMEM((1, N_EXPERTS), F32)],
        compiler_params=pltpu.CompilerParams(dimension_semantics=("arbitrary",), vmem_limit_bytes=VMEM_LIMIT),
        name="outproj",
    )(oa_p, oa_s, ob, oc_p, oc_s, xp, xs, w_out, g, b, rw, rb, tri)


def _dispatch_kernel(pe_ref, x_ref, dest_ref, xg_hbm, zero_s, sem):
    i = pl.program_id(0)

    @pl.when(i == 0)
    def _():
        zero_s[...] = jnp.zeros_like(zero_s)

        def fill(e):
            start = pl.multiple_of(pe_ref[e] - MOE_TILE, MOE_TILE)
            return pltpu.make_async_copy(zero_s, xg_hbm.at[pl.ds(start, MOE_TILE)], sem)

        def nonempty(e):
            return pe_ref[e] > (pe_ref[e - 1] if e else 0)

        n_blocks = xg_hbm.shape[0] // MOE_TILE
        first_tail = (n_blocks * MOE_TILE - N_EXPERTS * (MOE_TILE - 1)) // MOE_TILE

        def tail(b):
            return pltpu.make_async_copy(zero_s, xg_hbm.at[pl.ds(b * MOE_TILE, MOE_TILE)], sem)

        def unused(b):
            return b * MOE_TILE >= pe_ref[N_EXPERTS - 1]

        for e in range(N_EXPERTS):
            pl.when(nonempty(e))(lambda e=e: fill(e).start())
        for b in range(first_tail, n_blocks):
            pl.when(unused(b))(lambda b=b: tail(b).start())
        for e in range(N_EXPERTS):
            pl.when(nonempty(e))(lambda e=e: fill(e).wait())
        for b in range(first_tail, n_blocks):
            pl.when(unused(b))(lambda b=b: tail(b).wait())

    def row_copy(g, i, k):
        d = dest_ref[0, 0, (g * SUBLANES + i) * TOP_K + k]
        return pltpu.make_async_copy(x_ref.at[g, pl.ds(i, 1)], xg_hbm.at[pl.ds(d, 1)], sem)

    def for_rows(fn):
        def body(g, c):
            for i in range(SUBLANES):
                for k in range(TOP_K):
                    fn(row_copy(g, i, k))
            return c
        lax.fori_loop(0, x_ref.shape[0], body, 0)

    for_rows(lambda cp: cp.start())
    for_rows(lambda cp: cp.wait())


def _dispatch_call(pad_end, x1, dest3, n_pad):
    n = x1.shape[0]
    tm = ROW_TILE
    grid_spec = pltpu.PrefetchScalarGridSpec(
        num_scalar_prefetch=1, grid=(n // tm,),
        in_specs=[pl.BlockSpec((tm // SUBLANES, SUBLANES, D_MODEL), lambda i, pe: (i, 0, 0)),
                  pl.BlockSpec((1, 1, tm * TOP_K), lambda i, pe: (i, 0, 0), memory_space=pltpu.SMEM)],
        out_specs=pl.BlockSpec(memory_space=pl.ANY),
        scratch_shapes=[pltpu.VMEM((MOE_TILE, D_MODEL), F32), pltpu.SemaphoreType.DMA(())])
    return pl.pallas_call(
        _dispatch_kernel, grid_spec=grid_spec,
        out_shape=jax.ShapeDtypeStruct((n_pad, D_MODEL), F32),
        compiler_params=pltpu.CompilerParams(dimension_semantics=("arbitrary",), vmem_limit_bytes=VMEM_LIMIT),
        name="dispatch",
    )(pad_end, x1.reshape(n // SUBLANES, SUBLANES, D_MODEL), dest3)


def _moe_kernel(be_ref, nb_ref, xg_ref, wgu_ref, bgu_ref, wdn_ref, bdn_ref, y_ref, wgu_s, wdn_s):
    i = pl.program_id(0)

    @pl.when(i < nb_ref[0])
    def _():
        e = be_ref[i]
        prev = be_ref[jnp.maximum(i - 1, 0)]

        @pl.when((i == 0) | (e != prev))
        def _():
            wgu_s[...] = wgu_ref[0, 0].astype(BF16)
            wdn_s[...] = wdn_ref[0, 0].astype(BF16)

        h = jnp.dot(xg_ref[...].astype(BF16), wgu_s[...], preferred_element_type=F32) + bgu_ref[0, 0]
        gate = jnp.minimum(h[:, :D_FF], SWIGLU_LIMIT)
        lin = jnp.clip(h[:, D_FF:], -SWIGLU_LIMIT, SWIGLU_LIMIT)
        act = (lin + 1.0) * (gate * jax.nn.sigmoid(SWIGLU_ALPHA * gate))
        y_ref[...] = jnp.dot(act.astype(BF16), wdn_s[...], preferred_element_type=F32) + bdn_ref[0, 0]

    @pl.when(i >= nb_ref[0])
    def _():
        y_ref[...] = jnp.zeros_like(y_ref)


def _moe_call(block_e, n_used, xg, w_gu, b_gu, w_dn, b_dn, layer):
    n_pad = xg.shape[0]
    tm = MOE_TILE

    def used(i, be, nb):
        return (jnp.minimum(i, nb[0] - 1), 0)

    def expert(i, be, nb):
        return (layer, be[i], 0, 0)

    grid_spec = pltpu.PrefetchScalarGridSpec(
        num_scalar_prefetch=2, grid=(n_pad // tm,),
        in_specs=[pl.BlockSpec((tm, D_MODEL), used),
                  pl.BlockSpec((1, 1, D_MODEL, 2 * D_FF), expert), pl.BlockSpec((1, 1, 1, 2 * D_FF), expert),
                  pl.BlockSpec((1, 1, D_FF, D_MODEL), expert), pl.BlockSpec((1, 1, 1, D_MODEL), expert)],
        out_specs=pl.BlockSpec((tm, D_MODEL), lambda i, be, nb: (i, 0)),
        scratch_shapes=[pltpu.VMEM((D_MODEL, 2 * D_FF), BF16), pltpu.VMEM((D_FF, D_MODEL), BF16)])
    return pl.pallas_call(
        _moe_kernel, grid_spec=grid_spec,
        out_shape=jax.ShapeDtypeStruct((n_pad, D_MODEL), F32),
        compiler_params=pltpu.CompilerParams(dimension_semantics=("arbitrary",),
                                             vmem_limit_bytes=56 * 1024 * 1024),
        name="moe",
    )(block_e, n_used, xg, w_gu, b_gu.reshape(DEPTH, N_EXPERTS, 1, 2 * D_FF), w_dn,
      b_dn.reshape(DEPTH, N_EXPERTS, 1, D_MODEL))


def _combine_kernel(x1_ref, gate_ref, dcur_ref, dnext_ref, g_ref, b_ref, y_hbm, op_ref, os_ref, ybuf, sem):
    i = pl.program_id(0)
    tm = x1_ref.shape[0]
    slot = i % 2

    def row_copy(d_ref, s, g, i, k):
        d = d_ref[0, 0, (g * SUBLANES + i) * TOP_K + k]
        return pltpu.make_async_copy(y_hbm.at[pl.ds(d, 1)], ybuf.at[s, k, g, pl.ds(i, 1)], sem.at[s])

    def for_rows(d_ref, s, fn):
        def body(g, c):
            for i in range(SUBLANES):
                for k in range(TOP_K):
                    fn(row_copy(d_ref, s, g, i, k))
            return c
        lax.fori_loop(0, tm // SUBLANES, body, 0)

    @pl.when(i == 0)
    def _():
        for_rows(dcur_ref, 0, lambda cp: cp.start())

    @pl.when(i + 1 < pl.num_programs(0))
    def _():
        for_rows(dnext_ref, 1 - slot, lambda cp: cp.start())

    for_rows(dcur_ref, slot, lambda cp: cp.wait())
    gates = gate_ref[...]
    moe = gates[:, 0:1] * ybuf[slot, 0].reshape(tm, D_MODEL)
    for k in range(1, TOP_K):
        moe = moe + gates[:, k:k + 1] * ybuf[slot, k].reshape(tm, D_MODEL)
    res = _layer_norm(ALPHA * x1_ref[...] + moe, g_ref[...], b_ref[...])
    last = i + 1 == pl.num_programs(0)

    @pl.when(jnp.logical_not(last))
    def _():
        op_ref[...] = res

    @pl.when(last)
    def _():
        os_ref[...] = res


def _combine_call(x1, gates, dest3, g, b, y_rows):
    n = x1.shape[0]
    tm = ROW_TILE
    n_tiles = n // tm
    row = lambda i: (i, 0)
    const2 = lambda i: (0, 0)
    return pl.pallas_call(
        _combine_kernel, grid=(n_tiles,),
        in_specs=[pl.BlockSpec((tm, D_MODEL), row), pl.BlockSpec((tm, TOP_K), row),
                  pl.BlockSpec((1, 1, tm * TOP_K), lambda i: (i, 0, 0), memory_space=pltpu.SMEM),
                  pl.BlockSpec((1, 1, tm * TOP_K), lambda i: (jnp.minimum(i + 1, n_tiles - 1), 0, 0),
                               memory_space=pltpu.SMEM),
                  pl.BlockSpec((1, D_MODEL), const2), pl.BlockSpec((1, D_MODEL), const2),
                  pl.BlockSpec(memory_space=pl.ANY)],
        out_specs=[pl.BlockSpec((tm, D_MODEL), lambda i: (jnp.minimum(i, n_tiles - 2), 0)),
                   pl.BlockSpec((tm, D_MODEL), const2)],
        out_shape=[jax.ShapeDtypeStruct((n - tm, D_MODEL), F32), jax.ShapeDtypeStruct((tm, D_MODEL), F32)],
        scratch_shapes=[pltpu.VMEM((2, TOP_K, tm // SUBLANES, SUBLANES, D_MODEL), F32),
                        pltpu.SemaphoreType.DMA((2,))],
        compiler_params=pltpu.CompilerParams(dimension_semantics=("arbitrary",), vmem_limit_bytes=VMEM_LIMIT),
        name="combine",
    )(x1, gates, dest3, dest3, g, b, y_rows)


def _pack_w_in(w):
    sizes = (D_A, D_A, D_A, H_I * D_I, D_I, H_I, 2 * W_B, D_CQ, D_C, DR_C)
    offs = np.cumsum((0,) + sizes)
    seg = [w[:, offs[k]:offs[k + 1]] for k in range(len(sizes))]
    z = lambda c: jnp.zeros((w.shape[0], c), w.dtype)
    return jnp.concatenate(
        [seg[0], seg[1], seg[2], seg[3], seg[4], seg[5], z(LANES - D_I - H_I), seg[6], seg[7], seg[8],
         seg[9], z(LANES - DR_C)], axis=1).astype(BF16)


def _pack_w_uq(w):
    w3 = w.reshape(D_CQ, H_C, DN_C + DR_C)
    w3 = jnp.pad(w3, ((0, 0), (0, 0), (0, QC_SLOT - DN_C - DR_C)))
    return w3.reshape(D_CQ, D_QC).astype(BF16)


def _pack_w_kc(w_uk):
    wk = jnp.pad(w_uk.reshape(D_C, H_C, DN_C), ((0, 0), (0, 0), (0, QC_SLOT - DN_C))).reshape(D_C, D_QC)
    eye = jnp.pad(jnp.eye(DR_C, dtype=w_uk.dtype), ((0, LANES - DR_C), (DN_C, QC_SLOT - DN_C - DR_C)))
    return jnp.concatenate([wk, jnp.tile(eye, (1, H_C))], axis=0).astype(BF16)


def _rope_tables(pos):
    posf = pos.astype(F32)[:, None]
    n = pos.shape[0]

    def cs(half):
        inv = ROPE_THETA ** (-jnp.arange(half, dtype=F32) / half)
        ang = posf * inv[None, :]
        return jnp.cos(ang), jnp.sin(ang)

    c32, s32 = cs(32)
    z32 = jnp.zeros_like(s32)
    c64 = jnp.tile(jnp.concatenate([c32, c32], 1), (1, 2))
    sa64 = jnp.tile(jnp.concatenate([-s32, z32], 1), (1, 2))
    sb64 = jnp.tile(jnp.concatenate([z32, s32], 1), (1, 2))
    c16, s16 = cs(16)
    z16 = jnp.zeros_like(s16)
    one = lambda c: jnp.ones((n, c), F32)
    zero = lambda c: jnp.zeros((n, c), F32)
    cq = jnp.concatenate([one(DN_C), c16, c16, one(32)], 1)
    saq = jnp.concatenate([zero(DN_C), -s16, z16, zero(32)], 1)
    sbq = jnp.concatenate([zero(DN_C), z16, s16, zero(32)], 1)
    ck = jnp.concatenate([c16, c16, one(96)], 1)
    sak = jnp.concatenate([-s16, z16, zero(96)], 1)
    sbk = jnp.concatenate([z16, s16, zero(96)], 1)
    return [c64, sa64, sb64, cq, saq, sbq, ck, sak, sbk]


def _sgu_tables(w_s, b_s, n_sample_seq, n_sample_batch):
    i = jnp.arange(SGU_CHUNK)
    w_m = jnp.where((i[None, :] // CHUNK) <= (i[:, None] // CHUNK), w_s, 0.0)
    reps = ROW_TILE // SGU_CHUNK
    bd_p = jax.vmap(lambda m: jnp.kron(jnp.eye(reps, dtype=m.dtype), m))(w_m)
    ns = n_sample_seq
    bd_s = jax.vmap(lambda m: jnp.kron(jnp.eye(n_sample_batch, dtype=m.dtype), m[:ns, :ns]))(w_m)
    bd = jnp.stack([bd_p, bd_s]).astype(BF16)
    bias_chunk = jnp.repeat(b_s.T, DG_B, axis=1)
    bias = jnp.stack([jnp.tile(bias_chunk, (reps, 1)), jnp.tile(bias_chunk[:ns], (n_sample_batch, 1))])
    return bd, bias


def _route(top_idx, rank, counts):
    n_assign = top_idx.size
    counts = counts.reshape(N_EXPERTS).astype(I32)
    padded = ((counts + MOE_TILE - 1) // MOE_TILE) * MOE_TILE
    pad_end = jnp.cumsum(padded).astype(I32)
    pad_start = pad_end - padded
    dest = jnp.take(pad_start, top_idx) + rank
    n_blocks = -(-(n_assign + N_EXPERTS * (MOE_TILE - 1)) // MOE_TILE)
    n_used = pad_end[-1] // MOE_TILE
    blk = jnp.minimum(jnp.arange(n_blocks, dtype=I32), n_used - 1) * MOE_TILE
    block_e = jnp.minimum(jnp.sum(pad_end[None, :] <= blk[:, None], axis=1), N_EXPERTS - 1).astype(I32)
    return dest, pad_end, block_e, n_used.reshape(1), n_blocks * MOE_TILE


def _key_tiles_t(v, n_batch, k_rows, tk):
    c = v.shape[1]
    return v.reshape(n_batch, k_rows // tk, tk, c).transpose(0, 1, 3, 2).reshape(n_batch * (k_rows // tk), c, tk)


def kernel(x_prompt, x_sample, cache_a_k, cache_a_v, cache_a_kidx, cache_c_latent, cache_c_krope, w_in, mla_q_norm, mla_kv_norm, w_uq, w_uk, w_uv, sgu_ln_g, sgu_ln_b, w_spatial, b_spatial, w_out, ln1_g, ln1_b, router_w, router_b, w_gate_up, b_gate_up, w_down, b_down, ln2_g, ln2_b):
    batch, seq, _ = x_prompt.shape
    dec_batch, dec_seq, _ = x_sample.shape
    past = cache_a_k.shape[2]
    n_p, n_s = batch * seq, dec_batch * dec_seq
    assert n_p % ROW_TILE == 0 and n_s == ROW_TILE and seq % ROW_TILE == 0
    assert seq % ATT_TK == 0 and seq % ATT_TQ == 0 and dec_seq <= SAMPLE_TQ
    n_all = n_p + n_s
    tab_period = seq // ROW_TILE

    pos_p = jnp.arange(seq, dtype=I32)
    pos_s = past + jnp.arange(dec_seq, dtype=I32)
    tabs = _rope_tables(jnp.concatenate([pos_p, jnp.tile(pos_s, dec_batch)]))

    n_keys_s = past + dec_seq
    l_pad_s = -(-n_keys_s // SAMPLE_TK) * SAMPLE_TK
    top_k_p = min(TOPK_MAX, seq // 4)
    top_k_s = min(TOPK_MAX, n_keys_s // 4)

    xp, xs = x_prompt.reshape(n_p, D_MODEL), x_sample.reshape(n_s, D_MODEL)
    caches_p = [[] for _ in range(5)]
    caches_s = [[] for _ in range(6)]

    def with_cache(cache, new, width):
        allk = jnp.concatenate([cache.astype(BF16), new.reshape(dec_batch, dec_seq, width)], axis=1)
        allk = jnp.pad(allk, ((0, 0), (0, l_pad_s - n_keys_s), (0, 0)))
        return allk.reshape(dec_batch * l_pad_s, width)

    def sample_queries(arr):
        q = arr[n_p:].reshape(dec_batch, dec_seq, -1)
        return jnp.pad(q, ((0, 0), (0, SAMPLE_TQ - dec_seq), (0, 0))).reshape(dec_batch * SAMPLE_TQ, -1)

    def sample_rows(o):
        return o.reshape(dec_batch, SAMPLE_TQ, -1)[:, :dec_seq].reshape(n_s, -1)

    for l in range(DEPTH):
        w_packed = _pack_w_in(w_in[l])
        wuq = _pack_w_uq(w_uq[l])
        wkc = _pack_w_kc(w_uk[l])
        wuv = w_uv[l].astype(BF16)
        bd, bsb = _sgu_tables(w_spatial[l], b_spatial[l], dec_seq, dec_batch)
        (qa, kab, qi, kib, wi, ob, qc, kc,
         ka_p, va_p, ki_p, lat_p, kr_p, vat_p, vct_p,
         ka_s1, va_s1, ki_s1, lat_s1, kr_s1, vb_s1, vab_s, vc_new) = _proj_call(
            xp, xs, w_packed, tabs, mla_q_norm[l].reshape(1, -1), mla_kv_norm[l].reshape(1, -1), wuq, wkc, wuv,
            sgu_ln_g[l].reshape(1, -1), sgu_ln_b[l].reshape(1, -1), bd, bsb, tab_period)

        oa_p = _dsa_call(qi, wi, qa, kib, kab, vat_p, n_batch=batch, n_q=seq, q_row0=0, k_rows=seq,
                         tq=ATT_TQ, tk=ATT_TK, n_valid=seq, q_pos0=0, top_k=top_k_p)
        oc_p = _mla_call(qc, kc, vct_p, n_batch=batch, n_q=seq, q_row0=0, k_rows=seq,
                         tq=ATT_TQ, tk=ATT_TK, n_valid=seq, q_pos0=0)

        kc_c, vc_c = _kvup_call(cache_c_latent[l].reshape(dec_batch * past, D_C),
                                cache_c_krope[l].reshape(dec_batch * past, DR_C), wkc, wuv)
        ki_s = with_cache(cache_a_kidx[l], kib[n_p:], D_I)
        ka_s = with_cache(cache_a_k[l].reshape(dec_batch, past, D_A), kab[n_p:], D_A)
        va_s = with_cache(cache_a_v[l].reshape(dec_batch, past, D_A), vab_s, D_A)
        kc_s = with_cache(kc_c.reshape(dec_batch, past, D_QC), kc[n_p:], D_QC)
        vc_s = with_cache(vc_c.reshape(dec_batch, past, D_VC), vc_new, D_VC)
        oa_s = _dsa_call(sample_queries(qi), sample_queries(wi), sample_queries(qa), ki_s, ka_s,
                         _key_tiles_t(va_s, dec_batch, l_pad_s, SAMPLE_TK),
                         n_batch=dec_batch, n_q=SAMPLE_TQ, q_row0=0, k_rows=l_pad_s,
                         tq=SAMPLE_TQ, tk=SAMPLE_TK, n_valid=n_keys_s, q_pos0=past, top_k=top_k_s)
        oc_s = _mla_call(sample_queries(qc), kc_s, _key_tiles_t(vc_s, dec_batch, l_pad_s, SAMPLE_TK),
                         n_batch=dec_batch, n_q=SAMPLE_TQ, q_row0=0, k_rows=l_pad_s,
                         tq=SAMPLE_TQ, tk=SAMPLE_TK, n_valid=n_keys_s, q_pos0=past)

        x1, top_idx, gates, rank, counts = _outproj_call(
            oa_p, sample_rows(oa_s), ob, oc_p, sample_rows(oc_s), xp, xs, w_out[l].astype(BF16),
            ln1_g[l].reshape(1, -1), ln1_b[l].reshape(1, -1), router_w[l].astype(BF16), router_b[l].reshape(1, -1))

        dest, pad_end, block_e, n_used, n_pad = _route(top_idx, rank, counts)
        dest3 = dest.reshape(n_all // ROW_TILE, 1, ROW_TILE * TOP_K)
        xg = _dispatch_call(pad_end, x1, dest3, n_pad)
        y_rows = _moe_call(block_e, n_used, xg, w_gate_up, b_gate_up, w_down, b_down, l)
        xp, xs = _combine_call(x1, gates, dest3, ln2_g[l].reshape(1, -1), ln2_b[l].reshape(1, -1), y_rows)

        for dst, arr, shp in ((caches_p[0], ka_p, (H_A, DH_A)), (caches_p[1], va_p, (H_A, DH_A)),
                              (caches_p[2], ki_p, (D_I,)), (caches_p[3], lat_p, (D_C,)), (caches_p[4], kr_p, (DR_C,))):
            dst.append(arr.reshape((batch, seq) + shp))
        for dst, arr, shp in ((caches_s[0], ka_s1, (H_A, DH_A)), (caches_s[1], va_s1, (H_A, DH_A)),
                              (caches_s[2], ki_s1, (D_I,)), (caches_s[3], lat_s1, (D_C,)),
                              (caches_s[4], kr_s1, (DR_C,)), (caches_s[5], vb_s1, (W_B,))):
            dst.append(arr.reshape((dec_batch, dec_seq) + shp))

    y_prompt = xp.reshape(batch, seq, D_MODEL)
    y_sample = xs.reshape(dec_batch, dec_seq, D_MODEL)
    return (y_prompt, y_sample) + tuple(jnp.stack(c) for c in caches_p) + tuple(jnp.stack(c) for c in caches_s)
```

```python
import functools

import numpy as np
import jax
import jax.numpy as jnp
from jax import lax
from jax.experimental import pallas as pl
from jax.experimental.pallas import tpu as pltpu

F32 = jnp.float32
BF16 = jnp.bfloat16
I32 = jnp.int32
I16 = jnp.int16

D_MODEL = 1024
DEPTH = 2
CHUNK = 64
ROPE_THETA = 10000.0
LN_EPS = 1e-5
RMS_EPS = 1e-6
H_A, DH_A = 6, 64
H_I, D_I = 8, 64
TOPK_MAX = 256
G_B, DG_B = 4, 64
W_B = G_B * DG_B
SGU_CHUNK = 128
H_C, DN_C, DR_C, DV_C = 6, 64, 32, 64
D_CQ, D_C = 256, 256
MLA_SCALE = (DN_C + DR_C) ** -0.5
LOG2E = float(np.log2(np.e))
D_A = H_A * DH_A
D_VC = H_C * DV_C
N_EXPERTS = 32
TOP_K = 4
D_FF = 1024
SWIGLU_LIMIT = 7.0
SWIGLU_ALPHA = 1.702
ALPHA = (2 * DEPTH) ** 0.25

LANES = 128
SUBLANES = 8
PACKED_ROWS = 16
HALF_RANGE = 1 << 15
COUNT_CHAINS = 4
ROW_TILE = 256
MOE_TILE = 256
ATT_TQ = 256
ATT_TK = 512
MLA_TQ = 512
SAMPLE_TQ = 128
SAMPLE_TK = 384
VMEM_LIMIT = 48 * 1024 * 1024

C_QA, C_KA, C_VA, C_QI, C_KIW, C_ZU, C_ZV, C_CQ, C_CKV, C_KR = (
    0, 384, 768, 1152, 1664, 1792, 2048, 2304, 2560, 2816)
IN_PACKED = 2944
QC_SLOT = 128
D_QC = H_C * QC_SLOT

NEG_BIG = -1e30
KEY_NEG_INF = -2139095041
INT_MIN = -2147483648


def _nt_dot(a, b):
    return lax.dot_general(a, b, (((1,), (1,)), ((), ())), preferred_element_type=F32)


def _rope128(x, c, sa, sb, half):
    return x * c + pltpu.roll(x, LANES - half, 1) * sa + pltpu.roll(x, half, 1) * sb


def _proj_kernel(xp_ref, xs_ref, w_ref, c64_ref, sa64_ref, sb64_ref, cq_ref, saq_ref, sbq_ref,
                 ck_ref, sak_ref, sbk_ref, qn_ref, kvn_ref, wuq_ref, wkc_ref, wuv_ref,
                 sgug_ref, sgub_ref, bd_ref, bsb_ref,
                 qa_o, kab_o, qi_o, kib_o, wi_o, ob_o, qc_o, kc_o,
                 kap_o, vap_o, kip_o, latp_o, krp_o, vat_o, vct_o,
                 kas_o, vas_o, kis_o, lats_o, krs_o, vbs_o, vabs_o, vcs_o, *, n_prompt_tiles):
    is_prompt = pl.program_id(0) < n_prompt_tiles
    x = jnp.where(is_prompt, xp_ref[...], xs_ref[...])
    h = jnp.dot(x.astype(BF16), w_ref[...], preferred_element_type=F32)
    c64, sa64, sb64 = c64_ref[...], sa64_ref[...], sb64_ref[...]

    def rope64(col):
        return _rope128(h[:, col:col + LANES], c64, sa64, sb64, 32)

    kas = []
    for c in range(D_A // LANES):
        qa_o[:, c * LANES:(c + 1) * LANES] = (rope64(C_QA + c * LANES) * (DH_A ** -0.5 * LOG2E)).astype(BF16)
        kas.append(rope64(C_KA + c * LANES))
        kab_o[:, c * LANES:(c + 1) * LANES] = kas[c].astype(BF16)
    va = h[:, C_VA:C_VA + D_A]
    for c in range(H_I * D_I // LANES):
        qi_o[:, c * LANES:(c + 1) * LANES] = (rope64(C_QI + c * LANES) * (D_I ** -0.5)).astype(BF16)
    ki = rope64(C_KIW)[:, :D_I]
    kib_o[...] = ki.astype(BF16)
    wi_o[...] = h[:, C_KIW:C_KIW + LANES] * (H_I ** -0.5)

    z = h[:, C_ZU:C_ZU + 2 * W_B]
    z = 0.5 * z * (1.0 + jnp.tanh(np.sqrt(2.0 / np.pi) * (z + 0.044715 * (z * z * z))))
    u = z[:, :W_B]
    v = z[:, W_B:]
    mu = jnp.mean(v, axis=-1, keepdims=True)
    var = jnp.mean(jnp.square(v - mu), axis=-1, keepdims=True)
    v = (v - mu) * lax.rsqrt(var + LN_EPS) * sgug_ref[...] + sgub_ref[...]
    v16 = v.astype(BF16)
    lane = lax.broadcasted_iota(I32, (v.shape[0], LANES), 1)
    for w in range(W_B // LANES):
        vw = v16[:, w * LANES:(w + 1) * LANES]
        m0 = jnp.dot(bd_ref[0, 2 * w], vw, preferred_element_type=F32)
        m1 = jnp.dot(bd_ref[0, 2 * w + 1], vw, preferred_element_type=F32)
        mixed = jnp.where(lane < DG_B, m0, m1) + bsb_ref[0, :, w * LANES:(w + 1) * LANES]
        ob_o[:, w * LANES:(w + 1) * LANES] = (u[:, w * LANES:(w + 1) * LANES] * mixed).astype(BF16)

    cq = h[:, C_CQ:C_CQ + D_CQ]
    cq = cq * lax.rsqrt(jnp.mean(jnp.square(cq), axis=-1, keepdims=True) + RMS_EPS) * qn_ref[...]
    q = jnp.dot(cq.astype(BF16), wuq_ref[...], preferred_element_type=F32)
    cqt, saq, sbq = cq_ref[...], saq_ref[...], sbq_ref[...]
    for hh in range(H_C):
        qs = _rope128(q[:, hh * QC_SLOT:(hh + 1) * QC_SLOT], cqt, saq, sbq, DR_C // 2)
        qc_o[:, hh * QC_SLOT:(hh + 1) * QC_SLOT] = (qs * (MLA_SCALE * LOG2E)).astype(BF16)
    ckv = h[:, C_CKV:C_CKV + D_C]
    lat = ckv * lax.rsqrt(jnp.mean(jnp.square(ckv), axis=-1, keepdims=True) + RMS_EPS) * kvn_ref[...]
    kr = _rope128(h[:, C_KR:C_KR + LANES], ck_ref[...], sak_ref[...], sbk_ref[...], DR_C // 2)
    lat16 = lat.astype(BF16)
    kc = (jnp.dot(lat16, wkc_ref[:D_C, :], preferred_element_type=F32)
          + jnp.dot(kr.astype(BF16), wkc_ref[D_C:, :], preferred_element_type=F32))
    kc_o[...] = kc.astype(BF16)
    vc = jnp.dot(lat16, wuv_ref[...], preferred_element_type=F32)

    @pl.when(is_prompt)
    def _():
        for c in range(D_A // LANES):
            kap_o[:, c * LANES:(c + 1) * LANES] = kas[c]
            vat_o[0, c * LANES:(c + 1) * LANES, :] = va[:, c * LANES:(c + 1) * LANES].T.astype(BF16)
            vct_o[0, c * LANES:(c + 1) * LANES, :] = vc[:, c * LANES:(c + 1) * LANES].T.astype(BF16)
        vap_o[...] = va
        kip_o[...] = ki
        latp_o[...] = lat
        krp_o[...] = kr[:, :DR_C]

    @pl.when(jnp.logical_not(is_prompt))
    def _():
        for c in range(D_A // LANES):
            kas_o[:, c * LANES:(c + 1) * LANES] = kas[c]
        vas_o[...] = va
        kis_o[...] = ki
        lats_o[...] = lat
        krs_o[...] = kr[:, :DR_C]
        vbs_o[...] = v
        vabs_o[...] = va.astype(BF16)
        vcs_o[...] = vc.astype(BF16)


def _proj_call(xp, xs, w_packed, tabs, qn, kvn, wuq, wkc, wuv, sgug, sgub, bd, bsb, tab_period):
    n_p = xp.shape[0]
    tm = ROW_TILE
    n_prompt_tiles = n_p // tm
    n = n_p + tm
    grid = (n_prompt_tiles + 1,)
    per_kt = ATT_TK // tm

    def row(i):
        return (i, 0)

    def prow(i):
        return (jnp.minimum(i, n_prompt_tiles - 1), 0)

    def ptile(i):
        ip = jnp.minimum(i, n_prompt_tiles - 1)
        return (ip // per_kt, 0, ip % per_kt)

    def const2(i):
        return (0, 0)

    def tab(i):
        return (jnp.where(i < n_prompt_tiles, i % tab_period, tab_period), 0)

    def grp(i):
        return (jnp.where(i < n_prompt_tiles, 0, 1), 0, 0, 0)

    def grp3(i):
        return (jnp.where(i < n_prompt_tiles, 0, 1), 0, 0)

    in_specs = [pl.BlockSpec((tm, D_MODEL), prow), pl.BlockSpec((tm, D_MODEL), const2),
                pl.BlockSpec((D_MODEL, IN_PACKED), const2)]
    in_specs += [pl.BlockSpec((tm, LANES), tab)] * 9
    in_specs += [pl.BlockSpec((1, D_CQ), const2), pl.BlockSpec((1, D_C), const2),
                 pl.BlockSpec((D_CQ, D_QC), const2), pl.BlockSpec((D_C + LANES, D_QC), const2),
                 pl.BlockSpec((D_C, D_VC), const2),
                 pl.BlockSpec((1, W_B), const2), pl.BlockSpec((1, W_B), const2),
                 pl.BlockSpec((1, G_B, tm, tm), grp), pl.BlockSpec((1, tm, W_B), grp3)]
    all_rows = [(D_A, BF16), (D_A, BF16), (H_I * D_I, BF16), (D_I, BF16), (LANES, F32), (W_B, BF16),
                (D_QC, BF16), (D_QC, BF16)]
    caches = [(D_A, F32), (D_A, F32), (D_I, F32), (D_C, F32), (DR_C, F32)]
    sample_only = caches + [(W_B, F32), (D_A, BF16), (D_VC, BF16)]
    out_shape = ([jax.ShapeDtypeStruct((n, c), d) for c, d in all_rows]
                 + [jax.ShapeDtypeStruct((n_p, c), d) for c, d in caches]
                 + [jax.ShapeDtypeStruct((n_p // ATT_TK, c, ATT_TK), BF16) for c in (D_A, D_VC)]
                 + [jax.ShapeDtypeStruct((tm, c), d) for c, d in sample_only])
    out_specs = ([pl.BlockSpec((tm, c), row) for c, _ in all_rows]
                 + [pl.BlockSpec((tm, c), prow) for c, _ in caches]
                 + [pl.BlockSpec((1, c, tm), ptile) for c in (D_A, D_VC)]
                 + [pl.BlockSpec((tm, c), const2) for c, _ in sample_only])
    return pl.pallas_call(
        functools.partial(_proj_kernel, n_prompt_tiles=n_prompt_tiles),
        grid=grid, in_specs=in_specs, out_specs=out_specs, out_shape=out_shape,
        compiler_params=pltpu.CompilerParams(dimension_semantics=("arbitrary",), vmem_limit_bytes=VMEM_LIMIT),
        name="proj",
    )(xp, xs, w_packed, *tabs, qn, kvn, wuq, wkc, wuv, sgug, sgub, bd, bsb)


def _kvup_kernel(lat_ref, kr_ref, wkc_ref, wuv_ref, kc_o, vc_o):
    lat16 = lat_ref[...].astype(BF16)
    kc = (jnp.dot(lat16, wkc_ref[:D_C, :], preferred_element_type=F32)
          + jnp.dot(kr_ref[...].astype(BF16), wkc_ref[D_C:D_C + DR_C, :], preferred_element_type=F32))
    kc_o[...] = kc.astype(BF16)
    vc_o[...] = jnp.dot(lat16, wuv_ref[...], preferred_element_type=F32).astype(BF16)


def _kvup_call(lat, kr, wkc, wuv):
    n = lat.shape[0]
    tm = 1024
    return pl.pallas_call(
        _kvup_kernel, grid=(n // tm,),
        in_specs=[pl.BlockSpec((tm, D_C), lambda i: (i, 0)), pl.BlockSpec((tm, DR_C), lambda i: (i, 0)),
                  pl.BlockSpec((D_C + LANES, D_QC), lambda i: (0, 0)), pl.BlockSpec((D_C, D_VC), lambda i: (0, 0))],
        out_specs=[pl.BlockSpec((tm, D_QC), lambda i: (i, 0)), pl.BlockSpec((tm, D_VC), lambda i: (i, 0))],
        out_shape=[jax.ShapeDtypeStruct((n, D_QC), BF16), jax.ShapeDtypeStruct((n, D_VC), BF16)],
        compiler_params=pltpu.CompilerParams(dimension_semantics=("arbitrary",), vmem_limit_bytes=VMEM_LIMIT),
        name="kvup",
    )(lat, kr, wkc, wuv)


def _key_bounds(j, tq, tk, n_valid, q_pos0):
    q_first = q_pos0 + j * tq
    kmax = jnp.minimum(((q_first + tq - 1) // CHUNK + 1) * CHUNK, n_valid)
    n_tiles = (kmax + tk - 1) // tk
    n_full = jnp.minimum((q_first // CHUNK + 1) * CHUNK, n_valid) // tk
    qpos = q_first + lax.broadcasted_iota(I32, (1, tq), 1)
    bound = jnp.minimum((qpos // CHUNK + 1) * CHUNK, n_valid)
    return n_tiles, n_full, bound


def _flash_init(m_s, l_s, acc_s):
    m_s[...] = jnp.full(m_s.shape, NEG_BIG, F32)
    l_s[...] = jnp.zeros(l_s.shape, F32)
    acc_s[...] = jnp.zeros(acc_s.shape, F32)


def _flash_update(hh, s_t, v_t, m_s, l_s):
    m_old = m_s[hh]
    m_new = jnp.maximum(m_old, jnp.max(s_t, axis=0, keepdims=True))
    p = jnp.exp2(s_t - m_new)
    a = jnp.exp2(m_old - m_new)
    l_s[hh] = a * l_s[hh] + jnp.sum(p, axis=0, keepdims=True)
    m_s[hh] = m_new
    return a, jnp.dot(v_t, p.astype(BF16), preferred_element_type=F32)


def _flash_pair_accumulate(w, first_rows, upd0, upd1, acc_s):
    (a0, pv0), (a1, pv1) = upd0, upd1
    acc_s[w] = acc_s[w] * jnp.where(first_rows, a0, a1) + jnp.where(first_rows, pv0, pv1)


def _flash_heads(scores, values, n_heads, first_rows, m_s, l_s, acc_s):
    s_all = [scores(hh) for hh in range(n_heads)]
    upd = []
    for hh in range(n_heads):
        upd.append(_flash_update(hh, s_all[hh], values(hh // 2), m_s, l_s))
        if hh % 2:
            _flash_pair_accumulate(hh // 2, first_rows, upd[hh - 1], upd[hh], acc_s)


def _flash_finish(o_ref, first_rows, l_s, acc_s):
    for w in range(acc_s.shape[0]):
        l_sel = jnp.where(first_rows, l_s[2 * w], l_s[2 * w + 1])
        o_ref[:, w * LANES:(w + 1) * LANES] = (acc_s[w] / l_sel).T.astype(o_ref.dtype)


def _flash_scratch(n_heads, tq):
    return [pltpu.VMEM((n_heads, 1, tq), F32), pltpu.VMEM((n_heads, 1, tq), F32),
            pltpu.VMEM((n_heads // 2, LANES, tq), F32)]


def _dsa_kernel(qi_ref, wi_ref, qa_ref, ki_ref, ka_ref, vat_ref, o_ref, key_ref, k16_ref, m_s, l_s, acc_s, *,
                tq, tk, n_valid, q_pos0, top_k):
    j = pl.program_id(1)
    n_tiles, _, bound = _key_bounds(j, tq, tk, n_valid, q_pos0)
    row_tk = lax.broadcasted_iota(I32, (tk, tq), 0)
    row8 = lax.broadcasted_iota(I32, (SUBLANES, tq), 0)

    qi = qi_ref[...]
    q_stack = jnp.concatenate([qi[:, hh * D_I:(hh + 1) * D_I] for hh in range(H_I)], axis=0)
    wi_t = wi_ref[...].T
    wis = [wi_t[D_I + hh:D_I + hh + 1, :] for hh in range(H_I)]

    def score_tile(kt, carry):
        k0 = pl.multiple_of(kt * tk, tk)
        d = _nt_dot(ki_ref[pl.ds(k0, tk), :], q_stack)
        s = wis[0] * jnp.maximum(d[:, 0:tq], 0.0)
        for hh in range(1, H_I):
            s = s + wis[hh] * jnp.maximum(d[:, hh * tq:(hh + 1) * tq], 0.0)
        bits = pltpu.bitcast(s, I32)
        key = jnp.where(bits < 0, bits ^ 0x7FFFFFFF, bits)
        key = jnp.where(key == -1, 0, key)
        key = jnp.where(k0 + row_tk < bound, key, KEY_NEG_INF)
        key_ref[kt] = key
        k16_ref[kt] = (key >> 16).astype(I16)
        return carry

    lax.fori_loop(0, n_tiles, score_tile, 0)

    def count(pred):
        def body(kt, acc):
            for r in range(tk // SUBLANES):
                kk = key_ref[kt, r * SUBLANES:(r + 1) * SUBLANES, :]
                acc = acc + jnp.where(pred(kk, kt * tk + r * SUBLANES), 1.0, 0.0)
            return acc
        acc = lax.fori_loop(0, n_tiles, body, jnp.zeros((SUBLANES, tq), F32))
        return jnp.sum(acc, axis=0, keepdims=True)

    def count16_ge(cand):
        c16 = cand.astype(I16)

        def body(kt, accs):
            accs = list(accs)
            for r in range(tk // PACKED_ROWS):
                kk = k16_ref[kt, r * PACKED_ROWS:(r + 1) * PACKED_ROWS, :]
                accs[r % COUNT_CHAINS] = accs[r % COUNT_CHAINS] + jnp.where(kk >= c16, jnp.int16(1), jnp.int16(0))
            return tuple(accs)
        zero = jnp.zeros((PACKED_ROWS, tq), I16)
        accs = lax.fori_loop(0, n_tiles, body, (zero,) * COUNT_CHAINS)
        acc = accs[0].astype(F32)
        for a in accs[1:]:
            acc = acc + a.astype(F32)
        return jnp.sum(acc, axis=0, keepdims=True)

    kf = float(top_k)
    c0 = count16_ge(jnp.zeros((1, tq), I32))
    nonneg = c0 >= kf
    t_hi0 = jnp.where(nonneg, 0, -HALF_RANGE).astype(I32)
    cnt0 = jnp.where(nonneg, c0, (n_tiles * tk).astype(F32))

    def hi_body(it, carry):
        t_hi, cnt_t = carry
        cand = t_hi | lax.shift_left(jnp.int32(1), 14 - it)
        cnt = count16_ge(cand)
        ok = cnt >= kf
        return jnp.where(ok, cand, t_hi), jnp.where(ok, cnt, cnt_t)

    t_hi, cnt_hi = lax.fori_loop(0, 15, hi_body, (t_hi0, cnt0))
    need_lo = kf - count16_ge(t_hi + 1)
    cnt_class = cnt_hi - (kf - need_lo)

    def lo_tile(kt, carry):
        kk = key_ref[kt]
        lo = (kk & 0xFFFF) - HALF_RANGE
        k16_ref[kt] = jnp.where((kk >> 16) == t_hi, lo, -HALF_RANGE).astype(I16)
        return carry

    lax.fori_loop(0, n_tiles, lo_tile, 0)

    def lo_cond(carry):
        it, _, cnt_t = carry
        return (it < 16) & (jnp.max(jnp.abs(cnt_t - need_lo)) > 0.5)

    def lo_body(carry):
        it, t_lo, cnt_t = carry
        cand = t_lo | lax.shift_left(jnp.int32(1), 15 - it)
        cnt = count16_ge(cand - HALF_RANGE)
        ok = cnt >= need_lo
        return it + 1, jnp.where(ok, cand, t_lo), jnp.where(ok, cnt, cnt_t)

    _, t_lo, cnt_lo = lax.while_loop(lo_cond, lo_body, (jnp.int32(0), jnp.zeros((1, tq), I32), cnt_class))
    t = lax.shift_left(t_hi, 16) | t_lo
    inexact = jnp.max(jnp.abs(cnt_lo - need_lo)) > 0.5

    @pl.when(inexact)
    def _():
        c_gt = count(lambda kk, base: kk > t)
        c_eq = count(lambda kk, base: kk == t)
        need = kf - c_gt

        @pl.when(jnp.max(jnp.where((c_eq > need) & (t > KEY_NEG_INF), 1.0, 0.0)) > 0.5)
        def _():
            def jb(it, jj):
                cand = jj | lax.shift_left(jnp.int32(1), 12 - it)
                f = count(lambda kk, base: (kk == t) & (base + row8 < cand))
                return jnp.where(f < need, cand, jj)
            j_last = lax.fori_loop(0, 13, jb, jnp.zeros((1, tq), I32))

            def drop_tile(kt, carry):
                kk = key_ref[kt]
                key_ref[kt] = jnp.where((kk == t) & (kt * tk + row_tk > j_last), KEY_NEG_INF, kk)
                return carry

            lax.fori_loop(0, n_tiles, drop_tile, 0)

    t_member = jnp.maximum(t, KEY_NEG_INF + 1)

    lane = lax.broadcasted_iota(I32, (tq, LANES), 1)
    first_rows = lax.broadcasted_iota(I32, (LANES, tq), 0) < DH_A
    qa = qa_ref[...]
    half_masks = [jnp.where(lane < DH_A, 1.0, 0.0).astype(BF16), jnp.where(lane < DH_A, 0.0, 1.0).astype(BF16)]
    qhs = [qa[:, (hh // 2) * LANES:(hh // 2 + 1) * LANES] * half_masks[hh % 2] for hh in range(H_A)]
    _flash_init(m_s, l_s, acc_s)

    def att_tile(kt, carry):
        k0 = pl.multiple_of(kt * tk, tk)
        bias = jnp.where(key_ref[kt] >= t_member, 0.0, NEG_BIG)

        def scores(hh):
            return _nt_dot(ka_ref[pl.ds(k0, tk), (hh // 2) * LANES:(hh // 2 + 1) * LANES], qhs[hh]) + bias

        _flash_heads(scores, lambda w: vat_ref[kt, w * LANES:(w + 1) * LANES, :], H_A, first_rows, m_s, l_s, acc_s)
        return carry

    lax.fori_loop(0, n_tiles, att_tile, 0)
    _flash_finish(o_ref, first_rows, l_s, acc_s)


def _dsa_call(qi, wi, qa, ki, ka, vat, *, n_batch, n_q, q_row0, k_rows, tq, tk, n_valid, q_pos0, top_k):
    nq_blocks = n_q // tq
    qb0 = q_row0 // tq
    n_kt = k_rows // tk
    kern = functools.partial(_dsa_kernel, tq=tq, tk=tk, n_valid=n_valid, q_pos0=q_pos0, top_k=top_k)

    def qmap(b, j):
        return (qb0 + b * nq_blocks + j, 0)

    def kmap(b, j):
        return (b, 0)

    return pl.pallas_call(
        kern, grid=(n_batch, nq_blocks),
        in_specs=[pl.BlockSpec((tq, H_I * D_I), qmap), pl.BlockSpec((tq, LANES), qmap), pl.BlockSpec((tq, D_A), qmap),
                  pl.BlockSpec((k_rows, D_I), kmap), pl.BlockSpec((k_rows, D_A), kmap),
                  pl.BlockSpec((n_kt, D_A, tk), lambda b, j: (b, 0, 0))],
        out_specs=pl.BlockSpec((tq, D_A), lambda b, j: (b * nq_blocks + j, 0)),
        out_shape=jax.ShapeDtypeStruct((n_batch * n_q, D_A), BF16),
        scratch_shapes=[pltpu.VMEM((n_kt, tk, tq), I32), pltpu.VMEM((n_kt, tk, tq), I16)] + _flash_scratch(H_A, tq),
        compiler_params=pltpu.CompilerParams(dimension_semantics=("arbitrary", "arbitrary"),
                                             vmem_limit_bytes=VMEM_LIMIT),
        name="dsa",
    )(qi, wi, qa, ki, ka, vat)


def _mla_kernel(qc_ref, kc_ref, vct_ref, o_ref, m_s, l_s, acc_s, *, tq, tk, n_valid, q_pos0):
    j = pl.program_id(1)
    n_tiles, n_full, bound = _key_bounds(j, tq, tk, n_valid, q_pos0)
    row_tk = lax.broadcasted_iota(I32, (tk, tq), 0)
    first_rows = lax.broadcasted_iota(I32, (LANES, tq), 0) < DV_C
    qc = qc_ref[...]
    qhs = [qc[:, hh * QC_SLOT:(hh + 1) * QC_SLOT] for hh in range(H_C)]
    _flash_init(m_s, l_s, acc_s)

    def tile(kt, carry, masked):
        k0 = pl.multiple_of(kt * tk, tk)
        if masked:
            bias = jnp.where(k0 + row_tk < bound, 0.0, NEG_BIG)

        def scores(hh):
            s_t = _nt_dot(kc_ref[pl.ds(k0, tk), hh * QC_SLOT:(hh + 1) * QC_SLOT], qhs[hh])
            return s_t + bias if masked else s_t

        _flash_heads(scores, lambda w: vct_ref[kt, w * LANES:(w + 1) * LANES, :], H_C, first_rows, m_s, l_s, acc_s)
        return carry

    lax.fori_loop(0, n_full, functools.partial(tile, masked=False), 0)
    lax.fori_loop(n_full, n_tiles, functools.partial(tile, masked=True), 0)
    _flash_finish(o_ref, first_rows, l_s, acc_s)


def _mla_call(qc, kc, vct, *, n_batch, n_q, q_row0, k_rows, tq, tk, n_valid, q_pos0):
    nq_blocks = n_q // tq
    qb0 = q_row0 // tq
    n_kt = k_rows // tk
    kern = functools.partial(_mla_kernel, tq=tq, tk=tk, n_valid=n_valid, q_pos0=q_pos0)
    return pl.pallas_call(
        kern, grid=(n_batch, nq_blocks),
        in_specs=[pl.BlockSpec((tq, D_QC), lambda b, j: (qb0 + b * nq_blocks + j, 0)),
                  pl.BlockSpec((k_rows, D_QC), lambda b, j: (b, 0)),
                  pl.BlockSpec((n_kt, D_VC, tk), lambda b, j: (b, 0, 0))],
        out_specs=pl.BlockSpec((tq, D_VC), lambda b, j: (b * nq_blocks + j, 0)),
        out_shape=jax.ShapeDtypeStruct((n_batch * n_q, D_VC), BF16),
        scratch_shapes=_flash_scratch(H_C, tq),
        compiler_params=pltpu.CompilerParams(dimension_semantics=("arbitrary", "arbitrary"),
                                             vmem_limit_bytes=VMEM_LIMIT),
        name="mla",
    )(qc, kc, vct)


def _layer_norm(y, g, b):
    mu = jnp.mean(y, axis=-1, keepdims=True)
    var = jnp.mean(jnp.square(y - mu), axis=-1, keepdims=True)
    return (y - mu) * lax.rsqrt(var + LN_EPS) * g + b


def _outproj_kernel(oap_ref, oas_ref, ob_ref, ocp_ref, ocs_ref, xp_ref, xs_ref, w_ref, g_ref, b_ref, rw_ref, rb_ref,
                    tri_ref, x1_o, idx_o, gate_o, rank_o, cnt_o, run_s, *, n_prompt_tiles):
    @pl.when(pl.program_id(0) == 0)
    def _():
        run_s[...] = jnp.zeros_like(run_s)

    is_prompt = pl.program_id(0) < n_prompt_tiles
    oa = jnp.where(is_prompt, oap_ref[...], oas_ref[...])
    oc = jnp.where(is_prompt, ocp_ref[...], ocs_ref[...])
    x = jnp.where(is_prompt, xp_ref[...], xs_ref[...])
    mix = (jnp.dot(oa, w_ref[0:D_A, :], preferred_element_type=F32)
           + jnp.dot(ob_ref[...], w_ref[D_A:D_A + W_B, :], preferred_element_type=F32)
           + jnp.dot(oc, w_ref[D_A + W_B:, :], preferred_element_type=F32))
    x1 = _layer_norm(ALPHA * x + mix, g_ref[...], b_ref[...])
    x1_o[...] = x1
    lg = jnp.dot(x1.astype(BF16), rw_ref[...], preferred_element_type=F32) + rb_ref[...]
    e = lax.broadcasted_iota(I32, lg.shape, 1).astype(F32)
    vals, hots = [], []
    for k in range(TOP_K):
        m = jnp.max(lg, axis=1, keepdims=True)
        idx = jnp.min(jnp.where(lg == m, e, float(N_EXPERTS)), axis=1, keepdims=True)
        vals.append(m)
        hots.append(jnp.where(e == idx, 1.0, 0.0))
        idx_o[:, k:k + 1] = idx.astype(I32)
        lg = jnp.where(e == idx, -jnp.inf, lg)
    ex = [jnp.exp(vv - vals[0]) for vv in vals]
    den = ex[0] + ex[1] + ex[2] + ex[3]
    for k in range(TOP_K):
        gate_o[:, k:k + 1] = ex[k] / den
    per_token = hots[0] + hots[1] + hots[2] + hots[3]
    before = run_s[...] + jnp.dot(tri_ref[...], per_token.astype(BF16), preferred_element_type=F32)
    for k in range(TOP_K):
        rank_o[:, k:k + 1] = jnp.sum(hots[k] * before, axis=1, keepdims=True).astype(I32)
    run_s[...] = run_s[...] + jnp.sum(per_token, axis=0, keepdims=True)
    cnt_o[...] = run_s[...]


def _outproj_call(oa_p, oa_s, ob, oc_p, oc_s, xp, xs, w_out, g, b, rw, rb):
    tm = ROW_TILE
    n_prompt_tiles = xp.shape[0] // tm
    n = xp.shape[0] + tm
    row = lambda i: (i, 0)
    prow = lambda i: (jnp.minimum(i, n_prompt_tiles - 1), 0)
    const2 = lambda i: (0, 0)
    tri = jnp.tril(jnp.ones((tm, tm), BF16), -1)
    return pl.pallas_call(
        functools.partial(_outproj_kernel, n_prompt_tiles=n_prompt_tiles), grid=(n // tm,),
        in_specs=[pl.BlockSpec((tm, D_A), prow), pl.BlockSpec((tm, D_A), const2), pl.BlockSpec((tm, W_B), row),
                  pl.BlockSpec((tm, D_VC), prow), pl.BlockSpec((tm, D_VC), const2),
                  pl.BlockSpec((tm, D_MODEL), prow), pl.BlockSpec((tm, D_MODEL), const2),
                  pl.BlockSpec((D_MODEL, D_MODEL), const2),
                  pl.BlockSpec((1, D_MODEL), const2), pl.BlockSpec((1, D_MODEL), const2),
                  pl.BlockSpec((D_MODEL, N_EXPERTS), const2), pl.BlockSpec((1, N_EXPERTS), const2),
                  pl.BlockSpec((tm, tm), const2)],
        out_specs=[pl.BlockSpec((tm, D_MODEL), row), pl.BlockSpec((tm, TOP_K), row), pl.BlockSpec((tm, TOP_K), row),
                   pl.BlockSpec((tm, TOP_K), row), pl.BlockSpec((1, N_EXPERTS), const2)],
        out_shape=[jax.ShapeDtypeStruct((n, D_MODEL), F32), jax.ShapeDtypeStruct((n, TOP_K), I32),
                   jax.ShapeDtypeStruct((n, TOP_K), F32), jax.ShapeDtypeStruct((n, TOP_K), I32),
                   jax.ShapeDtypeStruct((1, N_EXPERTS), F32)],
        scratch_shapes=[pltpu.VMEM((1, N_EXPERTS), F32)],
        compiler_params=pltpu.CompilerParams(dimension_semantics=("arbitrary",), vmem_limit_bytes=VMEM_LIMIT),
        name="outproj",
    )(oa_p, oa_s, ob, oc_p, oc_s, xp, xs, w_out, g, b, rw, rb, tri)


def _dispatch_kernel(pe_ref, x_ref, dest_ref, xg_hbm, zero_s, sem):
    i = pl.program_id(0)

    @pl.when(i == 0)
    def _():
        zero_s[...] = jnp.zeros_like(zero_s)

        def fill(e):
            start = pl.multiple_of(pe_ref[e] - MOE_TILE, MOE_TILE)
            return pltpu.make_async_copy(zero_s, xg_hbm.at[pl.ds(start, MOE_TILE)], sem)

        def nonempty(e):
            return pe_ref[e] > (pe_ref[e - 1] if e else 0)

        n_blocks = xg_hbm.shape[0] // MOE_TILE
        first_tail = (n_blocks * MOE_TILE - N_EXPERTS * (MOE_TILE - 1)) // MOE_TILE

        def tail(b):
            return pltpu.make_async_copy(zero_s, xg_hbm.at[pl.ds(b * MOE_TILE, MOE_TILE)], sem)

        def unused(b):
            return b * MOE_TILE >= pe_ref[N_EXPERTS - 1]

        for e in range(N_EXPERTS):
            pl.when(nonempty(e))(lambda e=e: fill(e).start())
        for b in range(first_tail, n_blocks):
            pl.when(unused(b))(lambda b=b: tail(b).start())
        for e in range(N_EXPERTS):
            pl.when(nonempty(e))(lambda e=e: fill(e).wait())
        for b in range(first_tail, n_blocks):
            pl.when(unused(b))(lambda b=b: tail(b).wait())

    def row_copy(g, i, k):
        d = dest_ref[0, 0, (g * SUBLANES + i) * TOP_K + k]
        return pltpu.make_async_copy(x_ref.at[g, pl.ds(i, 1)], xg_hbm.at[pl.ds(d, 1)], sem)

    def for_rows(fn):
        def body(g, c):
            for i in range(SUBLANES):
                for k in range(TOP_K):
                    fn(row_copy(g, i, k))
            return c
        lax.fori_loop(0, x_ref.shape[0], body, 0)

    for_rows(lambda cp: cp.start())
    for_rows(lambda cp: cp.wait())


def _dispatch_call(pad_end, x1, dest3, n_pad):
    n = x1.shape[0]
    tm = ROW_TILE
    grid_spec = pltpu.PrefetchScalarGridSpec(
        num_scalar_prefetch=1, grid=(n // tm,),
        in_specs=[pl.BlockSpec((tm // SUBLANES, SUBLANES, D_MODEL), lambda i, pe: (i, 0, 0)),
                  pl.BlockSpec((1, 1, tm * TOP_K), lambda i, pe: (i, 0, 0), memory_space=pltpu.SMEM)],
        out_specs=pl.BlockSpec(memory_space=pl.ANY),
        scratch_shapes=[pltpu.VMEM((MOE_TILE, D_MODEL), F32), pltpu.SemaphoreType.DMA(())])
    return pl.pallas_call(
        _dispatch_kernel, grid_spec=grid_spec,
        out_shape=jax.ShapeDtypeStruct((n_pad, D_MODEL), F32),
        compiler_params=pltpu.CompilerParams(dimension_semantics=("arbitrary",), vmem_limit_bytes=VMEM_LIMIT),
        name="dispatch",
    )(pad_end, x1.reshape(n // SUBLANES, SUBLANES, D_MODEL), dest3)


def _moe_kernel(be_ref, nb_ref, xg_ref, wgu_ref, bgu_ref, wdn_ref, bdn_ref, y_ref, wgu_s, wdn_s):
    i = pl.program_id(0)

    @pl.when(i < nb_ref[0])
    def _():
        e = be_ref[i]
        prev = be_ref[jnp.maximum(i - 1, 0)]

        @pl.when((i == 0) | (e != prev))
        def _():
            wgu_s[...] = wgu_ref[0, 0].astype(BF16)
            wdn_s[...] = wdn_ref[0, 0].astype(BF16)

        h = jnp.dot(xg_ref[...].astype(BF16), wgu_s[...], preferred_element_type=F32) + bgu_ref[0, 0]
        gate = jnp.minimum(h[:, :D_FF], SWIGLU_LIMIT)
        lin = jnp.clip(h[:, D_FF:], -SWIGLU_LIMIT, SWIGLU_LIMIT)
        act = (lin + 1.0) * (gate * jax.nn.sigmoid(SWIGLU_ALPHA * gate))
        y_ref[...] = jnp.dot(act.astype(BF16), wdn_s[...], preferred_element_type=F32) + bdn_ref[0, 0]

    @pl.when(i >= nb_ref[0])
    def _():
        y_ref[...] = jnp.zeros_like(y_ref)


def _moe_call(block_e, n_used, xg, w_gu, b_gu, w_dn, b_dn, layer):
    n_pad = xg.shape[0]
    tm = MOE_TILE

    def used(i, be, nb):
        return (jnp.minimum(i, nb[0] - 1), 0)

    def expert(i, be, nb):
        return (layer, be[i], 0, 0)

    grid_spec = pltpu.PrefetchScalarGridSpec(
        num_scalar_prefetch=2, grid=(n_pad // tm,),
        in_specs=[pl.BlockSpec((tm, D_MODEL), used),
                  pl.BlockSpec((1, 1, D_MODEL, 2 * D_FF), expert), pl.BlockSpec((1, 1, 1, 2 * D_FF), expert),
                  pl.BlockSpec((1, 1, D_FF, D_MODEL), expert), pl.BlockSpec((1, 1, 1, D_MODEL), expert)],
        out_specs=pl.BlockSpec((tm, D_MODEL), lambda i, be, nb: (i, 0)),
        scratch_shapes=[pltpu.VMEM((D_MODEL, 2 * D_FF), BF16), pltpu.VMEM((D_FF, D_MODEL), BF16)])
    return pl.pallas_call(
        _moe_kernel, grid_spec=grid_spec,
        out_shape=jax.ShapeDtypeStruct((n_pad, D_MODEL), F32),
        compiler_params=pltpu.CompilerParams(dimension_semantics=("arbitrary",),
                                             vmem_limit_bytes=56 * 1024 * 1024),
        name="moe",
    )(block_e, n_used, xg, w_gu, b_gu.reshape(DEPTH, N_EXPERTS, 1, 2 * D_FF), w_dn,
      b_dn.reshape(DEPTH, N_EXPERTS, 1, D_MODEL))


def _combine_kernel(x1_ref, gate_ref, dcur_ref, dnext_ref, g_ref, b_ref, y_hbm, op_ref, os_ref, ybuf, sem):
    i = pl.program_id(0)
    tm = x1_ref.shape[0]
    slot = i % 2

    def row_copy(d_ref, s, g, i, k):
        d = d_ref[0, 0, (g * SUBLANES + i) * TOP_K + k]
        return pltpu.make_async_copy(y_hbm.at[pl.ds(d, 1)], ybuf.at[s, k, g, pl.ds(i, 1)], sem.at[s])

    def for_rows(d_ref, s, fn):
        def body(g, c):
            for i in range(SUBLANES):
                for k in range(TOP_K):
                    fn(row_copy(d_ref, s, g, i, k))
            return c
        lax.fori_loop(0, tm // SUBLANES, body, 0)

    @pl.when(i == 0)
    def _():
        for_rows(dcur_ref, 0, lambda cp: cp.start())

    @pl.when(i + 1 < pl.num_programs(0))
    def _():
        for_rows(dnext_ref, 1 - slot, lambda cp: cp.start())

    for_rows(dcur_ref, slot, lambda cp: cp.wait())
    gates = gate_ref[...]
    moe = gates[:, 0:1] * ybuf[slot, 0].reshape(tm, D_MODEL)
    for k in range(1, TOP_K):
        moe = moe + gates[:, k:k + 1] * ybuf[slot, k].reshape(tm, D_MODEL)
    res = _layer_norm(ALPHA * x1_ref[...] + moe, g_ref[...], b_ref[...])
    last = i + 1 == pl.num_programs(0)

    @pl.when(jnp.logical_not(last))
    def _():
        op_ref[...] = res

    @pl.when(last)
    def _():
        os_ref[...] = res


def _combine_call(x1, gates, dest3, g, b, y_rows):
    n = x1.shape[0]
    tm = ROW_TILE
    n_tiles = n // tm
    row = lambda i: (i, 0)
    const2 = lambda i: (0, 0)
    return pl.pallas_call(
        _combine_kernel, grid=(n_tiles,),
        in_specs=[pl.BlockSpec((tm, D_MODEL), row), pl.BlockSpec((tm, TOP_K), row),
                  pl.BlockSpec((1, 1, tm * TOP_K), lambda i: (i, 0, 0), memory_space=pltpu.SMEM),
                  pl.BlockSpec((1, 1, tm * TOP_K), lambda i: (jnp.minimum(i + 1, n_tiles - 1), 0, 0),
                               memory_space=pltpu.SMEM),
                  pl.BlockSpec((1, D_MODEL), const2), pl.BlockSpec((1, D_MODEL), const2),
                  pl.BlockSpec(memory_space=pl.ANY)],
        out_specs=[pl.BlockSpec((tm, D_MODEL), lambda i: (jnp.minimum(i, n_tiles - 2), 0)),
                   pl.BlockSpec((tm, D_MODEL), const2)],
        out_shape=[jax.ShapeDtypeStruct((n - tm, D_MODEL), F32), jax.ShapeDtypeStruct((tm, D_MODEL), F32)],
        scratch_shapes=[pltpu.VMEM((2, TOP_K, tm // SUBLANES, SUBLANES, D_MODEL), F32),
                        pltpu.SemaphoreType.DMA((2,))],
        compiler_params=pltpu.CompilerParams(dimension_semantics=("arbitrary",), vmem_limit_bytes=VMEM_LIMIT),
        name="combine",
    )(x1, gates, dest3, dest3, g, b, y_rows)


def _pack_w_in(w):
    sizes = (D_A, D_A, D_A, H_I * D_I, D_I, H_I, 2 * W_B, D_CQ, D_C, DR_C)
    offs = np.cumsum((0,) + sizes)
    seg = [w[:, offs[k]:offs[k + 1]] for k in range(len(sizes))]
    z = lambda c: jnp.zeros((w.shape[0], c), w.dtype)
    return jnp.concatenate(
        [seg[0], seg[1], seg[2], seg[3], seg[4], seg[5], z(LANES - D_I - H_I), seg[6], seg[7], seg[8],
         seg[9], z(LANES - DR_C)], axis=1).astype(BF16)


def _pack_w_uq(w):
    w3 = w.reshape(D_CQ, H_C, DN_C + DR_C)
    w3 = jnp.pad(w3, ((0, 0), (0, 0), (0, QC_SLOT - DN_C - DR_C)))
    return w3.reshape(D_CQ, D_QC).astype(BF16)


def _pack_w_kc(w_uk):
    wk = jnp.pad(w_uk.reshape(D_C, H_C, DN_C), ((0, 0), (0, 0), (0, QC_SLOT - DN_C))).reshape(D_C, D_QC)
    eye = jnp.pad(jnp.eye(DR_C, dtype=w_uk.dtype), ((0, LANES - DR_C), (DN_C, QC_SLOT - DN_C - DR_C)))
    return jnp.concatenate([wk, jnp.tile(eye, (1, H_C))], axis=0).astype(BF16)


def _rope_tables(pos):
    posf = pos.astype(F32)[:, None]
    n = pos.shape[0]

    def cs(half):
        inv = ROPE_THETA ** (-jnp.arange(half, dtype=F32) / half)
        ang = posf * inv[None, :]
        return jnp.cos(ang), jnp.sin(ang)

    c32, s32 = cs(32)
    z32 = jnp.zeros_like(s32)
    c64 = jnp.tile(jnp.concatenate([c32, c32], 1), (1, 2))
    sa64 = jnp.tile(jnp.concatenate([-s32, z32], 1), (1, 2))
    sb64 = jnp.tile(jnp.concatenate([z32, s32], 1), (1, 2))
    c16, s16 = cs(16)
    z16 = jnp.zeros_like(s16)
    one = lambda c: jnp.ones((n, c), F32)
    zero = lambda c: jnp.zeros((n, c), F32)
    cq = jnp.concatenate([one(DN_C), c16, c16, one(32)], 1)
    saq = jnp.concatenate([zero(DN_C), -s16, z16, zero(32)], 1)
    sbq = jnp.concatenate([zero(DN_C), z16, s16, zero(32)], 1)
    ck = jnp.concatenate([c16, c16, one(96)], 1)
    sak = jnp.concatenate([-s16, z16, zero(96)], 1)
    sbk = jnp.concatenate([z16, s16, zero(96)], 1)
    return [c64, sa64, sb64, cq, saq, sbq, ck, sak, sbk]


def _sgu_tables(w_s, b_s, n_sample_seq, n_sample_batch):
    i = jnp.arange(SGU_CHUNK)
    w_m = jnp.where((i[None, :] // CHUNK) <= (i[:, None] // CHUNK), w_s, 0.0)
    reps = ROW_TILE // SGU_CHUNK
    bd_p = jax.vmap(lambda m: jnp.kron(jnp.eye(reps, dtype=m.dtype), m))(w_m)
    ns = n_sample_seq
    bd_s = jax.vmap(lambda m: jnp.kron(jnp.eye(n_sample_batch, dtype=m.dtype), m[:ns, :ns]))(w_m)
    bd = jnp.stack([bd_p, bd_s]).astype(BF16)
    bias_chunk = jnp.repeat(b_s.T, DG_B, axis=1)
    bias = jnp.stack([jnp.tile(bias_chunk, (reps, 1)), jnp.tile(bias_chunk[:ns], (n_sample_batch, 1))])
    return bd, bias


def _route(top_idx, rank, counts):
    n_assign = top_idx.size
    counts = counts.reshape(N_EXPERTS).astype(I32)
    padded = ((counts + MOE_TILE - 1) // MOE_TILE) * MOE_TILE
    pad_end = jnp.cumsum(padded).astype(I32)
    pad_start = pad_end - padded
    dest = jnp.take(pad_start, top_idx) + rank
    n_blocks = -(-(n_assign + N_EXPERTS * (MOE_TILE - 1)) // MOE_TILE)
    n_used = pad_end[-1] // MOE_TILE
    blk = jnp.minimum(jnp.arange(n_blocks, dtype=I32), n_used - 1) * MOE_TILE
    block_e = jnp.minimum(jnp.sum(pad_end[None, :] <= blk[:, None], axis=1), N_EXPERTS - 1).astype(I32)
    return dest, pad_end, block_e, n_used.reshape(1), n_blocks * MOE_TILE


def _key_tiles_t(v, n_batch, k_rows, tk):
    c = v.shape[1]
    return v.reshape(n_batch, k_rows // tk, tk, c).transpose(0, 1, 3, 2).reshape(n_batch * (k_rows // tk), c, tk)


def kernel(x_prompt, x_sample, cache_a_k, cache_a_v, cache_a_kidx, cache_c_latent, cache_c_krope, w_in, mla_q_norm, mla_kv_norm, w_uq, w_uk, w_uv, sgu_ln_g, sgu_ln_b, w_spatial, b_spatial, w_out, ln1_g, ln1_b, router_w, router_b, w_gate_up, b_gate_up, w_down, b_down, ln2_g, ln2_b):
    batch, seq, _ = x_prompt.shape
    dec_batch, dec_seq, _ = x_sample.shape
    past = cache_a_k.shape[2]
    n_p, n_s = batch * seq, dec_batch * dec_seq
    assert n_p % ROW_TILE == 0 and n_s == ROW_TILE and seq % ROW_TILE == 0
    assert seq % ATT_TK == 0 and seq % ATT_TQ == 0 and dec_seq <= SAMPLE_TQ
    n_all = n_p + n_s
    tab_period = seq // ROW_TILE

    pos_p = jnp.arange(seq, dtype=I32)
    pos_s = past + jnp.arange(dec_seq, dtype=I32)
    tabs = _rope_tables(jnp.concatenate([pos_p, jnp.tile(pos_s, dec_batch)]))

    n_keys_s = past + dec_seq
    l_pad_s = -(-n_keys_s // SAMPLE_TK) * SAMPLE_TK
    top_k_p = min(TOPK_MAX, seq // 4)
    top_k_s = min(TOPK_MAX, n_keys_s // 4)

    xp, xs = x_prompt.reshape(n_p, D_MODEL), x_sample.reshape(n_s, D_MODEL)
    caches_p = [[] for _ in range(5)]
    caches_s = [[] for _ in range(6)]

    def with_cache(cache, new, width):
        allk = jnp.concatenate([cache.astype(BF16), new.reshape(dec_batch, dec_seq, width)], axis=1)
        allk = jnp.pad(allk, ((0, 0), (0, l_pad_s - n_keys_s), (0, 0)))
        return allk.reshape(dec_batch * l_pad_s, width)

    def sample_queries(arr):
        q = arr[n_p:].reshape(dec_batch, dec_seq, -1)
        return jnp.pad(q, ((0, 0), (0, SAMPLE_TQ - dec_seq), (0, 0))).reshape(dec_batch * SAMPLE_TQ, -1)

    def sample_rows(o):
        return o.reshape(dec_batch, SAMPLE_TQ, -1)[:, :dec_seq].reshape(n_s, -1)

    for l in range(DEPTH):
        w_packed = _pack_w_in(w_in[l])
        wuq = _pack_w_uq(w_uq[l])
        wkc = _pack_w_kc(w_uk[l])
        wuv = w_uv[l].astype(BF16)
        bd, bsb = _sgu_tables(w_spatial[l], b_spatial[l], dec_seq, dec_batch)
        (qa, kab, qi, kib, wi, ob, qc, kc,
         ka_p, va_p, ki_p, lat_p, kr_p, vat_p, vct_p,
         ka_s1, va_s1, ki_s1, lat_s1, kr_s1, vb_s1, vab_s, vc_new) = _proj_call(
            xp, xs, w_packed, tabs, mla_q_norm[l].reshape(1, -1), mla_kv_norm[l].reshape(1, -1), wuq, wkc, wuv,
            sgu_ln_g[l].reshape(1, -1), sgu_ln_b[l].reshape(1, -1), bd, bsb, tab_period)

        oa_p = _dsa_call(qi, wi, qa, kib, kab, vat_p, n_batch=batch, n_q=seq, q_row0=0, k_rows=seq,
                         tq=ATT_TQ, tk=ATT_TK, n_valid=seq, q_pos0=0, top_k=top_k_p)
        oc_p = _mla_call(qc, kc, vct_p, n_batch=batch, n_q=seq, q_row0=0, k_rows=seq,
                         tq=MLA_TQ, tk=ATT_TK, n_valid=seq, q_pos0=0)

        kc_c, vc_c = _kvup_call(cache_c_latent[l].reshape(dec_batch * past, D_C),
                                cache_c_krope[l].reshape(dec_batch * past, DR_C), wkc, wuv)
        ki_s = with_cache(cache_a_kidx[l], kib[n_p:], D_I)
        ka_s = with_cache(cache_a_k[l].reshape(dec_batch, past, D_A), kab[n_p:], D_A)
        va_s = with_cache(cache_a_v[l].reshape(dec_batch, past, D_A), vab_s, D_A)
        kc_s = with_cache(kc_c.reshape(dec_batch, past, D_QC), kc[n_p:], D_QC)
        vc_s = with_cache(vc_c.reshape(dec_batch, past, D_VC), vc_new, D_VC)
        oa_s = _dsa_call(sample_queries(qi), sample_queries(wi), sample_queries(qa), ki_s, ka_s,
                         _key_tiles_t(va_s, dec_batch, l_pad_s, SAMPLE_TK),
                         n_batch=dec_batch, n_q=SAMPLE_TQ, q_row0=0, k_rows=l_pad_s,
                         tq=SAMPLE_TQ, tk=SAMPLE_TK, n_valid=n_keys_s, q_pos0=past, top_k=top_k_s)
        oc_s = _mla_call(sample_queries(qc), kc_s, _key_tiles_t(vc_s, dec_batch, l_pad_s, SAMPLE_TK),
                         n_batch=dec_batch, n_q=SAMPLE_TQ, q_row0=0, k_rows=l_pad_s,
                         tq=SAMPLE_TQ, tk=SAMPLE_TK, n_valid=n_keys_s, q_pos0=past)

        x1, top_idx, gates, rank, counts = _outproj_call(
            oa_p, sample_rows(oa_s), ob, oc_p, sample_rows(oc_s), xp, xs, w_out[l].astype(BF16),
            ln1_g[l].reshape(1, -1), ln1_b[l].reshape(1, -1), router_w[l].astype(BF16), router_b[l].reshape(1, -1))

        dest, pad_end, block_e, n_used, n_pad = _route(top_idx, rank, counts)
        dest3 = dest.reshape(n_all // ROW_TILE, 1, ROW_TILE * TOP_K)
        xg = _dispatch_call(pad_end, x1, dest3, n_pad)
        y_rows = _moe_call(block_e, n_used, xg, w_gate_up, b_gate_up, w_down, b_down, l)
        xp, xs = _combine_call(x1, gates, dest3, ln2_g[l].reshape(1, -1), ln2_b[l].reshape(1, -1), y_rows)

        for dst, arr, shp in ((caches_p[0], ka_p, (H_A, DH_A)), (caches_p[1], va_p, (H_A, DH_A)),
                              (caches_p[2], ki_p, (D_I,)), (caches_p[3], lat_p, (D_C,)), (caches_p[4], kr_p, (DR_C,))):
            dst.append(arr.reshape((batch, seq) + shp))
        for dst, arr, shp in ((caches_s[0], ka_s1, (H_A, DH_A)), (caches_s[1], va_s1, (H_A, DH_A)),
                              (caches_s[2], ki_s1, (D_I,)), (caches_s[3], lat_s1, (D_C,)),
                              (caches_s[4], kr_s1, (DR_C,)), (caches_s[5], vb_s1, (W_B,))):
            dst.append(arr.reshape((dec_batch, dec_seq) + shp))

    y_prompt = xp.reshape(batch, seq, D_MODEL)
    y_sample = xs.reshape(dec_batch, dec_seq, D_MODEL)
    return (y_prompt, y_sample) + tuple(jnp.stack(c) for c in caches_p) + tuple(jnp.stack(c) for c in caches_s)
```

```python
import functools

import numpy as np
import jax
import jax.numpy as jnp
from jax import lax
from jax.experimental import pallas as pl
from jax.experimental.pallas import tpu as pltpu

F32 = jnp.float32
BF16 = jnp.bfloat16
I32 = jnp.int32
I16 = jnp.int16

D_MODEL = 1024
DEPTH = 2
CHUNK = 64
ROPE_THETA = 10000.0
LN_EPS = 1e-5
RMS_EPS = 1e-6
H_A, DH_A = 6, 64
H_I, D_I = 8, 64
TOPK_MAX = 256
G_B, DG_B = 4, 64
W_B = G_B * DG_B
SGU_CHUNK = 128
H_C, DN_C, DR_C, DV_C = 6, 64, 32, 64
D_CQ, D_C = 256, 256
MLA_SCALE = (DN_C + DR_C) ** -0.5
LOG2E = float(np.log2(np.e))
D_A = H_A * DH_A
D_VC = H_C * DV_C
N_EXPERTS = 32
TOP_K = 4
D_FF = 1024
SWIGLU_LIMIT = 7.0
SWIGLU_ALPHA = 1.702
ALPHA = (2 * DEPTH) ** 0.25

LANES = 128
SUBLANES = 8
PACKED_ROWS = 16
HALF_RANGE = 1 << 15
COUNT_CHAINS = 4
LO_STEPS = 4
ROW_TILE = 256
MOE_TILE = 256
ATT_TQ = 256
ATT_TK = 512
MLA_TQ = 512
SAMPLE_TQ = 128
SAMPLE_TK = 384
VMEM_LIMIT = 48 * 1024 * 1024

C_QA, C_KA, C_VA, C_QI, C_KIW, C_ZU, C_ZV, C_CQ, C_CKV, C_KR = (
    0, 384, 768, 1152, 1664, 1792, 2048, 2304, 2560, 2816)
IN_PACKED = 2944
QC_SLOT = 128
D_QC = H_C * QC_SLOT

NEG_BIG = -1e30
KEY_NEG_INF = -2139095041
INT_MIN = -2147483648


def _nt_dot(a, b):
    return lax.dot_general(a, b, (((1,), (1,)), ((), ())), preferred_element_type=F32)


def _rope128(x, c, sa, sb, half):
    return x * c + pltpu.roll(x, LANES - half, 1) * sa + pltpu.roll(x, half, 1) * sb


def _proj_kernel(xp_ref, xs_ref, w_ref, c64_ref, sa64_ref, sb64_ref, cq_ref, saq_ref, sbq_ref,
                 ck_ref, sak_ref, sbk_ref, qn_ref, kvn_ref, wuq_ref, wkc_ref, wuv_ref,
                 sgug_ref, sgub_ref, bd_ref, bsb_ref,
                 qa_o, kab_o, qi_o, kib_o, wi_o, ob_o, qc_o, kc_o,
                 kap_o, vap_o, kip_o, latp_o, krp_o, vat_o, vct_o,
                 kas_o, vas_o, kis_o, lats_o, krs_o, vbs_o, vabs_o, vcs_o, *, n_prompt_tiles):
    is_prompt = pl.program_id(0) < n_prompt_tiles
    x = jnp.where(is_prompt, xp_ref[...], xs_ref[...])
    h = jnp.dot(x.astype(BF16), w_ref[...], preferred_element_type=F32)
    c64, sa64, sb64 = c64_ref[...], sa64_ref[...], sb64_ref[...]

    def rope64(col):
        return _rope128(h[:, col:col + LANES], c64, sa64, sb64, 32)

    kas = []
    for c in range(D_A // LANES):
        qa_o[:, c * LANES:(c + 1) * LANES] = (rope64(C_QA + c * LANES) * (DH_A ** -0.5 * LOG2E)).astype(BF16)
        kas.append(rope64(C_KA + c * LANES))
        kab_o[:, c * LANES:(c + 1) * LANES] = kas[c].astype(BF16)
    va = h[:, C_VA:C_VA + D_A]
    for c in range(H_I * D_I // LANES):
        qi_o[:, c * LANES:(c + 1) * LANES] = (rope64(C_QI + c * LANES) * (D_I ** -0.5)).astype(BF16)
    ki = rope64(C_KIW)[:, :D_I]
    kib_o[...] = ki.astype(BF16)
    wi_o[...] = h[:, C_KIW:C_KIW + LANES] * (H_I ** -0.5)

    z = h[:, C_ZU:C_ZU + 2 * W_B]
    z = 0.5 * z * (1.0 + jnp.tanh(np.sqrt(2.0 / np.pi) * (z + 0.044715 * (z * z * z))))
    u = z[:, :W_B]
    v = z[:, W_B:]
    mu = jnp.mean(v, axis=-1, keepdims=True)
    var = jnp.mean(jnp.square(v - mu), axis=-1, keepdims=True)
    v = (v - mu) * lax.rsqrt(var + LN_EPS) * sgug_ref[...] + sgub_ref[...]
    v16 = v.astype(BF16)
    lane = lax.broadcasted_iota(I32, (v.shape[0], LANES), 1)
    for w in range(W_B // LANES):
        vw = v16[:, w * LANES:(w + 1) * LANES]
        m0 = jnp.dot(bd_ref[0, 2 * w], vw, preferred_element_type=F32)
        m1 = jnp.dot(bd_ref[0, 2 * w + 1], vw, preferred_element_type=F32)
        mixed = jnp.where(lane < DG_B, m0, m1) + bsb_ref[0, :, w * LANES:(w + 1) * LANES]
        ob_o[:, w * LANES:(w + 1) * LANES] = (u[:, w * LANES:(w + 1) * LANES] * mixed).astype(BF16)

    cq = h[:, C_CQ:C_CQ + D_CQ]
    cq = cq * lax.rsqrt(jnp.mean(jnp.square(cq), axis=-1, keepdims=True) + RMS_EPS) * qn_ref[...]
    q = jnp.dot(cq.astype(BF16), wuq_ref[...], preferred_element_type=F32)
    cqt, saq, sbq = cq_ref[...], saq_ref[...], sbq_ref[...]
    for hh in range(H_C):
        qs = _rope128(q[:, hh * QC_SLOT:(hh + 1) * QC_SLOT], cqt, saq, sbq, DR_C // 2)
        qc_o[:, hh * QC_SLOT:(hh + 1) * QC_SLOT] = (qs * (MLA_SCALE * LOG2E)).astype(BF16)
    ckv = h[:, C_CKV:C_CKV + D_C]
    lat = ckv * lax.rsqrt(jnp.mean(jnp.square(ckv), axis=-1, keepdims=True) + RMS_EPS) * kvn_ref[...]
    kr = _rope128(h[:, C_KR:C_KR + LANES], ck_ref[...], sak_ref[...], sbk_ref[...], DR_C // 2)
    lat16 = lat.astype(BF16)
    kc = (jnp.dot(lat16, wkc_ref[:D_C, :], preferred_element_type=F32)
          + jnp.dot(kr.astype(BF16), wkc_ref[D_C:, :], preferred_element_type=F32))
    kc_o[...] = kc.astype(BF16)
    vc = jnp.dot(lat16, wuv_ref[...], preferred_element_type=F32)

    @pl.when(is_prompt)
    def _():
        for c in range(D_A // LANES):
            kap_o[:, c * LANES:(c + 1) * LANES] = kas[c]
            vat_o[0, c * LANES:(c + 1) * LANES, :] = va[:, c * LANES:(c + 1) * LANES].T.astype(BF16)
            vct_o[0, c * LANES:(c + 1) * LANES, :] = vc[:, c * LANES:(c + 1) * LANES].T.astype(BF16)
        vap_o[...] = va
        kip_o[...] = ki
        latp_o[...] = lat
        krp_o[...] = kr[:, :DR_C]

    @pl.when(jnp.logical_not(is_prompt))
    def _():
        for c in range(D_A // LANES):
            kas_o[:, c * LANES:(c + 1) * LANES] = kas[c]
        vas_o[...] = va
        kis_o[...] = ki
        lats_o[...] = lat
        krs_o[...] = kr[:, :DR_C]
        vbs_o[...] = v
        vabs_o[...] = va.astype(BF16)
        vcs_o[...] = vc.astype(BF16)


def _proj_call(xp, xs, w_packed, tabs, qn, kvn, wuq, wkc, wuv, sgug, sgub, bd, bsb, tab_period):
    n_p = xp.shape[0]
    tm = ROW_TILE
    n_prompt_tiles = n_p // tm
    n = n_p + tm
    grid = (n_prompt_tiles + 1,)
    per_kt = ATT_TK // tm

    def row(i):
        return (i, 0)

    def prow(i):
        return (jnp.minimum(i, n_prompt_tiles - 1), 0)

    def ptile(i):
        ip = jnp.minimum(i, n_prompt_tiles - 1)
        return (ip // per_kt, 0, ip % per_kt)

    def const2(i):
        return (0, 0)

    def tab(i):
        return (jnp.where(i < n_prompt_tiles, i % tab_period, tab_period), 0)

    def grp(i):
        return (jnp.where(i < n_prompt_tiles, 0, 1), 0, 0, 0)

    def grp3(i):
        return (jnp.where(i < n_prompt_tiles, 0, 1), 0, 0)

    in_specs = [pl.BlockSpec((tm, D_MODEL), prow), pl.BlockSpec((tm, D_MODEL), const2),
                pl.BlockSpec((D_MODEL, IN_PACKED), const2)]
    in_specs += [pl.BlockSpec((tm, LANES), tab)] * 9
    in_specs += [pl.BlockSpec((1, D_CQ), const2), pl.BlockSpec((1, D_C), const2),
                 pl.BlockSpec((D_CQ, D_QC), const2), pl.BlockSpec((D_C + LANES, D_QC), const2),
                 pl.BlockSpec((D_C, D_VC), const2),
                 pl.BlockSpec((1, W_B), const2), pl.BlockSpec((1, W_B), const2),
                 pl.BlockSpec((1, G_B, tm, tm), grp), pl.BlockSpec((1, tm, W_B), grp3)]
    all_rows = [(D_A, BF16), (D_A, BF16), (H_I * D_I, BF16), (D_I, BF16), (LANES, F32), (W_B, BF16),
                (D_QC, BF16), (D_QC, BF16)]
    caches = [(D_A, F32), (D_A, F32), (D_I, F32), (D_C, F32), (DR_C, F32)]
    sample_only = caches + [(W_B, F32), (D_A, BF16), (D_VC, BF16)]
    out_shape = ([jax.ShapeDtypeStruct((n, c), d) for c, d in all_rows]
                 + [jax.ShapeDtypeStruct((n_p, c), d) for c, d in caches]
                 + [jax.ShapeDtypeStruct((n_p // ATT_TK, c, ATT_TK), BF16) for c in (D_A, D_VC)]
                 + [jax.ShapeDtypeStruct((tm, c), d) for c, d in sample_only])
    out_specs = ([pl.BlockSpec((tm, c), row) for c, _ in all_rows]
                 + [pl.BlockSpec((tm, c), prow) for c, _ in caches]
                 + [pl.BlockSpec((1, c, tm), ptile) for c in (D_A, D_VC)]
                 + [pl.BlockSpec((tm, c), const2) for c, _ in sample_only])
    return pl.pallas_call(
        functools.partial(_proj_kernel, n_prompt_tiles=n_prompt_tiles),
        grid=grid, in_specs=in_specs, out_specs=out_specs, out_shape=out_shape,
        compiler_params=pltpu.CompilerParams(dimension_semantics=("arbitrary",), vmem_limit_bytes=VMEM_LIMIT),
        name="proj",
    )(xp, xs, w_packed, *tabs, qn, kvn, wuq, wkc, wuv, sgug, sgub, bd, bsb)


def _kvup_kernel(lat_ref, kr_ref, wkc_ref, wuv_ref, kc_o, vc_o):
    lat16 = lat_ref[...].astype(BF16)
    kc = (jnp.dot(lat16, wkc_ref[:D_C, :], preferred_element_type=F32)
          + jnp.dot(kr_ref[...].astype(BF16), wkc_ref[D_C:D_C + DR_C, :], preferred_element_type=F32))
    kc_o[...] = kc.astype(BF16)
    vc_o[...] = jnp.dot(lat16, wuv_ref[...], preferred_element_type=F32).astype(BF16)


def _kvup_call(lat, kr, wkc, wuv):
    n = lat.shape[0]
    tm = 1024
    return pl.pallas_call(
        _kvup_kernel, grid=(n // tm,),
        in_specs=[pl.BlockSpec((tm, D_C), lambda i: (i, 0)), pl.BlockSpec((tm, DR_C), lambda i: (i, 0)),
                  pl.BlockSpec((D_C + LANES, D_QC), lambda i: (0, 0)), pl.BlockSpec((D_C, D_VC), lambda i: (0, 0))],
        out_specs=[pl.BlockSpec((tm, D_QC), lambda i: (i, 0)), pl.BlockSpec((tm, D_VC), lambda i: (i, 0))],
        out_shape=[jax.ShapeDtypeStruct((n, D_QC), BF16), jax.ShapeDtypeStruct((n, D_VC), BF16)],
        compiler_params=pltpu.CompilerParams(dimension_semantics=("arbitrary",), vmem_limit_bytes=VMEM_LIMIT),
        name="kvup",
    )(lat, kr, wkc, wuv)


def _key_bounds(j, tq, tk, n_valid, q_pos0):
    q_first = q_pos0 + j * tq
    kmax = jnp.minimum(((q_first + tq - 1) // CHUNK + 1) * CHUNK, n_valid)
    n_tiles = (kmax + tk - 1) // tk
    n_full = jnp.minimum((q_first // CHUNK + 1) * CHUNK, n_valid) // tk
    qpos = q_first + lax.broadcasted_iota(I32, (1, tq), 1)
    bound = jnp.minimum((qpos // CHUNK + 1) * CHUNK, n_valid)
    return n_tiles, n_full, bound


def _flash_init(m_s, l_s, acc_s):
    m_s[...] = jnp.full(m_s.shape, NEG_BIG, F32)
    l_s[...] = jnp.zeros(l_s.shape, F32)
    acc_s[...] = jnp.zeros(acc_s.shape, F32)


def _flash_update(hh, s_t, v_t, m_s, l_s):
    m_old = m_s[hh]
    m_new = jnp.maximum(m_old, jnp.max(s_t, axis=0, keepdims=True))
    p = jnp.exp2(s_t - m_new)
    a = jnp.exp2(m_old - m_new)
    l_s[hh] = a * l_s[hh] + jnp.sum(p, axis=0, keepdims=True)
    m_s[hh] = m_new
    return a, jnp.dot(v_t, p.astype(BF16), preferred_element_type=F32)


def _flash_pair_accumulate(w, first_rows, upd0, upd1, acc_s):
    (a0, pv0), (a1, pv1) = upd0, upd1
    acc_s[w] = acc_s[w] * jnp.where(first_rows, a0, a1) + jnp.where(first_rows, pv0, pv1)


def _flash_heads(scores, values, n_heads, first_rows, m_s, l_s, acc_s):
    s_all = [scores(hh) for hh in range(n_heads)]
    upd = []
    for hh in range(n_heads):
        upd.append(_flash_update(hh, s_all[hh], values(hh // 2), m_s, l_s))
        if hh % 2:
            _flash_pair_accumulate(hh // 2, first_rows, upd[hh - 1], upd[hh], acc_s)


def _flash_finish(o_ref, first_rows, l_s, acc_s):
    for w in range(acc_s.shape[0]):
        l_sel = jnp.where(first_rows, l_s[2 * w], l_s[2 * w + 1])
        o_ref[:, w * LANES:(w + 1) * LANES] = (acc_s[w] / l_sel).T.astype(o_ref.dtype)


def _flash_scratch(n_heads, tq):
    return [pltpu.VMEM((n_heads, 1, tq), F32), pltpu.VMEM((n_heads, 1, tq), F32),
            pltpu.VMEM((n_heads // 2, LANES, tq), F32)]


def _dsa_kernel(qi_ref, wi_ref, qa_ref, ki_ref, ka_ref, vat_ref, o_ref, key_ref, k16_ref, m_s, l_s, acc_s, *,
                tq, tk, n_valid, q_pos0, top_k):
    j = pl.program_id(1)
    n_tiles, _, bound = _key_bounds(j, tq, tk, n_valid, q_pos0)
    row_tk = lax.broadcasted_iota(I32, (tk, tq), 0)
    row8 = lax.broadcasted_iota(I32, (SUBLANES, tq), 0)

    qi = qi_ref[...]
    q_stack = jnp.concatenate([qi[:, hh * D_I:(hh + 1) * D_I] for hh in range(H_I)], axis=0)
    wi_t = wi_ref[...].T
    wis = [wi_t[D_I + hh:D_I + hh + 1, :] for hh in range(H_I)]

    def score_tile(kt, carry):
        k0 = pl.multiple_of(kt * tk, tk)
        d = _nt_dot(ki_ref[pl.ds(k0, tk), :], q_stack)
        s = wis[0] * jnp.maximum(d[:, 0:tq], 0.0)
        for hh in range(1, H_I):
            s = s + wis[hh] * jnp.maximum(d[:, hh * tq:(hh + 1) * tq], 0.0)
        bits = pltpu.bitcast(s, I32)
        key = jnp.where(bits < 0, bits ^ 0x7FFFFFFF, bits)
        key = jnp.where(key == -1, 0, key)
        key = jnp.where(k0 + row_tk < bound, key, KEY_NEG_INF)
        key_ref[kt] = key
        k16_ref[kt] = (key >> 16).astype(I16)
        return carry

    lax.fori_loop(0, n_tiles, score_tile, 0)

    def count(pred):
        def body(kt, acc):
            for r in range(tk // SUBLANES):
                kk = key_ref[kt, r * SUBLANES:(r + 1) * SUBLANES, :]
                acc = acc + jnp.where(pred(kk, kt * tk + r * SUBLANES), 1.0, 0.0)
            return acc
        acc = lax.fori_loop(0, n_tiles, body, jnp.zeros((SUBLANES, tq), F32))
        return jnp.sum(acc, axis=0, keepdims=True)

    def count16_ge(cand):
        c16 = cand.astype(I16)

        def body(kt, accs):
            accs = list(accs)
            for r in range(tk // PACKED_ROWS):
                kk = k16_ref[kt, r * PACKED_ROWS:(r + 1) * PACKED_ROWS, :]
                accs[r % COUNT_CHAINS] = accs[r % COUNT_CHAINS] + jnp.where(kk >= c16, jnp.int16(1), jnp.int16(0))
            return tuple(accs)
        zero = jnp.zeros((PACKED_ROWS, tq), I16)
        accs = lax.fori_loop(0, n_tiles, body, (zero,) * COUNT_CHAINS)
        acc = accs[0].astype(F32)
        for a in accs[1:]:
            acc = acc + a.astype(F32)
        return jnp.sum(acc, axis=0, keepdims=True)

    kf = float(top_k)
    c0 = count16_ge(jnp.zeros((1, tq), I32))
    nonneg = c0 >= kf
    t_hi0 = jnp.where(nonneg, 0, -HALF_RANGE).astype(I32)
    cnt0 = jnp.where(nonneg, c0, (n_tiles * tk).astype(F32))

    def hi_body(it, carry):
        t_hi, cnt_t = carry
        cand = t_hi | lax.shift_left(jnp.int32(1), 14 - it)
        cnt = count16_ge(cand)
        ok = cnt >= kf
        return jnp.where(ok, cand, t_hi), jnp.where(ok, cnt, cnt_t)

    t_hi, cnt_hi = lax.fori_loop(0, 15, hi_body, (t_hi0, cnt0))
    need_lo = kf - count16_ge(t_hi + 1)
    cnt_class = cnt_hi - (kf - need_lo)

    def lo_tile(kt, carry):
        kk = key_ref[kt]
        lo = (kk & 0xFFFF) - HALF_RANGE
        k16_ref[kt] = jnp.where((kk >> 16) == t_hi, lo, -HALF_RANGE).astype(I16)
        return carry

    lax.fori_loop(0, n_tiles, lo_tile, 0)

    def lo_cond(carry):
        it, _, cnt_t = carry
        return (it < 16) & (jnp.max(jnp.abs(cnt_t - need_lo)) > 0.5)

    def lo_body(carry):
        it, t_lo, cnt_t = carry
        for step in range(LO_STEPS):
            cand = t_lo | lax.shift_left(jnp.int32(1), 15 - (it + step))
            cnt = count16_ge(cand - HALF_RANGE)
            ok = cnt >= need_lo
            t_lo, cnt_t = jnp.where(ok, cand, t_lo), jnp.where(ok, cnt, cnt_t)
        return it + LO_STEPS, t_lo, cnt_t

    _, t_lo, cnt_lo = lax.while_loop(lo_cond, lo_body, (jnp.int32(0), jnp.zeros((1, tq), I32), cnt_class))
    t = lax.shift_left(t_hi, 16) | t_lo
    inexact = jnp.max(jnp.abs(cnt_lo - need_lo)) > 0.5

    @pl.when(inexact)
    def _():
        c_gt = count(lambda kk, base: kk > t)
        c_eq = count(lambda kk, base: kk == t)
        need = kf - c_gt

        @pl.when(jnp.max(jnp.where((c_eq > need) & (t > KEY_NEG_INF), 1.0, 0.0)) > 0.5)
        def _():
            def jb(it, jj):
                cand = jj | lax.shift_left(jnp.int32(1), 12 - it)
                f = count(lambda kk, base: (kk == t) & (base + row8 < cand))
                return jnp.where(f < need, cand, jj)
            j_last = lax.fori_loop(0, 13, jb, jnp.zeros((1, tq), I32))

            def drop_tile(kt, carry):
                kk = key_ref[kt]
                key_ref[kt] = jnp.where((kk == t) & (kt * tk + row_tk > j_last), KEY_NEG_INF, kk)
                return carry

            lax.fori_loop(0, n_tiles, drop_tile, 0)

    t_member = jnp.maximum(t, KEY_NEG_INF + 1)

    lane = lax.broadcasted_iota(I32, (tq, LANES), 1)
    first_rows = lax.broadcasted_iota(I32, (LANES, tq), 0) < DH_A
    qa = qa_ref[...]
    half_masks = [jnp.where(lane < DH_A, 1.0, 0.0).astype(BF16), jnp.where(lane < DH_A, 0.0, 1.0).astype(BF16)]
    qhs = [qa[:, (hh // 2) * LANES:(hh // 2 + 1) * LANES] * half_masks[hh % 2] for hh in range(H_A)]
    _flash_init(m_s, l_s, acc_s)

    def att_tile(kt, carry):
        k0 = pl.multiple_of(kt * tk, tk)
        bias = jnp.where(key_ref[kt] >= t_member, 0.0, NEG_BIG)

        def scores(hh):
            return _nt_dot(ka_ref[pl.ds(k0, tk), (hh // 2) * LANES:(hh // 2 + 1) * LANES], qhs[hh]) + bias

        _flash_heads(scores, lambda w: vat_ref[kt, w * LANES:(w + 1) * LANES, :], H_A, first_rows, m_s, l_s, acc_s)
        return carry

    lax.fori_loop(0, n_tiles, att_tile, 0)
    _flash_finish(o_ref, first_rows, l_s, acc_s)


def _dsa_call(qi, wi, qa, ki, ka, vat, *, n_batch, n_q, q_row0, k_rows, tq, tk, n_valid, q_pos0, top_k):
    nq_blocks = n_q // tq
    qb0 = q_row0 // tq
    n_kt = k_rows // tk
    kern = functools.partial(_dsa_kernel, tq=tq, tk=tk, n_valid=n_valid, q_pos0=q_pos0, top_k=top_k)

    def qmap(b, j):
        return (qb0 + b * nq_blocks + j, 0)

    def kmap(b, j):
        return (b, 0)

    return pl.pallas_call(
        kern, grid=(n_batch, nq_blocks),
        in_specs=[pl.BlockSpec((tq, H_I * D_I), qmap), pl.BlockSpec((tq, LANES), qmap), pl.BlockSpec((tq, D_A), qmap),
                  pl.BlockSpec((k_rows, D_I), kmap), pl.BlockSpec((k_rows, D_A), kmap),
                  pl.BlockSpec((n_kt, D_A, tk), lambda b, j: (b, 0, 0))],
        out_specs=pl.BlockSpec((tq, D_A), lambda b, j: (b * nq_blocks + j, 0)),
        out_shape=jax.ShapeDtypeStruct((n_batch * n_q, D_A), BF16),
        scratch_shapes=[pltpu.VMEM((n_kt, tk, tq), I32), pltpu.VMEM((n_kt, tk, tq), I16)] + _flash_scratch(H_A, tq),
        compiler_params=pltpu.CompilerParams(dimension_semantics=("arbitrary", "arbitrary"),
                                             vmem_limit_bytes=VMEM_LIMIT),
        name="dsa",
    )(qi, wi, qa, ki, ka, vat)


def _mla_kernel(qc_ref, kc_ref, vct_ref, o_ref, m_s, l_s, acc_s, *, tq, tk, n_valid, q_pos0):
    j = pl.program_id(1)
    n_tiles, n_full, bound = _key_bounds(j, tq, tk, n_valid, q_pos0)
    row_tk = lax.broadcasted_iota(I32, (tk, tq), 0)
    first_rows = lax.broadcasted_iota(I32, (LANES, tq), 0) < DV_C
    qc = qc_ref[...]
    qhs = [qc[:, hh * QC_SLOT:(hh + 1) * QC_SLOT] for hh in range(H_C)]
    _flash_init(m_s, l_s, acc_s)

    def tile(kt, carry, masked):
        k0 = pl.multiple_of(kt * tk, tk)
        if masked:
            bias = jnp.where(k0 + row_tk < bound, 0.0, NEG_BIG)

        def scores(hh):
            s_t = _nt_dot(kc_ref[pl.ds(k0, tk), hh * QC_SLOT:(hh + 1) * QC_SLOT], qhs[hh])
            return s_t + bias if masked else s_t

        _flash_heads(scores, lambda w: vct_ref[kt, w * LANES:(w + 1) * LANES, :], H_C, first_rows, m_s, l_s, acc_s)
        return carry

    lax.fori_loop(0, n_full, functools.partial(tile, masked=False), 0)
    lax.fori_loop(n_full, n_tiles, functools.partial(tile, masked=True), 0)
    _flash_finish(o_ref, first_rows, l_s, acc_s)


def _mla_call(qc, kc, vct, *, n_batch, n_q, q_row0, k_rows, tq, tk, n_valid, q_pos0):
    nq_blocks = n_q // tq
    qb0 = q_row0 // tq
    n_kt = k_rows // tk
    kern = functools.partial(_mla_kernel, tq=tq, tk=tk, n_valid=n_valid, q_pos0=q_pos0)
    return pl.pallas_call(
        kern, grid=(n_batch, nq_blocks),
        in_specs=[pl.BlockSpec((tq, D_QC), lambda b, j: (qb0 + b * nq_blocks + j, 0)),
                  pl.BlockSpec((k_rows, D_QC), lambda b, j: (b, 0)),
                  pl.BlockSpec((n_kt, D_VC, tk), lambda b, j: (b, 0, 0))],
        out_specs=pl.BlockSpec((tq, D_VC), lambda b, j: (b * nq_blocks + j, 0)),
        out_shape=jax.ShapeDtypeStruct((n_batch * n_q, D_VC), BF16),
        scratch_shapes=_flash_scratch(H_C, tq),
        compiler_params=pltpu.CompilerParams(dimension_semantics=("arbitrary", "arbitrary"),
                                             vmem_limit_bytes=VMEM_LIMIT),
        name="mla",
    )(qc, kc, vct)


def _layer_norm(y, g, b):
    mu = jnp.mean(y, axis=-1, keepdims=True)
    var = jnp.mean(jnp.square(y - mu), axis=-1, keepdims=True)
    return (y - mu) * lax.rsqrt(var + LN_EPS) * g + b


def _outproj_kernel(oap_ref, oas_ref, ob_ref, ocp_ref, ocs_ref, xp_ref, xs_ref, w_ref, g_ref, b_ref, rw_ref, rb_ref,
                    tri_ref, x1_o, idx_o, gate_o, rank_o, cnt_o, run_s, *, n_prompt_tiles):
    @pl.when(pl.program_id(0) == 0)
    def _():
        run_s[...] = jnp.zeros_like(run_s)

    is_prompt = pl.program_id(0) < n_prompt_tiles
    oa = jnp.where(is_prompt, oap_ref[...], oas_ref[...])
    oc = jnp.where(is_prompt, ocp_ref[...], ocs_ref[...])
    x = jnp.where(is_prompt, xp_ref[...], xs_ref[...])
    mix = (jnp.dot(oa, w_ref[0:D_A, :], preferred_element_type=F32)
           + jnp.dot(ob_ref[...], w_ref[D_A:D_A + W_B, :], preferred_element_type=F32)
           + jnp.dot(oc, w_ref[D_A + W_B:, :], preferred_element_type=F32))
    x1 = _layer_norm(ALPHA * x + mix, g_ref[...], b_ref[...])
    x1_o[...] = x1
    lg = _nt_dot(rw_ref[...], x1.astype(BF16)) + rb_ref[...]
    e = lax.broadcasted_iota(I32, lg.shape, 0).astype(F32)
    vals, hots = [], []
    for k in range(TOP_K):
        m = jnp.max(lg, axis=0, keepdims=True)
        idx = jnp.min(jnp.where(lg == m, e, float(N_EXPERTS)), axis=0, keepdims=True)
        vals.append(m)
        hots.append(jnp.where(e == idx, 1.0, 0.0))
        idx_o[k:k + 1, :] = idx.astype(I32)
        lg = jnp.where(e == idx, -jnp.inf, lg)
    ex = [jnp.exp(vv - vals[0]) for vv in vals]
    den = ex[0] + ex[1] + ex[2] + ex[3]
    for k in range(TOP_K):
        gate_o[k:k + 1, :] = ex[k] / den
    per_token = hots[0] + hots[1] + hots[2] + hots[3]
    before = run_s[...] + jnp.dot(per_token.astype(BF16), tri_ref[...], preferred_element_type=F32)
    for k in range(TOP_K):
        rank_o[k:k + 1, :] = jnp.sum(hots[k] * before, axis=0, keepdims=True).astype(I32)
    run_s[...] = run_s[...] + jnp.sum(per_token, axis=1, keepdims=True)
    cnt_o[...] = run_s[...]


def _outproj_call(oa_p, oa_s, ob, oc_p, oc_s, xp, xs, w_out, g, b, rw, rb):
    tm = ROW_TILE
    n_prompt_tiles = xp.shape[0] // tm
    n = xp.shape[0] + tm
    row = lambda i: (i, 0)
    prow = lambda i: (jnp.minimum(i, n_prompt_tiles - 1), 0)
    const2 = lambda i: (0, 0)
    tri = jnp.triu(jnp.ones((tm, tm), BF16), 1)
    col = lambda i: (0, i)
    return pl.pallas_call(
        functools.partial(_outproj_kernel, n_prompt_tiles=n_prompt_tiles), grid=(n // tm,),
        in_specs=[pl.BlockSpec((tm, D_A), prow), pl.BlockSpec((tm, D_A), const2), pl.BlockSpec((tm, W_B), row),
                  pl.BlockSpec((tm, D_VC), prow), pl.BlockSpec((tm, D_VC), const2),
                  pl.BlockSpec((tm, D_MODEL), prow), pl.BlockSpec((tm, D_MODEL), const2),
                  pl.BlockSpec((D_MODEL, D_MODEL), const2),
                  pl.BlockSpec((1, D_MODEL), const2), pl.BlockSpec((1, D_MODEL), const2),
                  pl.BlockSpec((N_EXPERTS, D_MODEL), const2), pl.BlockSpec((N_EXPERTS, 1), const2),
                  pl.BlockSpec((tm, tm), const2)],
        out_specs=[pl.BlockSpec((tm, D_MODEL), row), pl.BlockSpec((TOP_K, tm), col), pl.BlockSpec((TOP_K, tm), col),
                   pl.BlockSpec((TOP_K, tm), col), pl.BlockSpec((N_EXPERTS, 1), const2)],
        out_shape=[jax.ShapeDtypeStruct((n, D_MODEL), F32), jax.ShapeDtypeStruct((TOP_K, n), I32),
                   jax.ShapeDtypeStruct((TOP_K, n), F32), jax.ShapeDtypeStruct((TOP_K, n), I32),
                   jax.ShapeDtypeStruct((N_EXPERTS, 1), F32)],
        scratch_shapes=[pltpu.VMEM((N_EXPERTS, 1), F32)],
        compiler_params=pltpu.CompilerParams(dimension_semantics=("arbitrary",), vmem_limit_bytes=VMEM_LIMIT),
        name="outproj",
    )(oa_p, oa_s, ob, oc_p, oc_s, xp, xs, w_out, g, b, rw, rb, tri)


def _dispatch_kernel(pe_ref, x_ref, dest_ref, xg_hbm, zero_s, sem):
    i = pl.program_id(0)

    @pl.when(i == 0)
    def _():
        zero_s[...] = jnp.zeros_like(zero_s)

        def fill(e):
            start = pl.multiple_of(pe_ref[e] - MOE_TILE, MOE_TILE)
            return pltpu.make_async_copy(zero_s, xg_hbm.at[pl.ds(start, MOE_TILE)], sem)

        def nonempty(e):
            return pe_ref[e] > (pe_ref[e - 1] if e else 0)

        n_blocks = xg_hbm.shape[0] // MOE_TILE
        first_tail = (n_blocks * MOE_TILE - N_EXPERTS * (MOE_TILE - 1)) // MOE_TILE

        def tail(b):
            return pltpu.make_async_copy(zero_s, xg_hbm.at[pl.ds(b * MOE_TILE, MOE_TILE)], sem)

        def unused(b):
            return b * MOE_TILE >= pe_ref[N_EXPERTS - 1]

        for e in range(N_EXPERTS):
            pl.when(nonempty(e))(lambda e=e: fill(e).start())
        for b in range(first_tail, n_blocks):
            pl.when(unused(b))(lambda b=b: tail(b).start())
        for e in range(N_EXPERTS):
            pl.when(nonempty(e))(lambda e=e: fill(e).wait())
        for b in range(first_tail, n_blocks):
            pl.when(unused(b))(lambda b=b: tail(b).wait())

    def row_copy(g, i, k):
        d = dest_ref[0, 0, (g * SUBLANES + i) * TOP_K + k]
        return pltpu.make_async_copy(x_ref.at[g, pl.ds(i, 1)], xg_hbm.at[pl.ds(d, 1)], sem)

    def for_rows(fn):
        def body(g, c):
            for i in range(SUBLANES):
                for k in range(TOP_K):
                    fn(row_copy(g, i, k))
            return c
        lax.fori_loop(0, x_ref.shape[0], body, 0)

    for_rows(lambda cp: cp.start())
    for_rows(lambda cp: cp.wait())


def _dispatch_call(pad_end, x1, dest3, n_pad):
    n = x1.shape[0]
    tm = ROW_TILE
    grid_spec = pltpu.PrefetchScalarGridSpec(
        num_scalar_prefetch=1, grid=(n // tm,),
        in_specs=[pl.BlockSpec((tm // SUBLANES, SUBLANES, D_MODEL), lambda i, pe: (i, 0, 0)),
                  pl.BlockSpec((1, 1, tm * TOP_K), lambda i, pe: (i, 0, 0), memory_space=pltpu.SMEM)],
        out_specs=pl.BlockSpec(memory_space=pl.ANY),
        scratch_shapes=[pltpu.VMEM((MOE_TILE, D_MODEL), F32), pltpu.SemaphoreType.DMA(())])
    return pl.pallas_call(
        _dispatch_kernel, grid_spec=grid_spec,
        out_shape=jax.ShapeDtypeStruct((n_pad, D_MODEL), F32),
        compiler_params=pltpu.CompilerParams(dimension_semantics=("arbitrary",), vmem_limit_bytes=VMEM_LIMIT),
        name="dispatch",
    )(pad_end, x1.reshape(n // SUBLANES, SUBLANES, D_MODEL), dest3)


def _moe_kernel(be_ref, nb_ref, xg_ref, wgu_ref, bgu_ref, wdn_ref, bdn_ref, y_ref, wgu_s, wdn_s):
    i = pl.program_id(0)

    @pl.when(i < nb_ref[0])
    def _():
        e = be_ref[i]
        prev = be_ref[jnp.maximum(i - 1, 0)]

        @pl.when((i == 0) | (e != prev))
        def _():
            wgu_s[...] = wgu_ref[0, 0].astype(BF16)
            wdn_s[...] = wdn_ref[0, 0].astype(BF16)

        h = jnp.dot(xg_ref[...].astype(BF16), wgu_s[...], preferred_element_type=F32) + bgu_ref[0, 0]
        gate = jnp.minimum(h[:, :D_FF], SWIGLU_LIMIT)
        lin = jnp.clip(h[:, D_FF:], -SWIGLU_LIMIT, SWIGLU_LIMIT)
        act = (lin + 1.0) * (gate * jax.nn.sigmoid(SWIGLU_ALPHA * gate))
        y_ref[...] = jnp.dot(act.astype(BF16), wdn_s[...], preferred_element_type=F32) + bdn_ref[0, 0]

    @pl.when(i >= nb_ref[0])
    def _():
        y_ref[...] = jnp.zeros_like(y_ref)


def _moe_call(block_e, n_used, xg, w_gu, b_gu, w_dn, b_dn, layer):
    n_pad = xg.shape[0]
    tm = MOE_TILE

    def used(i, be, nb):
        return (jnp.minimum(i, nb[0] - 1), 0)

    def expert(i, be, nb):
        return (layer, be[i], 0, 0)

    grid_spec = pltpu.PrefetchScalarGridSpec(
        num_scalar_prefetch=2, grid=(n_pad // tm,),
        in_specs=[pl.BlockSpec((tm, D_MODEL), used),
                  pl.BlockSpec((1, 1, D_MODEL, 2 * D_FF), expert), pl.BlockSpec((1, 1, 1, 2 * D_FF), expert),
                  pl.BlockSpec((1, 1, D_FF, D_MODEL), expert), pl.BlockSpec((1, 1, 1, D_MODEL), expert)],
        out_specs=pl.BlockSpec((tm, D_MODEL), lambda i, be, nb: (i, 0)),
        scratch_shapes=[pltpu.VMEM((D_MODEL, 2 * D_FF), BF16), pltpu.VMEM((D_FF, D_MODEL), BF16)])
    return pl.pallas_call(
        _moe_kernel, grid_spec=grid_spec,
        out_shape=jax.ShapeDtypeStruct((n_pad, D_MODEL), F32),
        compiler_params=pltpu.CompilerParams(dimension_semantics=("arbitrary",),
                                             vmem_limit_bytes=56 * 1024 * 1024),
        name="moe",
    )(block_e, n_used, xg, w_gu, b_gu.reshape(DEPTH, N_EXPERTS, 1, 2 * D_FF), w_dn,
      b_dn.reshape(DEPTH, N_EXPERTS, 1, D_MODEL))


def _combine_kernel(x1_ref, gate_ref, dcur_ref, dnext_ref, g_ref, b_ref, y_hbm, op_ref, os_ref, ybuf, sem):
    i = pl.program_id(0)
    tm = x1_ref.shape[0]
    slot = i % 2

    def row_copy(d_ref, s, g, i, k):
        d = d_ref[0, 0, (g * SUBLANES + i) * TOP_K + k]
        return pltpu.make_async_copy(y_hbm.at[pl.ds(d, 1)], ybuf.at[s, k, g, pl.ds(i, 1)], sem.at[s])

    def for_rows(d_ref, s, fn):
        def body(g, c):
            for i in range(SUBLANES):
                for k in range(TOP_K):
                    fn(row_copy(d_ref, s, g, i, k))
            return c
        lax.fori_loop(0, tm // SUBLANES, body, 0)

    @pl.when(i == 0)
    def _():
        for_rows(dcur_ref, 0, lambda cp: cp.start())

    @pl.when(i + 1 < pl.num_programs(0))
    def _():
        for_rows(dnext_ref, 1 - slot, lambda cp: cp.start())

    for_rows(dcur_ref, slot, lambda cp: cp.wait())
    gates = gate_ref[...]
    moe = gates[:, 0:1] * ybuf[slot, 0].reshape(tm, D_MODEL)
    for k in range(1, TOP_K):
        moe = moe + gates[:, k:k + 1] * ybuf[slot, k].reshape(tm, D_MODEL)
    res = _layer_norm(ALPHA * x1_ref[...] + moe, g_ref[...], b_ref[...])
    last = i + 1 == pl.num_programs(0)

    @pl.when(jnp.logical_not(last))
    def _():
        op_ref[...] = res

    @pl.when(last)
    def _():
        os_ref[...] = res


def _combine_call(x1, gates, dest3, g, b, y_rows):
    n = x1.shape[0]
    tm = ROW_TILE
    n_tiles = n // tm
    row = lambda i: (i, 0)
    const2 = lambda i: (0, 0)
    return pl.pallas_call(
        _combine_kernel, grid=(n_tiles,),
        in_specs=[pl.BlockSpec((tm, D_MODEL), row), pl.BlockSpec((tm, TOP_K), row),
                  pl.BlockSpec((1, 1, tm * TOP_K), lambda i: (i, 0, 0), memory_space=pltpu.SMEM),
                  pl.BlockSpec((1, 1, tm * TOP_K), lambda i: (jnp.minimum(i + 1, n_tiles - 1), 0, 0),
                               memory_space=pltpu.SMEM),
                  pl.BlockSpec((1, D_MODEL), const2), pl.BlockSpec((1, D_MODEL), const2),
                  pl.BlockSpec(memory_space=pl.ANY)],
        out_specs=[pl.BlockSpec((tm, D_MODEL), lambda i: (jnp.minimum(i, n_tiles - 2), 0)),
                   pl.BlockSpec((tm, D_MODEL), const2)],
        out_shape=[jax.ShapeDtypeStruct((n - tm, D_MODEL), F32), jax.ShapeDtypeStruct((tm, D_MODEL), F32)],
        scratch_shapes=[pltpu.VMEM((2, TOP_K, tm // SUBLANES, SUBLANES, D_MODEL), F32),
                        pltpu.SemaphoreType.DMA((2,))],
        compiler_params=pltpu.CompilerParams(dimension_semantics=("arbitrary",), vmem_limit_bytes=VMEM_LIMIT),
        name="combine",
    )(x1, gates, dest3, dest3, g, b, y_rows)


def _pack_w_in(w):
    sizes = (D_A, D_A, D_A, H_I * D_I, D_I, H_I, 2 * W_B, D_CQ, D_C, DR_C)
    offs = np.cumsum((0,) + sizes)
    seg = [w[:, offs[k]:offs[k + 1]] for k in range(len(sizes))]
    z = lambda c: jnp.zeros((w.shape[0], c), w.dtype)
    return jnp.concatenate(
        [seg[0], seg[1], seg[2], seg[3], seg[4], seg[5], z(LANES - D_I - H_I), seg[6], seg[7], seg[8],
         seg[9], z(LANES - DR_C)], axis=1).astype(BF16)


def _pack_w_uq(w):
    w3 = w.reshape(D_CQ, H_C, DN_C + DR_C)
    w3 = jnp.pad(w3, ((0, 0), (0, 0), (0, QC_SLOT - DN_C - DR_C)))
    return w3.reshape(D_CQ, D_QC).astype(BF16)


def _pack_w_kc(w_uk):
    wk = jnp.pad(w_uk.reshape(D_C, H_C, DN_C), ((0, 0), (0, 0), (0, QC_SLOT - DN_C))).reshape(D_C, D_QC)
    eye = jnp.pad(jnp.eye(DR_C, dtype=w_uk.dtype), ((0, LANES - DR_C), (DN_C, QC_SLOT - DN_C - DR_C)))
    return jnp.concatenate([wk, jnp.tile(eye, (1, H_C))], axis=0).astype(BF16)


def _rope_tables(pos):
    posf = pos.astype(F32)[:, None]
    n = pos.shape[0]

    def cs(half):
        inv = ROPE_THETA ** (-jnp.arange(half, dtype=F32) / half)
        ang = posf * inv[None, :]
        return jnp.cos(ang), jnp.sin(ang)

    c32, s32 = cs(32)
    z32 = jnp.zeros_like(s32)
    c64 = jnp.tile(jnp.concatenate([c32, c32], 1), (1, 2))
    sa64 = jnp.tile(jnp.concatenate([-s32, z32], 1), (1, 2))
    sb64 = jnp.tile(jnp.concatenate([z32, s32], 1), (1, 2))
    c16, s16 = cs(16)
    z16 = jnp.zeros_like(s16)
    one = lambda c: jnp.ones((n, c), F32)
    zero = lambda c: jnp.zeros((n, c), F32)
    cq = jnp.concatenate([one(DN_C), c16, c16, one(32)], 1)
    saq = jnp.concatenate([zero(DN_C), -s16, z16, zero(32)], 1)
    sbq = jnp.concatenate([zero(DN_C), z16, s16, zero(32)], 1)
    ck = jnp.concatenate([c16, c16, one(96)], 1)
    sak = jnp.concatenate([-s16, z16, zero(96)], 1)
    sbk = jnp.concatenate([z16, s16, zero(96)], 1)
    return [c64, sa64, sb64, cq, saq, sbq, ck, sak, sbk]


def _sgu_tables(w_s, b_s, n_sample_seq, n_sample_batch):
    i = jnp.arange(SGU_CHUNK)
    w_m = jnp.where((i[None, :] // CHUNK) <= (i[:, None] // CHUNK), w_s, 0.0)
    reps = ROW_TILE // SGU_CHUNK
    bd_p = jax.vmap(lambda m: jnp.kron(jnp.eye(reps, dtype=m.dtype), m))(w_m)
    ns = n_sample_seq
    bd_s = jax.vmap(lambda m: jnp.kron(jnp.eye(n_sample_batch, dtype=m.dtype), m[:ns, :ns]))(w_m)
    bd = jnp.stack([bd_p, bd_s]).astype(BF16)
    bias_chunk = jnp.repeat(b_s.T, DG_B, axis=1)
    bias = jnp.stack([jnp.tile(bias_chunk, (reps, 1)), jnp.tile(bias_chunk[:ns], (n_sample_batch, 1))])
    return bd, bias


def _route(top_idx, rank, counts):
    n_assign = top_idx.size
    counts = counts.reshape(N_EXPERTS).astype(I32)
    padded = ((counts + MOE_TILE - 1) // MOE_TILE) * MOE_TILE
    pad_end = jnp.cumsum(padded).astype(I32)
    pad_start = pad_end - padded
    dest = jnp.take(pad_start, top_idx) + rank
    n_blocks = -(-(n_assign + N_EXPERTS * (MOE_TILE - 1)) // MOE_TILE)
    n_used = pad_end[-1] // MOE_TILE
    blk = jnp.minimum(jnp.arange(n_blocks, dtype=I32), n_used - 1) * MOE_TILE
    block_e = jnp.minimum(jnp.sum(pad_end[None, :] <= blk[:, None], axis=1), N_EXPERTS - 1).astype(I32)
    return dest, pad_end, block_e, n_used.reshape(1), n_blocks * MOE_TILE


def _key_tiles_t(v, n_batch, k_rows, tk):
    c = v.shape[1]
    return v.reshape(n_batch, k_rows // tk, tk, c).transpose(0, 1, 3, 2).reshape(n_batch * (k_rows // tk), c, tk)


def kernel(x_prompt, x_sample, cache_a_k, cache_a_v, cache_a_kidx, cache_c_latent, cache_c_krope, w_in, mla_q_norm, mla_kv_norm, w_uq, w_uk, w_uv, sgu_ln_g, sgu_ln_b, w_spatial, b_spatial, w_out, ln1_g, ln1_b, router_w, router_b, w_gate_up, b_gate_up, w_down, b_down, ln2_g, ln2_b):
    batch, seq, _ = x_prompt.shape
    dec_batch, dec_seq, _ = x_sample.shape
    past = cache_a_k.shape[2]
    n_p, n_s = batch * seq, dec_batch * dec_seq
    assert n_p % ROW_TILE == 0 and n_s == ROW_TILE and seq % ROW_TILE == 0
    assert seq % ATT_TK == 0 and seq % ATT_TQ == 0 and dec_seq <= SAMPLE_TQ
    n_all = n_p + n_s
    tab_period = seq // ROW_TILE

    pos_p = jnp.arange(seq, dtype=I32)
    pos_s = past + jnp.arange(dec_seq, dtype=I32)
    tabs = _rope_tables(jnp.concatenate([pos_p, jnp.tile(pos_s, dec_batch)]))

    n_keys_s = past + dec_seq
    l_pad_s = -(-n_keys_s // SAMPLE_TK) * SAMPLE_TK
    top_k_p = min(TOPK_MAX, seq // 4)
    top_k_s = min(TOPK_MAX, n_keys_s // 4)

    xp, xs = x_prompt.reshape(n_p, D_MODEL), x_sample.reshape(n_s, D_MODEL)
    caches_p = [[] for _ in range(5)]
    caches_s = [[] for _ in range(6)]

    def with_cache(cache, new, width):
        allk = jnp.concatenate([cache.astype(BF16), new.reshape(dec_batch, dec_seq, width)], axis=1)
        allk = jnp.pad(allk, ((0, 0), (0, l_pad_s - n_keys_s), (0, 0)))
        return allk.reshape(dec_batch * l_pad_s, width)

    def sample_queries(arr):
        q = arr[n_p:].reshape(dec_batch, dec_seq, -1)
        return jnp.pad(q, ((0, 0), (0, SAMPLE_TQ - dec_seq), (0, 0))).reshape(dec_batch * SAMPLE_TQ, -1)

    def sample_rows(o):
        return o.reshape(dec_batch, SAMPLE_TQ, -1)[:, :dec_seq].reshape(n_s, -1)

    for l in range(DEPTH):
        w_packed = _pack_w_in(w_in[l])
        wuq = _pack_w_uq(w_uq[l])
        wkc = _pack_w_kc(w_uk[l])
        wuv = w_uv[l].astype(BF16)
        bd, bsb = _sgu_tables(w_spatial[l], b_spatial[l], dec_seq, dec_batch)
        (qa, kab, qi, kib, wi, ob, qc, kc,
         ka_p, va_p, ki_p, lat_p, kr_p, vat_p, vct_p,
         ka_s1, va_s1, ki_s1, lat_s1, kr_s1, vb_s1, vab_s, vc_new) = _proj_call(
            xp, xs, w_packed, tabs, mla_q_norm[l].reshape(1, -1), mla_kv_norm[l].reshape(1, -1), wuq, wkc, wuv,
            sgu_ln_g[l].reshape(1, -1), sgu_ln_b[l].reshape(1, -1), bd, bsb, tab_period)

        oa_p = _dsa_call(qi, wi, qa, kib, kab, vat_p, n_batch=batch, n_q=seq, q_row0=0, k_rows=seq,
                         tq=ATT_TQ, tk=ATT_TK, n_valid=seq, q_pos0=0, top_k=top_k_p)
        oc_p = _mla_call(qc, kc, vct_p, n_batch=batch, n_q=seq, q_row0=0, k_rows=seq,
                         tq=MLA_TQ, tk=ATT_TK, n_valid=seq, q_pos0=0)

        kc_c, vc_c = _kvup_call(cache_c_latent[l].reshape(dec_batch * past, D_C),
                                cache_c_krope[l].reshape(dec_batch * past, DR_C), wkc, wuv)
        ki_s = with_cache(cache_a_kidx[l], kib[n_p:], D_I)
        ka_s = with_cache(cache_a_k[l].reshape(dec_batch, past, D_A), kab[n_p:], D_A)
        va_s = with_cache(cache_a_v[l].reshape(dec_batch, past, D_A), vab_s, D_A)
        kc_s = with_cache(kc_c.reshape(dec_batch, past, D_QC), kc[n_p:], D_QC)
        vc_s = with_cache(vc_c.reshape(dec_batch, past, D_VC), vc_new, D_VC)
        oa_s = _dsa_call(sample_queries(qi), sample_queries(wi), sample_queries(qa), ki_s, ka_s,
                         _key_tiles_t(va_s, dec_batch, l_pad_s, SAMPLE_TK),
                         n_batch=dec_batch, n_q=SAMPLE_TQ, q_row0=0, k_rows=l_pad_s,
                         tq=SAMPLE_TQ, tk=SAMPLE_TK, n_valid=n_keys_s, q_pos0=past, top_k=top_k_s)
        oc_s = _mla_call(sample_queries(qc), kc_s, _key_tiles_t(vc_s, dec_batch, l_pad_s, SAMPLE_TK),
                         n_batch=dec_batch, n_q=SAMPLE_TQ, q_row0=0, k_rows=l_pad_s,
                         tq=SAMPLE_TQ, tk=SAMPLE_TK, n_valid=n_keys_s, q_pos0=past)

        x1, top_idx_t, gates_t, rank_t, counts = _outproj_call(
            oa_p, sample_rows(oa_s), ob, oc_p, sample_rows(oc_s), xp, xs, w_out[l].astype(BF16),
            ln1_g[l].reshape(1, -1), ln1_b[l].reshape(1, -1), router_w[l].T.astype(BF16), router_b[l].reshape(-1, 1))
        gates = gates_t.T

        dest, pad_end, block_e, n_used, n_pad = _route(top_idx_t.T, rank_t.T, counts)
        dest3 = dest.reshape(n_all // ROW_TILE, 1, ROW_TILE * TOP_K)
        xg = _dispatch_call(pad_end, x1, dest3, n_pad)
        y_rows = _moe_call(block_e, n_used, xg, w_gate_up, b_gate_up, w_down, b_down, l)
        xp, xs = _combine_call(x1, gates, dest3, ln2_g[l].reshape(1, -1), ln2_b[l].reshape(1, -1), y_rows)

        for dst, arr, shp in ((caches_p[0], ka_p, (H_A, DH_A)), (caches_p[1], va_p, (H_A, DH_A)),
                              (caches_p[2], ki_p, (D_I,)), (caches_p[3], lat_p, (D_C,)), (caches_p[4], kr_p, (DR_C,))):
            dst.append(arr.reshape((batch, seq) + shp))
        for dst, arr, shp in ((caches_s[0], ka_s1, (H_A, DH_A)), (caches_s[1], va_s1, (H_A, DH_A)),
                              (caches_s[2], ki_s1, (D_I,)), (caches_s[3], lat_s1, (D_C,)),
                              (caches_s[4], kr_s1, (DR_C,)), (caches_s[5], vb_s1, (W_B,))):
            dst.append(arr.reshape((dec_batch, dec_seq) + shp))

    y_prompt = xp.reshape(batch, seq, D_MODEL)
    y_sample = xs.reshape(dec_batch, dec_seq, D_MODEL)
    return (y_prompt, y_sample) + tuple(jnp.stack(c) for c in caches_p) + tuple(jnp.stack(c) for c in caches_s)
```

```python
import functools

import numpy as np
import jax
import jax.numpy as jnp
from jax import lax
from jax.experimental import pallas as pl
from jax.experimental.pallas import tpu as pltpu

F32 = jnp.float32
BF16 = jnp.bfloat16
I32 = jnp.int32
I16 = jnp.int16

D_MODEL = 1024
DEPTH = 2
CHUNK = 64
ROPE_THETA = 10000.0
LN_EPS = 1e-5
RMS_EPS = 1e-6
H_A, DH_A = 6, 64
H_I, D_I = 8, 64
TOPK_MAX = 256
G_B, DG_B = 4, 64
W_B = G_B * DG_B
SGU_CHUNK = 128
H_C, DN_C, DR_C, DV_C = 6, 64, 32, 64
D_CQ, D_C = 256, 256
MLA_SCALE = (DN_C + DR_C) ** -0.5
LOG2E = float(np.log2(np.e))
D_A = H_A * DH_A
D_VC = H_C * DV_C
N_EXPERTS = 32
TOP_K = 4
D_FF = 1024
SWIGLU_LIMIT = 7.0
SWIGLU_ALPHA = 1.702
ALPHA = (2 * DEPTH) ** 0.25

LANES = 128
SUBLANES = 8
PACKED_ROWS = 16
HALF_RANGE = 1 << 15
COUNT_CHAINS = 4
LO_STEPS = 4
ROW_TILE = 256
MOE_TILE = 256
ATT_TQ = 512
ATT_TK = 512
MLA_TQ = 512
SAMPLE_TQ = 128
SAMPLE_TK = 384
VMEM_LIMIT = 48 * 1024 * 1024

C_QA, C_KA, C_VA, C_QI, C_KIW, C_ZU, C_ZV, C_CQ, C_CKV, C_KR = (
    0, 384, 768, 1152, 1664, 1792, 2048, 2304, 2560, 2816)
IN_PACKED = 2944
QC_SLOT = 128
D_QC = H_C * QC_SLOT

NEG_BIG = -1e30
KEY_NEG_INF = -2139095041
INT_MIN = -2147483648


def _nt_dot(a, b):
    return lax.dot_general(a, b, (((1,), (1,)), ((), ())), preferred_element_type=F32)


def _rope128(x, c, sa, sb, half):
    return x * c + pltpu.roll(x, LANES - half, 1) * sa + pltpu.roll(x, half, 1) * sb


def _proj_kernel(xp_ref, xs_ref, w_ref, c64_ref, sa64_ref, sb64_ref, cq_ref, saq_ref, sbq_ref,
                 ck_ref, sak_ref, sbk_ref, qn_ref, kvn_ref, wuq_ref, wkc_ref, wuv_ref,
                 sgug_ref, sgub_ref, bd_ref, bsb_ref,
                 qa_o, kab_o, qi_o, kib_o, wi_o, ob_o, qc_o, kc_o,
                 kap_o, vap_o, kip_o, latp_o, krp_o, vat_o, vct_o,
                 kas_o, vas_o, kis_o, lats_o, krs_o, vbs_o, vabs_o, vcs_o, *, n_prompt_tiles):
    is_prompt = pl.program_id(0) < n_prompt_tiles
    x = jnp.where(is_prompt, xp_ref[...], xs_ref[...])
    h = jnp.dot(x.astype(BF16), w_ref[...], preferred_element_type=F32)
    c64, sa64, sb64 = c64_ref[...], sa64_ref[...], sb64_ref[...]

    def rope64(col):
        return _rope128(h[:, col:col + LANES], c64, sa64, sb64, 32)

    kas = []
    for c in range(D_A // LANES):
        qa_o[:, c * LANES:(c + 1) * LANES] = (rope64(C_QA + c * LANES) * (DH_A ** -0.5 * LOG2E)).astype(BF16)
        kas.append(rope64(C_KA + c * LANES))
        kab_o[:, c * LANES:(c + 1) * LANES] = kas[c].astype(BF16)
    va = h[:, C_VA:C_VA + D_A]
    for c in range(H_I * D_I // LANES):
        qi_o[:, c * LANES:(c + 1) * LANES] = (rope64(C_QI + c * LANES) * (D_I ** -0.5)).astype(BF16)
    ki = rope64(C_KIW)[:, :D_I]
    kib_o[...] = ki.astype(BF16)
    wi_o[...] = h[:, C_KIW:C_KIW + LANES] * (H_I ** -0.5)

    z = h[:, C_ZU:C_ZU + 2 * W_B]
    z = 0.5 * z * (1.0 + jnp.tanh(np.sqrt(2.0 / np.pi) * (z + 0.044715 * (z * z * z))))
    u = z[:, :W_B]
    v = z[:, W_B:]
    mu = jnp.mean(v, axis=-1, keepdims=True)
    var = jnp.mean(jnp.square(v - mu), axis=-1, keepdims=True)
    v = (v - mu) * lax.rsqrt(var + LN_EPS) * sgug_ref[...] + sgub_ref[...]
    v16 = v.astype(BF16)
    lane = lax.broadcasted_iota(I32, (v.shape[0], LANES), 1)
    for w in range(W_B // LANES):
        vw = v16[:, w * LANES:(w + 1) * LANES]
        m0 = jnp.dot(bd_ref[0, 2 * w], vw, preferred_element_type=F32)
        m1 = jnp.dot(bd_ref[0, 2 * w + 1], vw, preferred_element_type=F32)
        mixed = jnp.where(lane < DG_B, m0, m1) + bsb_ref[0, :, w * LANES:(w + 1) * LANES]
        ob_o[:, w * LANES:(w + 1) * LANES] = (u[:, w * LANES:(w + 1) * LANES] * mixed).astype(BF16)

    cq = h[:, C_CQ:C_CQ + D_CQ]
    cq = cq * lax.rsqrt(jnp.mean(jnp.square(cq), axis=-1, keepdims=True) + RMS_EPS) * qn_ref[...]
    q = jnp.dot(cq.astype(BF16), wuq_ref[...], preferred_element_type=F32)
    cqt, saq, sbq = cq_ref[...], saq_ref[...], sbq_ref[...]
    for hh in range(H_C):
        qs = _rope128(q[:, hh * QC_SLOT:(hh + 1) * QC_SLOT], cqt, saq, sbq, DR_C // 2)
        qc_o[:, hh * QC_SLOT:(hh + 1) * QC_SLOT] = (qs * (MLA_SCALE * LOG2E)).astype(BF16)
    ckv = h[:, C_CKV:C_CKV + D_C]
    lat = ckv * lax.rsqrt(jnp.mean(jnp.square(ckv), axis=-1, keepdims=True) + RMS_EPS) * kvn_ref[...]
    kr = _rope128(h[:, C_KR:C_KR + LANES], ck_ref[...], sak_ref[...], sbk_ref[...], DR_C // 2)
    lat16 = lat.astype(BF16)
    kc = (jnp.dot(lat16, wkc_ref[:D_C, :], preferred_element_type=F32)
          + jnp.dot(kr.astype(BF16), wkc_ref[D_C:, :], preferred_element_type=F32))
    kc_o[...] = kc.astype(BF16)
    vc = jnp.dot(lat16, wuv_ref[...], preferred_element_type=F32)

    @pl.when(is_prompt)
    def _():
        for c in range(D_A // LANES):
            kap_o[:, c * LANES:(c + 1) * LANES] = kas[c]
            vat_o[0, c * LANES:(c + 1) * LANES, :] = va[:, c * LANES:(c + 1) * LANES].T.astype(BF16)
            vct_o[0, c * LANES:(c + 1) * LANES, :] = vc[:, c * LANES:(c + 1) * LANES].T.astype(BF16)
        vap_o[...] = va
        kip_o[...] = ki
        latp_o[...] = lat
        krp_o[...] = kr[:, :DR_C]

    @pl.when(jnp.logical_not(is_prompt))
    def _():
        for c in range(D_A // LANES):
            kas_o[:, c * LANES:(c + 1) * LANES] = kas[c]
        vas_o[...] = va
        kis_o[...] = ki
        lats_o[...] = lat
        krs_o[...] = kr[:, :DR_C]
        vbs_o[...] = v
        vabs_o[...] = va.astype(BF16)
        vcs_o[...] = vc.astype(BF16)


def _proj_call(xp, xs, w_packed, tabs, qn, kvn, wuq, wkc, wuv, sgug, sgub, bd, bsb, tab_period):
    n_p = xp.shape[0]
    tm = ROW_TILE
    n_prompt_tiles = n_p // tm
    n = n_p + tm
    grid = (n_prompt_tiles + 1,)
    per_kt = ATT_TK // tm

    def row(i):
        return (i, 0)

    def prow(i):
        return (jnp.minimum(i, n_prompt_tiles - 1), 0)

    def ptile(i):
        ip = jnp.minimum(i, n_prompt_tiles - 1)
        return (ip // per_kt, 0, ip % per_kt)

    def const2(i):
        return (0, 0)

    def tab(i):
        return (jnp.where(i < n_prompt_tiles, i % tab_period, tab_period), 0)

    def grp(i):
        return (jnp.where(i < n_prompt_tiles, 0, 1), 0, 0, 0)

    def grp3(i):
        return (jnp.where(i < n_prompt_tiles, 0, 1), 0, 0)

    in_specs = [pl.BlockSpec((tm, D_MODEL), prow), pl.BlockSpec((tm, D_MODEL), const2),
                pl.BlockSpec((D_MODEL, IN_PACKED), const2)]
    in_specs += [pl.BlockSpec((tm, LANES), tab)] * 9
    in_specs += [pl.BlockSpec((1, D_CQ), const2), pl.BlockSpec((1, D_C), const2),
                 pl.BlockSpec((D_CQ, D_QC), const2), pl.BlockSpec((D_C + LANES, D_QC), const2),
                 pl.BlockSpec((D_C, D_VC), const2),
                 pl.BlockSpec((1, W_B), const2), pl.BlockSpec((1, W_B), const2),
                 pl.BlockSpec((1, G_B, tm, tm), grp), pl.BlockSpec((1, tm, W_B), grp3)]
    all_rows = [(D_A, BF16), (D_A, BF16), (H_I * D_I, BF16), (D_I, BF16), (LANES, F32), (W_B, BF16),
                (D_QC, BF16), (D_QC, BF16)]
    caches = [(D_A, F32), (D_A, F32), (D_I, F32), (D_C, F32), (DR_C, F32)]
    sample_only = caches + [(W_B, F32), (D_A, BF16), (D_VC, BF16)]
    out_shape = ([jax.ShapeDtypeStruct((n, c), d) for c, d in all_rows]
                 + [jax.ShapeDtypeStruct((n_p, c), d) for c, d in caches]
                 + [jax.ShapeDtypeStruct((n_p // ATT_TK, c, ATT_TK), BF16) for c in (D_A, D_VC)]
                 + [jax.ShapeDtypeStruct((tm, c), d) for c, d in sample_only])
    out_specs = ([pl.BlockSpec((tm, c), row) for c, _ in all_rows]
                 + [pl.BlockSpec((tm, c), prow) for c, _ in caches]
                 + [pl.BlockSpec((1, c, tm), ptile) for c in (D_A, D_VC)]
                 + [pl.BlockSpec((tm, c), const2) for c, _ in sample_only])
    return pl.pallas_call(
        functools.partial(_proj_kernel, n_prompt_tiles=n_prompt_tiles),
        grid=grid, in_specs=in_specs, out_specs=out_specs, out_shape=out_shape,
        compiler_params=pltpu.CompilerParams(dimension_semantics=("arbitrary",), vmem_limit_bytes=VMEM_LIMIT),
        name="proj",
    )(xp, xs, w_packed, *tabs, qn, kvn, wuq, wkc, wuv, sgug, sgub, bd, bsb)


def _kvup_kernel(lat_ref, kr_ref, wkc_ref, wuv_ref, kc_o, vc_o):
    lat16 = lat_ref[...].astype(BF16)
    kc = (jnp.dot(lat16, wkc_ref[:D_C, :], preferred_element_type=F32)
          + jnp.dot(kr_ref[...].astype(BF16), wkc_ref[D_C:D_C + DR_C, :], preferred_element_type=F32))
    kc_o[...] = kc.astype(BF16)
    vc_o[...] = jnp.dot(lat16, wuv_ref[...], preferred_element_type=F32).astype(BF16)


def _kvup_call(lat, kr, wkc, wuv):
    n = lat.shape[0]
    tm = 1024
    return pl.pallas_call(
        _kvup_kernel, grid=(n // tm,),
        in_specs=[pl.BlockSpec((tm, D_C), lambda i: (i, 0)), pl.BlockSpec((tm, DR_C), lambda i: (i, 0)),
                  pl.BlockSpec((D_C + LANES, D_QC), lambda i: (0, 0)), pl.BlockSpec((D_C, D_VC), lambda i: (0, 0))],
        out_specs=[pl.BlockSpec((tm, D_QC), lambda i: (i, 0)), pl.BlockSpec((tm, D_VC), lambda i: (i, 0))],
        out_shape=[jax.ShapeDtypeStruct((n, D_QC), BF16), jax.ShapeDtypeStruct((n, D_VC), BF16)],
        compiler_params=pltpu.CompilerParams(dimension_semantics=("arbitrary",), vmem_limit_bytes=VMEM_LIMIT),
        name="kvup",
    )(lat, kr, wkc, wuv)


def _key_bounds(j, tq, tk, n_valid, q_pos0):
    q_first = q_pos0 + j * tq
    kmax = jnp.minimum(((q_first + tq - 1) // CHUNK + 1) * CHUNK, n_valid)
    n_tiles = (kmax + tk - 1) // tk
    n_full = jnp.minimum((q_first // CHUNK + 1) * CHUNK, n_valid) // tk
    qpos = q_first + lax.broadcasted_iota(I32, (1, tq), 1)
    bound = jnp.minimum((qpos // CHUNK + 1) * CHUNK, n_valid)
    return n_tiles, n_full, bound


def _flash_init(m_s, l_s, acc_s):
    m_s[...] = jnp.full(m_s.shape, NEG_BIG, F32)
    l_s[...] = jnp.zeros(l_s.shape, F32)
    acc_s[...] = jnp.zeros(acc_s.shape, F32)


def _flash_update(hh, s_t, v_t, m_s, l_s):
    m_old = m_s[hh]
    m_new = jnp.maximum(m_old, jnp.max(s_t, axis=0, keepdims=True))
    p = jnp.exp2(s_t - m_new)
    a = jnp.exp2(m_old - m_new)
    l_s[hh] = a * l_s[hh] + jnp.sum(p, axis=0, keepdims=True)
    m_s[hh] = m_new
    return a, jnp.dot(v_t, p.astype(BF16), preferred_element_type=F32)


def _flash_pair_accumulate(w, first_rows, upd0, upd1, acc_s):
    (a0, pv0), (a1, pv1) = upd0, upd1
    acc_s[w] = acc_s[w] * jnp.where(first_rows, a0, a1) + jnp.where(first_rows, pv0, pv1)


def _flash_heads(scores, values, n_heads, first_rows, m_s, l_s, acc_s):
    s_all = [scores(hh) for hh in range(n_heads)]
    upd = []
    for hh in range(n_heads):
        upd.append(_flash_update(hh, s_all[hh], values(hh // 2), m_s, l_s))
        if hh % 2:
            _flash_pair_accumulate(hh // 2, first_rows, upd[hh - 1], upd[hh], acc_s)


def _flash_finish(o_ref, first_rows, l_s, acc_s):
    for w in range(acc_s.shape[0]):
        l_sel = jnp.where(first_rows, l_s[2 * w], l_s[2 * w + 1])
        o_ref[:, w * LANES:(w + 1) * LANES] = (acc_s[w] / l_sel).T.astype(o_ref.dtype)


def _flash_scratch(n_heads, tq):
    return [pltpu.VMEM((n_heads, 1, tq), F32), pltpu.VMEM((n_heads, 1, tq), F32),
            pltpu.VMEM((n_heads // 2, LANES, tq), F32)]


def _dsa_kernel(qi_ref, wi_ref, qa_ref, ki_ref, ka_ref, vat_ref, o_ref, key_ref, k16_ref, m_s, l_s, acc_s, *,
                tq, tk, n_valid, q_pos0, top_k):
    j = pl.program_id(1)
    n_tiles, _, bound = _key_bounds(j, tq, tk, n_valid, q_pos0)
    row_tk = lax.broadcasted_iota(I32, (tk, tq), 0)
    row8 = lax.broadcasted_iota(I32, (SUBLANES, tq), 0)

    qi = qi_ref[...]
    q_stack = jnp.concatenate([qi[:, hh * D_I:(hh + 1) * D_I] for hh in range(H_I)], axis=0)
    wi_t = wi_ref[...].T
    wis = [wi_t[D_I + hh:D_I + hh + 1, :] for hh in range(H_I)]

    def score_tile(kt, carry):
        k0 = pl.multiple_of(kt * tk, tk)
        d = _nt_dot(ki_ref[pl.ds(k0, tk), :], q_stack)
        s = wis[0] * jnp.maximum(d[:, 0:tq], 0.0)
        for hh in range(1, H_I):
            s = s + wis[hh] * jnp.maximum(d[:, hh * tq:(hh + 1) * tq], 0.0)
        bits = pltpu.bitcast(s, I32)
        key = jnp.where(bits < 0, bits ^ 0x7FFFFFFF, bits)
        key = jnp.where(key == -1, 0, key)
        key = jnp.where(k0 + row_tk < bound, key, KEY_NEG_INF)
        key_ref[kt] = key
        k16_ref[kt] = (key >> 16).astype(I16)
        return carry

    lax.fori_loop(0, n_tiles, score_tile, 0)

    def count(pred):
        def body(kt, acc):
            for r in range(tk // SUBLANES):
                kk = key_ref[kt, r * SUBLANES:(r + 1) * SUBLANES, :]
                acc = acc + jnp.where(pred(kk, kt * tk + r * SUBLANES), 1.0, 0.0)
            return acc
        acc = lax.fori_loop(0, n_tiles, body, jnp.zeros((SUBLANES, tq), F32))
        return jnp.sum(acc, axis=0, keepdims=True)

    def count16_ge(cand):
        c16 = cand.astype(I16)

        def body(kt, accs):
            accs = list(accs)
            for r in range(tk // PACKED_ROWS):
                kk = k16_ref[kt, r * PACKED_ROWS:(r + 1) * PACKED_ROWS, :]
                accs[r % COUNT_CHAINS] = accs[r % COUNT_CHAINS] + jnp.where(kk >= c16, jnp.int16(1), jnp.int16(0))
            return tuple(accs)
        zero = jnp.zeros((PACKED_ROWS, tq), I16)
        accs = lax.fori_loop(0, n_tiles, body, (zero,) * COUNT_CHAINS)
        acc = accs[0].astype(F32)
        for a in accs[1:]:
            acc = acc + a.astype(F32)
        return jnp.sum(acc, axis=0, keepdims=True)

    kf = float(top_k)
    c0 = count16_ge(jnp.zeros((1, tq), I32))
    nonneg = c0 >= kf
    t_hi0 = jnp.where(nonneg, 0, -HALF_RANGE).astype(I32)
    cnt0 = jnp.where(nonneg, c0, (n_tiles * tk).astype(F32))

    def hi_body(it, carry):
        t_hi, cnt_t = carry
        cand = t_hi | lax.shift_left(jnp.int32(1), 14 - it)
        cnt = count16_ge(cand)
        ok = cnt >= kf
        return jnp.where(ok, cand, t_hi), jnp.where(ok, cnt, cnt_t)

    t_hi, cnt_hi = lax.fori_loop(0, 15, hi_body, (t_hi0, cnt0))
    need_lo = kf - count16_ge(t_hi + 1)
    cnt_class = cnt_hi - (kf - need_lo)

    def lo_tile(kt, carry):
        kk = key_ref[kt]
        lo = (kk & 0xFFFF) - HALF_RANGE
        k16_ref[kt] = jnp.where((kk >> 16) == t_hi, lo, -HALF_RANGE).astype(I16)
        return carry

    lax.fori_loop(0, n_tiles, lo_tile, 0)

    def lo_cond(carry):
        it, _, cnt_t = carry
        return (it < 16) & (jnp.max(jnp.abs(cnt_t - need_lo)) > 0.5)

    def lo_body(carry):
        it, t_lo, cnt_t = carry
        for step in range(LO_STEPS):
            cand = t_lo | lax.shift_left(jnp.int32(1), 15 - (it + step))
            cnt = count16_ge(cand - HALF_RANGE)
            ok = cnt >= need_lo
            t_lo, cnt_t = jnp.where(ok, cand, t_lo), jnp.where(ok, cnt, cnt_t)
        return it + LO_STEPS, t_lo, cnt_t

    _, t_lo, cnt_lo = lax.while_loop(lo_cond, lo_body, (jnp.int32(0), jnp.zeros((1, tq), I32), cnt_class))
    t = lax.shift_left(t_hi, 16) | t_lo
    inexact = jnp.max(jnp.abs(cnt_lo - need_lo)) > 0.5

    @pl.when(inexact)
    def _():
        c_gt = count(lambda kk, base: kk > t)
        c_eq = count(lambda kk, base: kk == t)
        need = kf - c_gt

        @pl.when(jnp.max(jnp.where((c_eq > need) & (t > KEY_NEG_INF), 1.0, 0.0)) > 0.5)
        def _():
            def jb(it, jj):
                cand = jj | lax.shift_left(jnp.int32(1), 12 - it)
                f = count(lambda kk, base: (kk == t) & (base + row8 < cand))
                return jnp.where(f < need, cand, jj)
            j_last = lax.fori_loop(0, 13, jb, jnp.zeros((1, tq), I32))

            def drop_tile(kt, carry):
                kk = key_ref[kt]
                key_ref[kt] = jnp.where((kk == t) & (kt * tk + row_tk > j_last), KEY_NEG_INF, kk)
                return carry

            lax.fori_loop(0, n_tiles, drop_tile, 0)

    t_member = jnp.maximum(t, KEY_NEG_INF + 1)

    lane = lax.broadcasted_iota(I32, (tq, LANES), 1)
    first_rows = lax.broadcasted_iota(I32, (LANES, tq), 0) < DH_A
    qa = qa_ref[...]
    half_masks = [jnp.where(lane < DH_A, 1.0, 0.0).astype(BF16), jnp.where(lane < DH_A, 0.0, 1.0).astype(BF16)]
    qhs = [qa[:, (hh // 2) * LANES:(hh // 2 + 1) * LANES] * half_masks[hh % 2] for hh in range(H_A)]
    _flash_init(m_s, l_s, acc_s)

    def att_tile(kt, carry):
        k0 = pl.multiple_of(kt * tk, tk)
        bias = jnp.where(key_ref[kt] >= t_member, 0.0, NEG_BIG)

        def scores(hh):
            return _nt_dot(ka_ref[pl.ds(k0, tk), (hh // 2) * LANES:(hh // 2 + 1) * LANES], qhs[hh]) + bias

        _flash_heads(scores, lambda w: vat_ref[kt, w * LANES:(w + 1) * LANES, :], H_A, first_rows, m_s, l_s, acc_s)
        return carry

    lax.fori_loop(0, n_tiles, att_tile, 0)
    _flash_finish(o_ref, first_rows, l_s, acc_s)


def _dsa_call(qi, wi, qa, ki, ka, vat, *, n_batch, n_q, q_row0, k_rows, tq, tk, n_valid, q_pos0, top_k):
    nq_blocks = n_q // tq
    qb0 = q_row0 // tq
    n_kt = k_rows // tk
    kern = functools.partial(_dsa_kernel, tq=tq, tk=tk, n_valid=n_valid, q_pos0=q_pos0, top_k=top_k)

    def qmap(b, j):
        return (qb0 + b * nq_blocks + j, 0)

    def kmap(b, j):
        return (b, 0)

    return pl.pallas_call(
        kern, grid=(n_batch, nq_blocks),
        in_specs=[pl.BlockSpec((tq, H_I * D_I), qmap), pl.BlockSpec((tq, LANES), qmap), pl.BlockSpec((tq, D_A), qmap),
                  pl.BlockSpec((k_rows, D_I), kmap), pl.BlockSpec((k_rows, D_A), kmap),
                  pl.BlockSpec((n_kt, D_A, tk), lambda b, j: (b, 0, 0))],
        out_specs=pl.BlockSpec((tq, D_A), lambda b, j: (b * nq_blocks + j, 0)),
        out_shape=jax.ShapeDtypeStruct((n_batch * n_q, D_A), BF16),
        scratch_shapes=[pltpu.VMEM((n_kt, tk, tq), I32), pltpu.VMEM((n_kt, tk, tq), I16)] + _flash_scratch(H_A, tq),
        compiler_params=pltpu.CompilerParams(dimension_semantics=("arbitrary", "arbitrary"),
                                             vmem_limit_bytes=VMEM_LIMIT),
        name="dsa",
    )(qi, wi, qa, ki, ka, vat)


def _mla_kernel(qc_ref, kc_ref, vct_ref, o_ref, m_s, l_s, acc_s, *, tq, tk, n_valid, q_pos0):
    j = pl.program_id(1)
    n_tiles, n_full, bound = _key_bounds(j, tq, tk, n_valid, q_pos0)
    row_tk = lax.broadcasted_iota(I32, (tk, tq), 0)
    first_rows = lax.broadcasted_iota(I32, (LANES, tq), 0) < DV_C
    qc = qc_ref[...]
    qhs = [qc[:, hh * QC_SLOT:(hh + 1) * QC_SLOT] for hh in range(H_C)]
    _flash_init(m_s, l_s, acc_s)

    def tile(kt, carry, masked):
        k0 = pl.multiple_of(kt * tk, tk)
        if masked:
            bias = jnp.where(k0 + row_tk < bound, 0.0, NEG_BIG)

        def scores(hh):
            s_t = _nt_dot(kc_ref[pl.ds(k0, tk), hh * QC_SLOT:(hh + 1) * QC_SLOT], qhs[hh])
            return s_t + bias if masked else s_t

        _flash_heads(scores, lambda w: vct_ref[kt, w * LANES:(w + 1) * LANES, :], H_C, first_rows, m_s, l_s, acc_s)
        return carry

    lax.fori_loop(0, n_full, functools.partial(tile, masked=False), 0)
    lax.fori_loop(n_full, n_tiles, functools.partial(tile, masked=True), 0)
    _flash_finish(o_ref, first_rows, l_s, acc_s)


def _mla_call(qc, kc, vct, *, n_batch, n_q, q_row0, k_rows, tq, tk, n_valid, q_pos0):
    nq_blocks = n_q // tq
    qb0 = q_row0 // tq
    n_kt = k_rows // tk
    kern = functools.partial(_mla_kernel, tq=tq, tk=tk, n_valid=n_valid, q_pos0=q_pos0)
    return pl.pallas_call(
        kern, grid=(n_batch, nq_blocks),
        in_specs=[pl.BlockSpec((tq, D_QC), lambda b, j: (qb0 + b * nq_blocks + j, 0)),
                  pl.BlockSpec((k_rows, D_QC), lambda b, j: (b, 0)),
                  pl.BlockSpec((n_kt, D_VC, tk), lambda b, j: (b, 0, 0))],
        out_specs=pl.BlockSpec((tq, D_VC), lambda b, j: (b * nq_blocks + j, 0)),
        out_shape=jax.ShapeDtypeStruct((n_batch * n_q, D_VC), BF16),
        scratch_shapes=_flash_scratch(H_C, tq),
        compiler_params=pltpu.CompilerParams(dimension_semantics=("arbitrary", "arbitrary"),
                                             vmem_limit_bytes=VMEM_LIMIT),
        name="mla",
    )(qc, kc, vct)


def _layer_norm(y, g, b):
    mu = jnp.mean(y, axis=-1, keepdims=True)
    var = jnp.mean(jnp.square(y - mu), axis=-1, keepdims=True)
    return (y - mu) * lax.rsqrt(var + LN_EPS) * g + b


def _outproj_kernel(oap_ref, oas_ref, ob_ref, ocp_ref, ocs_ref, xp_ref, xs_ref, w_ref, g_ref, b_ref, rw_ref, rb_ref,
                    tri_ref, x1_o, idx_o, gate_o, rank_o, cnt_o, run_s, *, n_prompt_tiles):
    @pl.when(pl.program_id(0) == 0)
    def _():
        run_s[...] = jnp.zeros_like(run_s)

    is_prompt = pl.program_id(0) < n_prompt_tiles
    oa = jnp.where(is_prompt, oap_ref[...], oas_ref[...])
    oc = jnp.where(is_prompt, ocp_ref[...], ocs_ref[...])
    x = jnp.where(is_prompt, xp_ref[...], xs_ref[...])
    mix = (jnp.dot(oa, w_ref[0:D_A, :], preferred_element_type=F32)
           + jnp.dot(ob_ref[...], w_ref[D_A:D_A + W_B, :], preferred_element_type=F32)
           + jnp.dot(oc, w_ref[D_A + W_B:, :], preferred_element_type=F32))
    x1 = _layer_norm(ALPHA * x + mix, g_ref[...], b_ref[...])
    x1_o[...] = x1
    lg = _nt_dot(rw_ref[...], x1.astype(BF16)) + rb_ref[...]
    e = lax.broadcasted_iota(I32, lg.shape, 0).astype(F32)
    vals, hots = [], []
    for k in range(TOP_K):
        m = jnp.max(lg, axis=0, keepdims=True)
        idx = jnp.min(jnp.where(lg == m, e, float(N_EXPERTS)), axis=0, keepdims=True)
        vals.append(m)
        hots.append(jnp.where(e == idx, 1.0, 0.0))
        idx_o[k:k + 1, :] = idx.astype(I32)
        lg = jnp.where(e == idx, -jnp.inf, lg)
    ex = [jnp.exp(vv - vals[0]) for vv in vals]
    den = ex[0] + ex[1] + ex[2] + ex[3]
    for k in range(TOP_K):
        gate_o[k:k + 1, :] = ex[k] / den
    per_token = hots[0] + hots[1] + hots[2] + hots[3]
    before = run_s[...] + jnp.dot(per_token.astype(BF16), tri_ref[...], preferred_element_type=F32)
    for k in range(TOP_K):
        rank_o[k:k + 1, :] = jnp.sum(hots[k] * before, axis=0, keepdims=True).astype(I32)
    run_s[...] = run_s[...] + jnp.sum(per_token, axis=1, keepdims=True)
    cnt_o[...] = run_s[...]


def _outproj_call(oa_p, oa_s, ob, oc_p, oc_s, xp, xs, w_out, g, b, rw, rb):
    tm = ROW_TILE
    n_prompt_tiles = xp.shape[0] // tm
    n = xp.shape[0] + tm
    row = lambda i: (i, 0)
    prow = lambda i: (jnp.minimum(i, n_prompt_tiles - 1), 0)
    const2 = lambda i: (0, 0)
    tri = jnp.triu(jnp.ones((tm, tm), BF16), 1)
    col = lambda i: (0, i)
    return pl.pallas_call(
        functools.partial(_outproj_kernel, n_prompt_tiles=n_prompt_tiles), grid=(n // tm,),
        in_specs=[pl.BlockSpec((tm, D_A), prow), pl.BlockSpec((tm, D_A), const2), pl.BlockSpec((tm, W_B), row),
                  pl.BlockSpec((tm, D_VC), prow), pl.BlockSpec((tm, D_VC), const2),
                  pl.BlockSpec((tm, D_MODEL), prow), pl.BlockSpec((tm, D_MODEL), const2),
                  pl.BlockSpec((D_MODEL, D_MODEL), const2),
                  pl.BlockSpec((1, D_MODEL), const2), pl.BlockSpec((1, D_MODEL), const2),
                  pl.BlockSpec((N_EXPERTS, D_MODEL), const2), pl.BlockSpec((N_EXPERTS, 1), const2),
                  pl.BlockSpec((tm, tm), const2)],
        out_specs=[pl.BlockSpec((tm, D_MODEL), row), pl.BlockSpec((TOP_K, tm), col), pl.BlockSpec((TOP_K, tm), col),
                   pl.BlockSpec((TOP_K, tm), col), pl.BlockSpec((N_EXPERTS, 1), const2)],
        out_shape=[jax.ShapeDtypeStruct((n, D_MODEL), F32), jax.ShapeDtypeStruct((TOP_K, n), I32),
                   jax.ShapeDtypeStruct((TOP_K, n), F32), jax.ShapeDtypeStruct((TOP_K, n), I32),
                   jax.ShapeDtypeStruct((N_EXPERTS, 1), F32)],
        scratch_shapes=[pltpu.VMEM((N_EXPERTS, 1), F32)],
        compiler_params=pltpu.CompilerParams(dimension_semantics=("arbitrary",), vmem_limit_bytes=VMEM_LIMIT),
        name="outproj",
    )(oa_p, oa_s, ob, oc_p, oc_s, xp, xs, w_out, g, b, rw, rb, tri)


def _dispatch_kernel(pe_ref, x_ref, dest_ref, xg_hbm, zero_s, sem):
    i = pl.program_id(0)

    @pl.when(i == 0)
    def _():
        zero_s[...] = jnp.zeros_like(zero_s)

        def fill(e):
            start = pl.multiple_of(pe_ref[e] - MOE_TILE, MOE_TILE)
            return pltpu.make_async_copy(zero_s, xg_hbm.at[pl.ds(start, MOE_TILE)], sem)

        def nonempty(e):
            return pe_ref[e] > (pe_ref[e - 1] if e else 0)

        n_blocks = xg_hbm.shape[0] // MOE_TILE
        first_tail = (n_blocks * MOE_TILE - N_EXPERTS * (MOE_TILE - 1)) // MOE_TILE

        def tail(b):
            return pltpu.make_async_copy(zero_s, xg_hbm.at[pl.ds(b * MOE_TILE, MOE_TILE)], sem)

        def unused(b):
            return b * MOE_TILE >= pe_ref[N_EXPERTS - 1]

        for e in range(N_EXPERTS):
            pl.when(nonempty(e))(lambda e=e: fill(e).start())
        for b in range(first_tail, n_blocks):
            pl.when(unused(b))(lambda b=b: tail(b).start())
        for e in range(N_EXPERTS):
            pl.when(nonempty(e))(lambda e=e: fill(e).wait())
        for b in range(first_tail, n_blocks):
            pl.when(unused(b))(lambda b=b: tail(b).wait())

    def row_copy(g, i, k):
        d = dest_ref[0, 0, (g * SUBLANES + i) * TOP_K + k]
        return pltpu.make_async_copy(x_ref.at[g, pl.ds(i, 1)], xg_hbm.at[pl.ds(d, 1)], sem)

    def for_rows(fn):
        def body(g, c):
            for i in range(SUBLANES):
                for k in range(TOP_K):
                    fn(row_copy(g, i, k))
            return c
        lax.fori_loop(0, x_ref.shape[0], body, 0)

    for_rows(lambda cp: cp.start())
    for_rows(lambda cp: cp.wait())


def _dispatch_call(pad_end, x1, dest3, n_pad):
    n = x1.shape[0]
    tm = ROW_TILE
    grid_spec = pltpu.PrefetchScalarGridSpec(
        num_scalar_prefetch=1, grid=(n // tm,),
        in_specs=[pl.BlockSpec((tm // SUBLANES, SUBLANES, D_MODEL), lambda i, pe: (i, 0, 0)),
                  pl.BlockSpec((1, 1, tm * TOP_K), lambda i, pe: (i, 0, 0), memory_space=pltpu.SMEM)],
        out_specs=pl.BlockSpec(memory_space=pl.ANY),
        scratch_shapes=[pltpu.VMEM((MOE_TILE, D_MODEL), F32), pltpu.SemaphoreType.DMA(())])
    return pl.pallas_call(
        _dispatch_kernel, grid_spec=grid_spec,
        out_shape=jax.ShapeDtypeStruct((n_pad, D_MODEL), F32),
        compiler_params=pltpu.CompilerParams(dimension_semantics=("arbitrary",), vmem_limit_bytes=VMEM_LIMIT),
        name="dispatch",
    )(pad_end, x1.reshape(n // SUBLANES, SUBLANES, D_MODEL), dest3)


def _moe_kernel(be_ref, nb_ref, xg_ref, wgu_ref, bgu_ref, wdn_ref, bdn_ref, y_ref, wgu_s, wdn_s):
    i = pl.program_id(0)

    @pl.when(i < nb_ref[0])
    def _():
        e = be_ref[i]
        prev = be_ref[jnp.maximum(i - 1, 0)]

        @pl.when((i == 0) | (e != prev))
        def _():
            wgu_s[...] = wgu_ref[0, 0].astype(BF16)
            wdn_s[...] = wdn_ref[0, 0].astype(BF16)

        h = jnp.dot(xg_ref[...].astype(BF16), wgu_s[...], preferred_element_type=F32) + bgu_ref[0, 0]
        gate = jnp.minimum(h[:, :D_FF], SWIGLU_LIMIT)
        lin = jnp.clip(h[:, D_FF:], -SWIGLU_LIMIT, SWIGLU_LIMIT)
        act = (lin + 1.0) * (gate * jax.nn.sigmoid(SWIGLU_ALPHA * gate))
        y_ref[...] = jnp.dot(act.astype(BF16), wdn_s[...], preferred_element_type=F32) + bdn_ref[0, 0]

    @pl.when(i >= nb_ref[0])
    def _():
        y_ref[...] = jnp.zeros_like(y_ref)


def _moe_call(block_e, n_used, xg, w_gu, b_gu, w_dn, b_dn, layer):
    n_pad = xg.shape[0]
    tm = MOE_TILE

    def used(i, be, nb):
        return (jnp.minimum(i, nb[0] - 1), 0)

    def expert(i, be, nb):
        return (layer, be[i], 0, 0)

    grid_spec = pltpu.PrefetchScalarGridSpec(
        num_scalar_prefetch=2, grid=(n_pad // tm,),
        in_specs=[pl.BlockSpec((tm, D_MODEL), used),
                  pl.BlockSpec((1, 1, D_MODEL, 2 * D_FF), expert), pl.BlockSpec((1, 1, 1, 2 * D_FF), expert),
                  pl.BlockSpec((1, 1, D_FF, D_MODEL), expert), pl.BlockSpec((1, 1, 1, D_MODEL), expert)],
        out_specs=pl.BlockSpec((tm, D_MODEL), lambda i, be, nb: (i, 0)),
        scratch_shapes=[pltpu.VMEM((D_MODEL, 2 * D_FF), BF16), pltpu.VMEM((D_FF, D_MODEL), BF16)])
    return pl.pallas_call(
        _moe_kernel, grid_spec=grid_spec,
        out_shape=jax.ShapeDtypeStruct((n_pad, D_MODEL), F32),
        compiler_params=pltpu.CompilerParams(dimension_semantics=("arbitrary",),
                                             vmem_limit_bytes=56 * 1024 * 1024),
        name="moe",
    )(block_e, n_used, xg, w_gu, b_gu.reshape(DEPTH, N_EXPERTS, 1, 2 * D_FF), w_dn,
      b_dn.reshape(DEPTH, N_EXPERTS, 1, D_MODEL))


def _combine_kernel(x1_ref, gate_ref, dcur_ref, dnext_ref, g_ref, b_ref, y_hbm, op_ref, os_ref, ybuf, sem):
    i = pl.program_id(0)
    tm = x1_ref.shape[0]
    slot = i % 2

    def row_copy(d_ref, s, g, i, k):
        d = d_ref[0, 0, (g * SUBLANES + i) * TOP_K + k]
        return pltpu.make_async_copy(y_hbm.at[pl.ds(d, 1)], ybuf.at[s, k, g, pl.ds(i, 1)], sem.at[s])

    def for_rows(d_ref, s, fn):
        def body(g, c):
            for i in range(SUBLANES):
                for k in range(TOP_K):
                    fn(row_copy(d_ref, s, g, i, k))
            return c
        lax.fori_loop(0, tm // SUBLANES, body, 0)

    @pl.when(i == 0)
    def _():
        for_rows(dcur_ref, 0, lambda cp: cp.start())

    @pl.when(i + 1 < pl.num_programs(0))
    def _():
        for_rows(dnext_ref, 1 - slot, lambda cp: cp.start())

    for_rows(dcur_ref, slot, lambda cp: cp.wait())
    gates = gate_ref[...]
    moe = gates[:, 0:1] * ybuf[slot, 0].reshape(tm, D_MODEL)
    for k in range(1, TOP_K):
        moe = moe + gates[:, k:k + 1] * ybuf[slot, k].reshape(tm, D_MODEL)
    res = _layer_norm(ALPHA * x1_ref[...] + moe, g_ref[...], b_ref[...])
    last = i + 1 == pl.num_programs(0)

    @pl.when(jnp.logical_not(last))
    def _():
        op_ref[...] = res

    @pl.when(last)
    def _():
        os_ref[...] = res


def _combine_call(x1, gates, dest3, g, b, y_rows):
    n = x1.shape[0]
    tm = ROW_TILE
    n_tiles = n // tm
    row = lambda i: (i, 0)
    const2 = lambda i: (0, 0)
    return pl.pallas_call(
        _combine_kernel, grid=(n_tiles,),
        in_specs=[pl.BlockSpec((tm, D_MODEL), row), pl.BlockSpec((tm, TOP_K), row),
                  pl.BlockSpec((1, 1, tm * TOP_K), lambda i: (i, 0, 0), memory_space=pltpu.SMEM),
                  pl.BlockSpec((1, 1, tm * TOP_K), lambda i: (jnp.minimum(i + 1, n_tiles - 1), 0, 0),
                               memory_space=pltpu.SMEM),
                  pl.BlockSpec((1, D_MODEL), const2), pl.BlockSpec((1, D_MODEL), const2),
                  pl.BlockSpec(memory_space=pl.ANY)],
        out_specs=[pl.BlockSpec((tm, D_MODEL), lambda i: (jnp.minimum(i, n_tiles - 2), 0)),
                   pl.BlockSpec((tm, D_MODEL), const2)],
        out_shape=[jax.ShapeDtypeStruct((n - tm, D_MODEL), F32), jax.ShapeDtypeStruct((tm, D_MODEL), F32)],
        scratch_shapes=[pltpu.VMEM((2, TOP_K, tm // SUBLANES, SUBLANES, D_MODEL), F32),
                        pltpu.SemaphoreType.DMA((2,))],
        compiler_params=pltpu.CompilerParams(dimension_semantics=("arbitrary",), vmem_limit_bytes=VMEM_LIMIT),
        name="combine",
    )(x1, gates, dest3, dest3, g, b, y_rows)


def _pack_w_in(w):
    sizes = (D_A, D_A, D_A, H_I * D_I, D_I, H_I, 2 * W_B, D_CQ, D_C, DR_C)
    offs = np.cumsum((0,) + sizes)
    seg = [w[:, offs[k]:offs[k + 1]] for k in range(len(sizes))]
    z = lambda c: jnp.zeros((w.shape[0], c), w.dtype)
    return jnp.concatenate(
        [seg[0], seg[1], seg[2], seg[3], seg[4], seg[5], z(LANES - D_I - H_I), seg[6], seg[7], seg[8],
         seg[9], z(LANES - DR_C)], axis=1).astype(BF16)


def _pack_w_uq(w):
    w3 = w.reshape(D_CQ, H_C, DN_C + DR_C)
    w3 = jnp.pad(w3, ((0, 0), (0, 0), (0, QC_SLOT - DN_C - DR_C)))
    return w3.reshape(D_CQ, D_QC).astype(BF16)


def _pack_w_kc(w_uk):
    wk = jnp.pad(w_uk.reshape(D_C, H_C, DN_C), ((0, 0), (0, 0), (0, QC_SLOT - DN_C))).reshape(D_C, D_QC)
    eye = jnp.pad(jnp.eye(DR_C, dtype=w_uk.dtype), ((0, LANES - DR_C), (DN_C, QC_SLOT - DN_C - DR_C)))
    return jnp.concatenate([wk, jnp.tile(eye, (1, H_C))], axis=0).astype(BF16)


def _rope_tables(pos):
    posf = pos.astype(F32)[:, None]
    n = pos.shape[0]

    def cs(half):
        inv = ROPE_THETA ** (-jnp.arange(half, dtype=F32) / half)
        ang = posf * inv[None, :]
        return jnp.cos(ang), jnp.sin(ang)

    c32, s32 = cs(32)
    z32 = jnp.zeros_like(s32)
    c64 = jnp.tile(jnp.concatenate([c32, c32], 1), (1, 2))
    sa64 = jnp.tile(jnp.concatenate([-s32, z32], 1), (1, 2))
    sb64 = jnp.tile(jnp.concatenate([z32, s32], 1), (1, 2))
    c16, s16 = cs(16)
    z16 = jnp.zeros_like(s16)
    one = lambda c: jnp.ones((n, c), F32)
    zero = lambda c: jnp.zeros((n, c), F32)
    cq = jnp.concatenate([one(DN_C), c16, c16, one(32)], 1)
    saq = jnp.concatenate([zero(DN_C), -s16, z16, zero(32)], 1)
    sbq = jnp.concatenate([zero(DN_C), z16, s16, zero(32)], 1)
    ck = jnp.concatenate([c16, c16, one(96)], 1)
    sak = jnp.concatenate([-s16, z16, zero(96)], 1)
    sbk = jnp.concatenate([z16, s16, zero(96)], 1)
    return [c64, sa64, sb64, cq, saq, sbq, ck, sak, sbk]


def _sgu_tables(w_s, b_s, n_sample_seq, n_sample_batch):
    i = jnp.arange(SGU_CHUNK)
    w_m = jnp.where((i[None, :] // CHUNK) <= (i[:, None] // CHUNK), w_s, 0.0)
    reps = ROW_TILE // SGU_CHUNK
    bd_p = jax.vmap(lambda m: jnp.kron(jnp.eye(reps, dtype=m.dtype), m))(w_m)
    ns = n_sample_seq
    bd_s = jax.vmap(lambda m: jnp.kron(jnp.eye(n_sample_batch, dtype=m.dtype), m[:ns, :ns]))(w_m)
    bd = jnp.stack([bd_p, bd_s]).astype(BF16)
    bias_chunk = jnp.repeat(b_s.T, DG_B, axis=1)
    bias = jnp.stack([jnp.tile(bias_chunk, (reps, 1)), jnp.tile(bias_chunk[:ns], (n_sample_batch, 1))])
    return bd, bias


def _route(top_idx, rank, counts):
    n_assign = top_idx.size
    counts = counts.reshape(N_EXPERTS).astype(I32)
    padded = ((counts + MOE_TILE - 1) // MOE_TILE) * MOE_TILE
    pad_end = jnp.cumsum(padded).astype(I32)
    pad_start = pad_end - padded
    dest = jnp.take(pad_start, top_idx) + rank
    n_blocks = -(-(n_assign + N_EXPERTS * (MOE_TILE - 1)) // MOE_TILE)
    n_used = pad_end[-1] // MOE_TILE
    blk = jnp.minimum(jnp.arange(n_blocks, dtype=I32), n_used - 1) * MOE_TILE
    block_e = jnp.minimum(jnp.sum(pad_end[None, :] <= blk[:, None], axis=1), N_EXPERTS - 1).astype(I32)
    return dest, pad_end, block_e, n_used.reshape(1), n_blocks * MOE_TILE


def _key_tiles_t(v, n_batch, k_rows, tk):
    c = v.shape[1]
    return v.reshape(n_batch, k_rows // tk, tk, c).transpose(0, 1, 3, 2).reshape(n_batch * (k_rows // tk), c, tk)


def kernel(x_prompt, x_sample, cache_a_k, cache_a_v, cache_a_kidx, cache_c_latent, cache_c_krope, w_in, mla_q_norm, mla_kv_norm, w_uq, w_uk, w_uv, sgu_ln_g, sgu_ln_b, w_spatial, b_spatial, w_out, ln1_g, ln1_b, router_w, router_b, w_gate_up, b_gate_up, w_down, b_down, ln2_g, ln2_b):
    batch, seq, _ = x_prompt.shape
    dec_batch, dec_seq, _ = x_sample.shape
    past = cache_a_k.shape[2]
    n_p, n_s = batch * seq, dec_batch * dec_seq
    assert n_p % ROW_TILE == 0 and n_s == ROW_TILE and seq % ROW_TILE == 0
    assert seq % ATT_TK == 0 and seq % ATT_TQ == 0 and dec_seq <= SAMPLE_TQ
    n_all = n_p + n_s
    tab_period = seq // ROW_TILE

    pos_p = jnp.arange(seq, dtype=I32)
    pos_s = past + jnp.arange(dec_seq, dtype=I32)
    tabs = _rope_tables(jnp.concatenate([pos_p, jnp.tile(pos_s, dec_batch)]))

    n_keys_s = past + dec_seq
    l_pad_s = -(-n_keys_s // SAMPLE_TK) * SAMPLE_TK
    top_k_p = min(TOPK_MAX, seq // 4)
    top_k_s = min(TOPK_MAX, n_keys_s // 4)

    xp, xs = x_prompt.reshape(n_p, D_MODEL), x_sample.reshape(n_s, D_MODEL)
    caches_p = [[] for _ in range(5)]
    caches_s = [[] for _ in range(6)]

    def with_cache(cache, new, width):
        allk = jnp.concatenate([cache.astype(BF16), new.reshape(dec_batch, dec_seq, width)], axis=1)
        allk = jnp.pad(allk, ((0, 0), (0, l_pad_s - n_keys_s), (0, 0)))
        return allk.reshape(dec_batch * l_pad_s, width)

    def sample_queries(arr):
        q = arr[n_p:].reshape(dec_batch, dec_seq, -1)
        return jnp.pad(q, ((0, 0), (0, SAMPLE_TQ - dec_seq), (0, 0))).reshape(dec_batch * SAMPLE_TQ, -1)

    def sample_rows(o):
        return o.reshape(dec_batch, SAMPLE_TQ, -1)[:, :dec_seq].reshape(n_s, -1)

    for l in range(DEPTH):
        w_packed = _pack_w_in(w_in[l])
        wuq = _pack_w_uq(w_uq[l])
        wkc = _pack_w_kc(w_uk[l])
        wuv = w_uv[l].astype(BF16)
        bd, bsb = _sgu_tables(w_spatial[l], b_spatial[l], dec_seq, dec_batch)
        (qa, kab, qi, kib, wi, ob, qc, kc,
         ka_p, va_p, ki_p, lat_p, kr_p, vat_p, vct_p,
         ka_s1, va_s1, ki_s1, lat_s1, kr_s1, vb_s1, vab_s, vc_new) = _proj_call(
            xp, xs, w_packed, tabs, mla_q_norm[l].reshape(1, -1), mla_kv_norm[l].reshape(1, -1), wuq, wkc, wuv,
            sgu_ln_g[l].reshape(1, -1), sgu_ln_b[l].reshape(1, -1), bd, bsb, tab_period)

        oa_p = _dsa_call(qi, wi, qa, kib, kab, vat_p, n_batch=batch, n_q=seq, q_row0=0, k_rows=seq,
                         tq=ATT_TQ, tk=ATT_TK, n_valid=seq, q_pos0=0, top_k=top_k_p)
        oc_p = _mla_call(qc, kc, vct_p, n_batch=batch, n_q=seq, q_row0=0, k_rows=seq,
                         tq=MLA_TQ, tk=ATT_TK, n_valid=seq, q_pos0=0)

        kc_c, vc_c = _kvup_call(cache_c_latent[l].reshape(dec_batch * past, D_C),
                                cache_c_krope[l].reshape(dec_batch * past, DR_C), wkc, wuv)
        ki_s = with_cache(cache_a_kidx[l], kib[n_p:], D_I)
        ka_s = with_cache(cache_a_k[l].reshape(dec_batch, past, D_A), kab[n_p:], D_A)
        va_s = with_cache(cache_a_v[l].reshape(dec_batch, past, D_A), vab_s, D_A)
        kc_s = with_cache(kc_c.reshape(dec_batch, past, D_QC), kc[n_p:], D_QC)
        vc_s = with_cache(vc_c.reshape(dec_batch, past, D_VC), vc_new, D_VC)
        oa_s = _dsa_call(sample_queries(qi), sample_queries(wi), sample_queries(qa), ki_s, ka_s,
                         _key_tiles_t(va_s, dec_batch, l_pad_s, SAMPLE_TK),
                         n_batch=dec_batch, n_q=SAMPLE_TQ, q_row0=0, k_rows=l_pad_s,
                         tq=SAMPLE_TQ, tk=SAMPLE_TK, n_valid=n_keys_s, q_pos0=past, top_k=top_k_s)
        oc_s = _mla_call(sample_queries(qc), kc_s, _key_tiles_t(vc_s, dec_batch, l_pad_s, SAMPLE_TK),
                         n_batch=dec_batch, n_q=SAMPLE_TQ, q_row0=0, k_rows=l_pad_s,
                         tq=SAMPLE_TQ, tk=SAMPLE_TK, n_valid=n_keys_s, q_pos0=past)

        x1, top_idx_t, gates_t, rank_t, counts = _outproj_call(
            oa_p, sample_rows(oa_s), ob, oc_p, sample_rows(oc_s), xp, xs, w_out[l].astype(BF16),
            ln1_g[l].reshape(1, -1), ln1_b[l].reshape(1, -1), router_w[l].T.astype(BF16), router_b[l].reshape(-1, 1))
        gates = gates_t.T

        dest, pad_end, block_e, n_used, n_pad = _route(top_idx_t.T, rank_t.T, counts)
        dest3 = dest.reshape(n_all // ROW_TILE, 1, ROW_TILE * TOP_K)
        xg = _dispatch_call(pad_end, x1, dest3, n_pad)
        y_rows = _moe_call(block_e, n_used, xg, w_gate_up, b_gate_up, w_down, b_down, l)
        xp, xs = _combine_call(x1, gates, dest3, ln2_g[l].reshape(1, -1), ln2_b[l].reshape(1, -1), y_rows)

        for dst, arr, shp in ((caches_p[0], ka_p, (H_A, DH_A)), (caches_p[1], va_p, (H_A, DH_A)),
                              (caches_p[2], ki_p, (D_I,)), (caches_p[3], lat_p, (D_C,)), (caches_p[4], kr_p, (DR_C,))):
            dst.append(arr.reshape((batch, seq) + shp))
        for dst, arr, shp in ((caches_s[0], ka_s1, (H_A, DH_A)), (caches_s[1], va_s1, (H_A, DH_A)),
                              (caches_s[2], ki_s1, (D_I,)), (caches_s[3], lat_s1, (D_C,)),
                              (caches_s[4], kr_s1, (DR_C,)), (caches_s[5], vb_s1, (W_B,))):
            dst.append(arr.reshape((dec_batch, dec_seq) + shp))

    y_prompt = xp.reshape(batch, seq, D_MODEL)
    y_sample = xs.reshape(dec_batch, dec_seq, D_MODEL)
    return (y_prompt, y_sample) + tuple(jnp.stack(c) for c in caches_p) + tuple(jnp.stack(c) for c in caches_s)
```

```python
import functools

import numpy as np
import jax
import jax.numpy as jnp
from jax import lax
from jax.experimental import pallas as pl
from jax.experimental.pallas import tpu as pltpu

F32 = jnp.float32
BF16 = jnp.bfloat16
I32 = jnp.int32
I16 = jnp.int16

D_MODEL = 1024
DEPTH = 2
CHUNK = 64
ROPE_THETA = 10000.0
LN_EPS = 1e-5
RMS_EPS = 1e-6
H_A, DH_A = 6, 64
H_I, D_I = 8, 64
TOPK_MAX = 256
G_B, DG_B = 4, 64
W_B = G_B * DG_B
SGU_CHUNK = 128
H_C, DN_C, DR_C, DV_C = 6, 64, 32, 64
D_CQ, D_C = 256, 256
MLA_SCALE = (DN_C + DR_C) ** -0.5
LOG2E = float(np.log2(np.e))
D_A = H_A * DH_A
D_VC = H_C * DV_C
N_EXPERTS = 32
TOP_K = 4
D_FF = 1024
SWIGLU_LIMIT = 7.0
SWIGLU_ALPHA = 1.702
ALPHA = (2 * DEPTH) ** 0.25

LANES = 128
SUBLANES = 8
PACKED_ROWS = 16
HALF_RANGE = 1 << 15
COUNT_CHAINS = 4
LO_STEPS = 4
ROW_TILE = 256
MOE_TILE = 256
ATT_TQ = 512
ATT_TK = 512
MLA_TQ = 512
SAMPLE_TQ = 128
SAMPLE_TK = 384
VMEM_LIMIT = 48 * 1024 * 1024

C_QA, C_KA, C_VA, C_QI, C_KIW, C_ZU, C_ZV, C_CQ, C_CKV, C_KR = (
    0, 384, 768, 1152, 1664, 1792, 2048, 2304, 2560, 2816)
IN_PACKED = 2944
QC_SLOT = 128
D_QC = H_C * QC_SLOT

NEG_BIG = -1e30
KEY_NEG_INF = -2139095041
INT_MIN = -2147483648


def _nt_dot(a, b):
    return lax.dot_general(a, b, (((1,), (1,)), ((), ())), preferred_element_type=F32)


def _rope128(x, c, sa, sb, half):
    return x * c + pltpu.roll(x, LANES - half, 1) * sa + pltpu.roll(x, half, 1) * sb


def _proj_kernel(xp_ref, xs_ref, w_ref, c64_ref, sa64_ref, sb64_ref, cq_ref, saq_ref, sbq_ref,
                 ck_ref, sak_ref, sbk_ref, qn_ref, kvn_ref, wuq_ref, wkc_ref, wuv_ref,
                 sgug_ref, sgub_ref, bd_ref, bsb_ref,
                 qa_o, kab_o, qi_o, kib_o, wi_o, ob_o, qc_o, kc_o,
                 kap_o, vap_o, kip_o, latp_o, krp_o, vat_o, vct_o,
                 kas_o, vas_o, kis_o, lats_o, krs_o, vbs_o, vabs_o, vcs_o, *, n_prompt_tiles):
    is_prompt = pl.program_id(0) < n_prompt_tiles
    x = jnp.where(is_prompt, xp_ref[...], xs_ref[...])
    h = jnp.dot(x.astype(BF16), w_ref[...], preferred_element_type=F32)
    c64, sa64, sb64 = c64_ref[...], sa64_ref[...], sb64_ref[...]

    def rope64(col):
        return _rope128(h[:, col:col + LANES], c64, sa64, sb64, 32)

    kas = []
    for c in range(D_A // LANES):
        qa_o[:, c * LANES:(c + 1) * LANES] = (rope64(C_QA + c * LANES) * (DH_A ** -0.5 * LOG2E)).astype(BF16)
        kas.append(rope64(C_KA + c * LANES))
        kab_o[:, c * LANES:(c + 1) * LANES] = kas[c].astype(BF16)
    va = h[:, C_VA:C_VA + D_A]
    for c in range(H_I * D_I // LANES):
        qi_o[:, c * LANES:(c + 1) * LANES] = (rope64(C_QI + c * LANES) * (D_I ** -0.5)).astype(BF16)
    ki = rope64(C_KIW)[:, :D_I]
    kib_o[...] = ki.astype(BF16)
    wi_o[...] = h[:, C_KIW:C_KIW + LANES] * (H_I ** -0.5)

    z = h[:, C_ZU:C_ZU + 2 * W_B]
    z = 0.5 * z * (1.0 + jnp.tanh(np.sqrt(2.0 / np.pi) * (z + 0.044715 * (z * z * z))))
    u = z[:, :W_B]
    v = z[:, W_B:]
    mu = jnp.mean(v, axis=-1, keepdims=True)
    var = jnp.mean(jnp.square(v - mu), axis=-1, keepdims=True)
    v = (v - mu) * lax.rsqrt(var + LN_EPS) * sgug_ref[...] + sgub_ref[...]
    v16 = v.astype(BF16)
    lane = lax.broadcasted_iota(I32, (v.shape[0], LANES), 1)
    for w in range(W_B // LANES):
        vw = v16[:, w * LANES:(w + 1) * LANES]
        m0 = jnp.dot(bd_ref[0, 2 * w], vw, preferred_element_type=F32)
        m1 = jnp.dot(bd_ref[0, 2 * w + 1], vw, preferred_element_type=F32)
        mixed = jnp.where(lane < DG_B, m0, m1) + bsb_ref[0, :, w * LANES:(w + 1) * LANES]
        ob_o[:, w * LANES:(w + 1) * LANES] = (u[:, w * LANES:(w + 1) * LANES] * mixed).astype(BF16)

    cq = h[:, C_CQ:C_CQ + D_CQ]
    cq = cq * lax.rsqrt(jnp.mean(jnp.square(cq), axis=-1, keepdims=True) + RMS_EPS) * qn_ref[...]
    q = jnp.dot(cq.astype(BF16), wuq_ref[...], preferred_element_type=F32)
    cqt, saq, sbq = cq_ref[...], saq_ref[...], sbq_ref[...]
    for hh in range(H_C):
        qs = _rope128(q[:, hh * QC_SLOT:(hh + 1) * QC_SLOT], cqt, saq, sbq, DR_C // 2)
        qc_o[:, hh * QC_SLOT:(hh + 1) * QC_SLOT] = (qs * (MLA_SCALE * LOG2E)).astype(BF16)
    ckv = h[:, C_CKV:C_CKV + D_C]
    lat = ckv * lax.rsqrt(jnp.mean(jnp.square(ckv), axis=-1, keepdims=True) + RMS_EPS) * kvn_ref[...]
    kr = _rope128(h[:, C_KR:C_KR + LANES], ck_ref[...], sak_ref[...], sbk_ref[...], DR_C // 2)
    lat16 = lat.astype(BF16)
    kc = (jnp.dot(lat16, wkc_ref[:D_C, :], preferred_element_type=F32)
          + jnp.dot(kr.astype(BF16), wkc_ref[D_C:, :], preferred_element_type=F32))
    kc_o[...] = kc.astype(BF16)
    vc = jnp.dot(lat16, wuv_ref[...], preferred_element_type=F32)

    @pl.when(is_prompt)
    def _():
        for c in range(D_A // LANES):
            kap_o[:, c * LANES:(c + 1) * LANES] = kas[c]
            vat_o[0, c * LANES:(c + 1) * LANES, :] = va[:, c * LANES:(c + 1) * LANES].T.astype(BF16)
            vct_o[0, c * LANES:(c + 1) * LANES, :] = vc[:, c * LANES:(c + 1) * LANES].T.astype(BF16)
        vap_o[...] = va
        kip_o[...] = ki
        latp_o[...] = lat
        krp_o[...] = kr[:, :DR_C]

    @pl.when(jnp.logical_not(is_prompt))
    def _():
        for c in range(D_A // LANES):
            kas_o[:, c * LANES:(c + 1) * LANES] = kas[c]
        vas_o[...] = va
        kis_o[...] = ki
        lats_o[...] = lat
        krs_o[...] = kr[:, :DR_C]
        vbs_o[...] = v
        vabs_o[...] = va.astype(BF16)
        vcs_o[...] = vc.astype(BF16)


def _proj_call(xp, xs, w_packed, tabs, qn, kvn, wuq, wkc, wuv, sgug, sgub, bd, bsb, tab_period):
    n_p = xp.shape[0]
    tm = ROW_TILE
    n_prompt_tiles = n_p // tm
    n = n_p + tm
    grid = (n_prompt_tiles + 1,)
    per_kt = ATT_TK // tm

    def row(i):
        return (i, 0)

    def prow(i):
        return (jnp.minimum(i, n_prompt_tiles - 1), 0)

    def ptile(i):
        ip = jnp.minimum(i, n_prompt_tiles - 1)
        return (ip // per_kt, 0, ip % per_kt)

    def const2(i):
        return (0, 0)

    def tab(i):
        return (jnp.where(i < n_prompt_tiles, i % tab_period, tab_period), 0)

    def grp(i):
        return (jnp.where(i < n_prompt_tiles, 0, 1), 0, 0, 0)

    def grp3(i):
        return (jnp.where(i < n_prompt_tiles, 0, 1), 0, 0)

    in_specs = [pl.BlockSpec((tm, D_MODEL), prow), pl.BlockSpec((tm, D_MODEL), const2),
                pl.BlockSpec((D_MODEL, IN_PACKED), const2)]
    in_specs += [pl.BlockSpec((tm, LANES), tab)] * 9
    in_specs += [pl.BlockSpec((1, D_CQ), const2), pl.BlockSpec((1, D_C), const2),
                 pl.BlockSpec((D_CQ, D_QC), const2), pl.BlockSpec((D_C + LANES, D_QC), const2),
                 pl.BlockSpec((D_C, D_VC), const2),
                 pl.BlockSpec((1, W_B), const2), pl.BlockSpec((1, W_B), const2),
                 pl.BlockSpec((1, G_B, tm, tm), grp), pl.BlockSpec((1, tm, W_B), grp3)]
    all_rows = [(D_A, BF16), (D_A, BF16), (H_I * D_I, BF16), (D_I, BF16), (LANES, F32), (W_B, BF16),
                (D_QC, BF16), (D_QC, BF16)]
    caches = [(D_A, F32), (D_A, F32), (D_I, F32), (D_C, F32), (DR_C, F32)]
    sample_only = caches + [(W_B, F32), (D_A, BF16), (D_VC, BF16)]
    out_shape = ([jax.ShapeDtypeStruct((n, c), d) for c, d in all_rows]
                 + [jax.ShapeDtypeStruct((n_p, c), d) for c, d in caches]
                 + [jax.ShapeDtypeStruct((n_p // ATT_TK, c, ATT_TK), BF16) for c in (D_A, D_VC)]
                 + [jax.ShapeDtypeStruct((tm, c), d) for c, d in sample_only])
    out_specs = ([pl.BlockSpec((tm, c), row) for c, _ in all_rows]
                 + [pl.BlockSpec((tm, c), prow) for c, _ in caches]
                 + [pl.BlockSpec((1, c, tm), ptile) for c in (D_A, D_VC)]
                 + [pl.BlockSpec((tm, c), const2) for c, _ in sample_only])
    return pl.pallas_call(
        functools.partial(_proj_kernel, n_prompt_tiles=n_prompt_tiles),
        grid=grid, in_specs=in_specs, out_specs=out_specs, out_shape=out_shape,
        compiler_params=pltpu.CompilerParams(dimension_semantics=("arbitrary",), vmem_limit_bytes=VMEM_LIMIT),
        name="proj",
    )(xp, xs, w_packed, *tabs, qn, kvn, wuq, wkc, wuv, sgug, sgub, bd, bsb)


def _kvup_kernel(lat_ref, kr_ref, wkc_ref, wuv_ref, kc_o, vc_o):
    lat16 = lat_ref[...].astype(BF16)
    kc = (jnp.dot(lat16, wkc_ref[:D_C, :], preferred_element_type=F32)
          + jnp.dot(kr_ref[...].astype(BF16), wkc_ref[D_C:D_C + DR_C, :], preferred_element_type=F32))
    kc_o[...] = kc.astype(BF16)
    vc_o[...] = jnp.dot(lat16, wuv_ref[...], preferred_element_type=F32).astype(BF16)


def _kvup_call(lat, kr, wkc, wuv):
    n = lat.shape[0]
    tm = 1024
    return pl.pallas_call(
        _kvup_kernel, grid=(n // tm,),
        in_specs=[pl.BlockSpec((tm, D_C), lambda i: (i, 0)), pl.BlockSpec((tm, DR_C), lambda i: (i, 0)),
                  pl.BlockSpec((D_C + LANES, D_QC), lambda i: (0, 0)), pl.BlockSpec((D_C, D_VC), lambda i: (0, 0))],
        out_specs=[pl.BlockSpec((tm, D_QC), lambda i: (i, 0)), pl.BlockSpec((tm, D_VC), lambda i: (i, 0))],
        out_shape=[jax.ShapeDtypeStruct((n, D_QC), BF16), jax.ShapeDtypeStruct((n, D_VC), BF16)],
        compiler_params=pltpu.CompilerParams(dimension_semantics=("arbitrary",), vmem_limit_bytes=VMEM_LIMIT),
        name="kvup",
    )(lat, kr, wkc, wuv)


def _key_bounds(j, tq, tk, n_valid, q_pos0):
    q_first = q_pos0 + j * tq
    kmax = jnp.minimum(((q_first + tq - 1) // CHUNK + 1) * CHUNK, n_valid)
    n_tiles = (kmax + tk - 1) // tk
    n_full = jnp.minimum((q_first // CHUNK + 1) * CHUNK, n_valid) // tk
    qpos = q_first + lax.broadcasted_iota(I32, (1, tq), 1)
    bound = jnp.minimum((qpos // CHUNK + 1) * CHUNK, n_valid)
    return n_tiles, n_full, bound


def _flash_init(m_s, l_s, acc_s):
    m_s[...] = jnp.full(m_s.shape, NEG_BIG, F32)
    l_s[...] = jnp.zeros(l_s.shape, F32)
    acc_s[...] = jnp.zeros(acc_s.shape, F32)


def _flash_update(hh, s_t, v_t, m_s, l_s):
    m_old = m_s[hh]
    m_new = jnp.maximum(m_old, jnp.max(s_t, axis=0, keepdims=True))
    p = jnp.exp2(s_t - m_new)
    a = jnp.exp2(m_old - m_new)
    l_s[hh] = a * l_s[hh] + jnp.sum(p, axis=0, keepdims=True)
    m_s[hh] = m_new
    return a, jnp.dot(v_t, p.astype(BF16), preferred_element_type=F32)


def _flash_pair_accumulate(w, first_rows, upd0, upd1, acc_s):
    (a0, pv0), (a1, pv1) = upd0, upd1
    acc_s[w] = acc_s[w] * jnp.where(first_rows, a0, a1) + jnp.where(first_rows, pv0, pv1)


def _flash_heads(scores, values, n_heads, first_rows, m_s, l_s, acc_s):
    s_all = [scores(hh) for hh in range(n_heads)]
    upd = []
    for hh in range(n_heads):
        upd.append(_flash_update(hh, s_all[hh], values(hh // 2), m_s, l_s))
        if hh % 2:
            _flash_pair_accumulate(hh // 2, first_rows, upd[hh - 1], upd[hh], acc_s)


def _flash_finish(o_ref, first_rows, l_s, acc_s):
    for w in range(acc_s.shape[0]):
        l_sel = jnp.where(first_rows, l_s[2 * w], l_s[2 * w + 1])
        o_ref[:, w * LANES:(w + 1) * LANES] = (acc_s[w] / l_sel).T.astype(o_ref.dtype)


def _flash_scratch(n_heads, tq):
    return [pltpu.VMEM((n_heads, 1, tq), F32), pltpu.VMEM((n_heads, 1, tq), F32),
            pltpu.VMEM((n_heads // 2, LANES, tq), F32)]


def _dsa_kernel(qi_ref, wi_ref, qa_ref, ki_ref, ka_ref, vat_ref, o_ref, key_ref, k16_ref, m_s, l_s, acc_s, *,
                tq, tk, n_valid, q_pos0, top_k):
    j = pl.program_id(1)
    n_tiles, _, bound = _key_bounds(j, tq, tk, n_valid, q_pos0)
    row_tk = lax.broadcasted_iota(I32, (tk, tq), 0)
    row8 = lax.broadcasted_iota(I32, (SUBLANES, tq), 0)

    qi = qi_ref[...]
    q_stack = jnp.concatenate([qi[:, hh * D_I:(hh + 1) * D_I] for hh in range(H_I)], axis=0)
    wi_t = wi_ref[...].T
    wis = [wi_t[D_I + hh:D_I + hh + 1, :] for hh in range(H_I)]

    def score_tile(kt, carry):
        k0 = pl.multiple_of(kt * tk, tk)
        d = _nt_dot(ki_ref[pl.ds(k0, tk), :], q_stack)
        s = wis[0] * jnp.maximum(d[:, 0:tq], 0.0)
        for hh in range(1, H_I):
            s = s + wis[hh] * jnp.maximum(d[:, hh * tq:(hh + 1) * tq], 0.0)
        bits = pltpu.bitcast(s, I32)
        key = jnp.where(bits < 0, bits ^ 0x7FFFFFFF, bits)
        key = jnp.where(key == -1, 0, key)
        key = jnp.where(k0 + row_tk < bound, key, KEY_NEG_INF)
        key_ref[kt] = key
        k16_ref[kt] = (key >> 16).astype(I16)
        return carry

    lax.fori_loop(0, n_tiles, score_tile, 0)

    def count(pred):
        def body(kt, acc):
            for r in range(tk // SUBLANES):
                kk = key_ref[kt, r * SUBLANES:(r + 1) * SUBLANES, :]
                acc = acc + jnp.where(pred(kk, kt * tk + r * SUBLANES), 1.0, 0.0)
            return acc
        acc = lax.fori_loop(0, n_tiles, body, jnp.zeros((SUBLANES, tq), F32))
        return jnp.sum(acc, axis=0, keepdims=True)

    def count16_ge(cand):
        c16 = cand.astype(I16)

        def body(kt, accs):
            accs = list(accs)
            for r in range(tk // PACKED_ROWS):
                kk = k16_ref[kt, r * PACKED_ROWS:(r + 1) * PACKED_ROWS, :]
                accs[r % COUNT_CHAINS] = accs[r % COUNT_CHAINS] + jnp.where(kk >= c16, jnp.int16(1), jnp.int16(0))
            return tuple(accs)
        zero = jnp.zeros((PACKED_ROWS, tq), I16)
        accs = lax.fori_loop(0, n_tiles, body, (zero,) * COUNT_CHAINS)
        acc = accs[0].astype(F32)
        for a in accs[1:]:
            acc = acc + a.astype(F32)
        return jnp.sum(acc, axis=0, keepdims=True)

    kf = float(top_k)
    c0 = count16_ge(jnp.zeros((1, tq), I32))
    nonneg = c0 >= kf
    t_hi0 = jnp.where(nonneg, 0, -HALF_RANGE).astype(I32)
    cnt0 = jnp.where(nonneg, c0, (n_tiles * tk).astype(F32))

    def hi_body(it, carry):
        t_hi, cnt_t = carry
        cand = t_hi | lax.shift_left(jnp.int32(1), 14 - it)
        cnt = count16_ge(cand)
        ok = cnt >= kf
        return jnp.where(ok, cand, t_hi), jnp.where(ok, cnt, cnt_t)

    t_hi, cnt_hi = lax.fori_loop(0, 15, hi_body, (t_hi0, cnt0))
    need_lo = kf - count16_ge(t_hi + 1)
    cnt_class = cnt_hi - (kf - need_lo)

    def lo_tile(kt, carry):
        kk = key_ref[kt]
        lo = (kk & 0xFFFF) - HALF_RANGE
        k16_ref[kt] = jnp.where((kk >> 16) == t_hi, lo, -HALF_RANGE).astype(I16)
        return carry

    lax.fori_loop(0, n_tiles, lo_tile, 0)

    def lo_cond(carry):
        it, _, cnt_t = carry
        return (it < 16) & (jnp.max(jnp.abs(cnt_t - need_lo)) > 0.5)

    def lo_body(carry):
        it, t_lo, cnt_t = carry
        for step in range(LO_STEPS):
            cand = t_lo | lax.shift_left(jnp.int32(1), 15 - (it + step))
            cnt = count16_ge(cand - HALF_RANGE)
            ok = cnt >= need_lo
            t_lo, cnt_t = jnp.where(ok, cand, t_lo), jnp.where(ok, cnt, cnt_t)
        return it + LO_STEPS, t_lo, cnt_t

    _, t_lo, cnt_lo = lax.while_loop(lo_cond, lo_body, (jnp.int32(0), jnp.zeros((1, tq), I32), cnt_class))
    t = lax.shift_left(t_hi, 16) | t_lo
    inexact = jnp.max(jnp.abs(cnt_lo - need_lo)) > 0.5

    @pl.when(inexact)
    def _():
        c_gt = count(lambda kk, base: kk > t)
        c_eq = count(lambda kk, base: kk == t)
        need = kf - c_gt

        @pl.when(jnp.max(jnp.where((c_eq > need) & (t > KEY_NEG_INF), 1.0, 0.0)) > 0.5)
        def _():
            def jb(it, jj):
                cand = jj | lax.shift_left(jnp.int32(1), 12 - it)
                f = count(lambda kk, base: (kk == t) & (base + row8 < cand))
                return jnp.where(f < need, cand, jj)
            j_last = lax.fori_loop(0, 13, jb, jnp.zeros((1, tq), I32))

            def drop_tile(kt, carry):
                kk = key_ref[kt]
                key_ref[kt] = jnp.where((kk == t) & (kt * tk + row_tk > j_last), KEY_NEG_INF, kk)
                return carry

            lax.fori_loop(0, n_tiles, drop_tile, 0)

    t_member = jnp.maximum(t, KEY_NEG_INF + 1)

    lane = lax.broadcasted_iota(I32, (tq, LANES), 1)
    first_rows = lax.broadcasted_iota(I32, (LANES, tq), 0) < DH_A
    qa = qa_ref[...]
    half_masks = [jnp.where(lane < DH_A, 1.0, 0.0).astype(BF16), jnp.where(lane < DH_A, 0.0, 1.0).astype(BF16)]
    qhs = [qa[:, (hh // 2) * LANES:(hh // 2 + 1) * LANES] * half_masks[hh % 2] for hh in range(H_A)]
    _flash_init(m_s, l_s, acc_s)

    def att_tile(kt, carry):
        k0 = pl.multiple_of(kt * tk, tk)
        bias = jnp.where(key_ref[kt] >= t_member, 0.0, NEG_BIG)

        def scores(hh):
            return _nt_dot(ka_ref[pl.ds(k0, tk), (hh // 2) * LANES:(hh // 2 + 1) * LANES], qhs[hh]) + bias

        _flash_heads(scores, lambda w: vat_ref[kt, w * LANES:(w + 1) * LANES, :], H_A, first_rows, m_s, l_s, acc_s)
        return carry

    lax.fori_loop(0, n_tiles, att_tile, 0)
    _flash_finish(o_ref, first_rows, l_s, acc_s)


def _dsa_call(qi, wi, qa, ki, ka, vat, *, n_batch, n_q, q_row0, k_rows, tq, tk, n_valid, q_pos0, top_k):
    nq_blocks = n_q // tq
    qb0 = q_row0 // tq
    n_kt = k_rows // tk
    kern = functools.partial(_dsa_kernel, tq=tq, tk=tk, n_valid=n_valid, q_pos0=q_pos0, top_k=top_k)

    def qmap(b, j):
        return (qb0 + b * nq_blocks + j, 0)

    def kmap(b, j):
        return (b, 0)

    return pl.pallas_call(
        kern, grid=(n_batch, nq_blocks),
        in_specs=[pl.BlockSpec((tq, H_I * D_I), qmap), pl.BlockSpec((tq, LANES), qmap), pl.BlockSpec((tq, D_A), qmap),
                  pl.BlockSpec((k_rows, D_I), kmap), pl.BlockSpec((k_rows, D_A), kmap),
                  pl.BlockSpec((n_kt, D_A, tk), lambda b, j: (b, 0, 0))],
        out_specs=pl.BlockSpec((tq, D_A), lambda b, j: (b * nq_blocks + j, 0)),
        out_shape=jax.ShapeDtypeStruct((n_batch * n_q, D_A), BF16),
        scratch_shapes=[pltpu.VMEM((n_kt, tk, tq), I32), pltpu.VMEM((n_kt, tk, tq), I16)] + _flash_scratch(H_A, tq),
        compiler_params=pltpu.CompilerParams(dimension_semantics=("arbitrary", "arbitrary"),
                                             vmem_limit_bytes=VMEM_LIMIT),
        name="dsa",
    )(qi, wi, qa, ki, ka, vat)


def _mla_kernel(qc_ref, kc_ref, vct_ref, o_ref, m_s, l_s, acc_s, *, tq, tk, n_valid, q_pos0):
    j = pl.program_id(1)
    n_tiles, n_full, bound = _key_bounds(j, tq, tk, n_valid, q_pos0)
    row_tk = lax.broadcasted_iota(I32, (tk, tq), 0)
    first_rows = lax.broadcasted_iota(I32, (LANES, tq), 0) < DV_C
    qc = qc_ref[...]
    qhs = [qc[:, hh * QC_SLOT:(hh + 1) * QC_SLOT] for hh in range(H_C)]
    _flash_init(m_s, l_s, acc_s)

    def tile(kt, carry, masked):
        k0 = pl.multiple_of(kt * tk, tk)
        if masked:
            bias = jnp.where(k0 + row_tk < bound, 0.0, NEG_BIG)

        def scores(hh):
            s_t = _nt_dot(kc_ref[pl.ds(k0, tk), hh * QC_SLOT:(hh + 1) * QC_SLOT], qhs[hh])
            return s_t + bias if masked else s_t

        _flash_heads(scores, lambda w: vct_ref[kt, w * LANES:(w + 1) * LANES, :], H_C, first_rows, m_s, l_s, acc_s)
        return carry

    lax.fori_loop(0, n_full, functools.partial(tile, masked=False), 0)
    lax.fori_loop(n_full, n_tiles, functools.partial(tile, masked=True), 0)
    _flash_finish(o_ref, first_rows, l_s, acc_s)


def _mla_call(qc, kc, vct, *, n_batch, n_q, q_row0, k_rows, tq, tk, n_valid, q_pos0):
    nq_blocks = n_q // tq
    qb0 = q_row0 // tq
    n_kt = k_rows // tk
    kern = functools.partial(_mla_kernel, tq=tq, tk=tk, n_valid=n_valid, q_pos0=q_pos0)
    return pl.pallas_call(
        kern, grid=(n_batch, nq_blocks),
        in_specs=[pl.BlockSpec((tq, D_QC), lambda b, j: (qb0 + b * nq_blocks + j, 0)),
                  pl.BlockSpec((k_rows, D_QC), lambda b, j: (b, 0)),
                  pl.BlockSpec((n_kt, D_VC, tk), lambda b, j: (b, 0, 0))],
        out_specs=pl.BlockSpec((tq, D_VC), lambda b, j: (b * nq_blocks + j, 0)),
        out_shape=jax.ShapeDtypeStruct((n_batch * n_q, D_VC), BF16),
        scratch_shapes=_flash_scratch(H_C, tq),
        compiler_params=pltpu.CompilerParams(dimension_semantics=("arbitrary", "arbitrary"),
                                             vmem_limit_bytes=VMEM_LIMIT),
        name="mla",
    )(qc, kc, vct)


def _layer_norm(y, g, b):
    mu = jnp.mean(y, axis=-1, keepdims=True)
    var = jnp.mean(jnp.square(y - mu), axis=-1, keepdims=True)
    return (y - mu) * lax.rsqrt(var + LN_EPS) * g + b


def _outproj_kernel(oap_ref, oas_ref, ob_ref, ocp_ref, ocs_ref, xp_ref, xs_ref, w_ref, g_ref, b_ref, rw_ref, rb_ref,
                    tri_ref, x1_o, idx_o, gate_o, rank_o, cnt_o, run_s, *, n_prompt_tiles):
    @pl.when(pl.program_id(0) == 0)
    def _():
        run_s[...] = jnp.zeros_like(run_s)

    is_prompt = pl.program_id(0) < n_prompt_tiles
    oa = jnp.where(is_prompt, oap_ref[...], oas_ref[...])
    oc = jnp.where(is_prompt, ocp_ref[...], ocs_ref[...])
    x = jnp.where(is_prompt, xp_ref[...], xs_ref[...])
    mix = (jnp.dot(oa, w_ref[0:D_A, :], preferred_element_type=F32)
           + jnp.dot(ob_ref[...], w_ref[D_A:D_A + W_B, :], preferred_element_type=F32)
           + jnp.dot(oc, w_ref[D_A + W_B:, :], preferred_element_type=F32))
    x1 = _layer_norm(ALPHA * x + mix, g_ref[...], b_ref[...])
    x1_o[...] = x1
    lg = _nt_dot(rw_ref[...], x1.astype(BF16)) + rb_ref[...]
    e = lax.broadcasted_iota(I32, lg.shape, 0).astype(F32)
    vals, hots = [], []
    for k in range(TOP_K):
        m = jnp.max(lg, axis=0, keepdims=True)
        idx = jnp.min(jnp.where(lg == m, e, float(N_EXPERTS)), axis=0, keepdims=True)
        vals.append(m)
        hots.append(jnp.where(e == idx, 1.0, 0.0))
        idx_o[k:k + 1, :] = idx.astype(I32)
        lg = jnp.where(e == idx, -jnp.inf, lg)
    ex = [jnp.exp(vv - vals[0]) for vv in vals]
    den = ex[0] + ex[1] + ex[2] + ex[3]
    for k in range(TOP_K):
        gate_o[k:k + 1, :] = ex[k] / den
    per_token = hots[0] + hots[1] + hots[2] + hots[3]
    before = run_s[...] + jnp.dot(per_token.astype(BF16), tri_ref[...], preferred_element_type=F32)
    for k in range(TOP_K):
        rank_o[k:k + 1, :] = jnp.sum(hots[k] * before, axis=0, keepdims=True).astype(I32)
    run_s[...] = run_s[...] + jnp.sum(per_token, axis=1, keepdims=True)
    cnt_o[...] = run_s[...]


def _outproj_call(oa_p, oa_s, ob, oc_p, oc_s, xp, xs, w_out, g, b, rw, rb):
    tm = ROW_TILE
    n_prompt_tiles = xp.shape[0] // tm
    n = xp.shape[0] + tm
    row = lambda i: (i, 0)
    prow = lambda i: (jnp.minimum(i, n_prompt_tiles - 1), 0)
    const2 = lambda i: (0, 0)
    tri = jnp.triu(jnp.ones((tm, tm), BF16), 1)
    col = lambda i: (0, i)
    return pl.pallas_call(
        functools.partial(_outproj_kernel, n_prompt_tiles=n_prompt_tiles), grid=(n // tm,),
        in_specs=[pl.BlockSpec((tm, D_A), prow), pl.BlockSpec((tm, D_A), const2), pl.BlockSpec((tm, W_B), row),
                  pl.BlockSpec((tm, D_VC), prow), pl.BlockSpec((tm, D_VC), const2),
                  pl.BlockSpec((tm, D_MODEL), prow), pl.BlockSpec((tm, D_MODEL), const2),
                  pl.BlockSpec((D_MODEL, D_MODEL), const2),
                  pl.BlockSpec((1, D_MODEL), const2), pl.BlockSpec((1, D_MODEL), const2),
                  pl.BlockSpec((N_EXPERTS, D_MODEL), const2), pl.BlockSpec((N_EXPERTS, 1), const2),
                  pl.BlockSpec((tm, tm), const2)],
        out_specs=[pl.BlockSpec((tm, D_MODEL), row), pl.BlockSpec((TOP_K, tm), col), pl.BlockSpec((TOP_K, tm), col),
                   pl.BlockSpec((TOP_K, tm), col), pl.BlockSpec((N_EXPERTS, 1), const2)],
        out_shape=[jax.ShapeDtypeStruct((n, D_MODEL), F32), jax.ShapeDtypeStruct((TOP_K, n), I32),
                   jax.ShapeDtypeStruct((TOP_K, n), F32), jax.ShapeDtypeStruct((TOP_K, n), I32),
                   jax.ShapeDtypeStruct((N_EXPERTS, 1), F32)],
        scratch_shapes=[pltpu.VMEM((N_EXPERTS, 1), F32)],
        compiler_params=pltpu.CompilerParams(dimension_semantics=("arbitrary",), vmem_limit_bytes=VMEM_LIMIT),
        name="outproj",
    )(oa_p, oa_s, ob, oc_p, oc_s, xp, xs, w_out, g, b, rw, rb, tri)


def _dispatch_kernel(pe_ref, x_ref, dest_ref, xg_hbm, zero_s, sem):
    i = pl.program_id(0)

    @pl.when(i == 0)
    def _():
        zero_s[...] = jnp.zeros_like(zero_s)

        def fill(e):
            start = pl.multiple_of(pe_ref[e] - MOE_TILE, MOE_TILE)
            return pltpu.make_async_copy(zero_s, xg_hbm.at[pl.ds(start, MOE_TILE)], sem)

        def nonempty(e):
            return pe_ref[e] > (pe_ref[e - 1] if e else 0)

        n_blocks = xg_hbm.shape[0] // MOE_TILE
        first_tail = (n_blocks * MOE_TILE - N_EXPERTS * (MOE_TILE - 1)) // MOE_TILE

        def tail(b):
            return pltpu.make_async_copy(zero_s, xg_hbm.at[pl.ds(b * MOE_TILE, MOE_TILE)], sem)

        def unused(b):
            return b * MOE_TILE >= pe_ref[N_EXPERTS - 1]

        for e in range(N_EXPERTS):
            pl.when(nonempty(e))(lambda e=e: fill(e).start())
        for b in range(first_tail, n_blocks):
            pl.when(unused(b))(lambda b=b: tail(b).start())
        for e in range(N_EXPERTS):
            pl.when(nonempty(e))(lambda e=e: fill(e).wait())
        for b in range(first_tail, n_blocks):
            pl.when(unused(b))(lambda b=b: tail(b).wait())

    def row_copy(g, i, k):
        d = dest_ref[0, 0, (g * SUBLANES + i) * TOP_K + k]
        return pltpu.make_async_copy(x_ref.at[g, pl.ds(i, 1)], xg_hbm.at[pl.ds(d, 1)], sem)

    def for_rows(fn):
        def body(g, c):
            for i in range(SUBLANES):
                for k in range(TOP_K):
                    fn(row_copy(g, i, k), k)
            return c
        lax.fori_loop(0, x_ref.shape[0], body, 0)

    for_rows(lambda cp, k: cp.start(priority=k % 2))
    for_rows(lambda cp, k: cp.wait())


def _dispatch_call(pad_end, x1, dest3, n_pad):
    n = x1.shape[0]
    tm = ROW_TILE
    grid_spec = pltpu.PrefetchScalarGridSpec(
        num_scalar_prefetch=1, grid=(n // tm,),
        in_specs=[pl.BlockSpec((tm // SUBLANES, SUBLANES, D_MODEL), lambda i, pe: (i, 0, 0)),
                  pl.BlockSpec((1, 1, tm * TOP_K), lambda i, pe: (i, 0, 0), memory_space=pltpu.SMEM)],
        out_specs=pl.BlockSpec(memory_space=pl.ANY),
        scratch_shapes=[pltpu.VMEM((MOE_TILE, D_MODEL), F32), pltpu.SemaphoreType.DMA(())])
    return pl.pallas_call(
        _dispatch_kernel, grid_spec=grid_spec,
        out_shape=jax.ShapeDtypeStruct((n_pad, D_MODEL), F32),
        compiler_params=pltpu.CompilerParams(dimension_semantics=("arbitrary",), vmem_limit_bytes=VMEM_LIMIT),
        name="dispatch",
    )(pad_end, x1.reshape(n // SUBLANES, SUBLANES, D_MODEL), dest3)


def _moe_kernel(be_ref, nb_ref, xg_ref, wgu_ref, bgu_ref, wdn_ref, bdn_ref, y_ref, wgu_s, wdn_s):
    i = pl.program_id(0)

    @pl.when(i < nb_ref[0])
    def _():
        e = be_ref[i]
        prev = be_ref[jnp.maximum(i - 1, 0)]

        @pl.when((i == 0) | (e != prev))
        def _():
            wgu_s[...] = wgu_ref[0, 0].astype(BF16)
            wdn_s[...] = wdn_ref[0, 0].astype(BF16)

        h = jnp.dot(xg_ref[...].astype(BF16), wgu_s[...], preferred_element_type=F32) + bgu_ref[0, 0]
        gate = jnp.minimum(h[:, :D_FF], SWIGLU_LIMIT)
        lin = jnp.clip(h[:, D_FF:], -SWIGLU_LIMIT, SWIGLU_LIMIT)
        act = (lin + 1.0) * (gate * jax.nn.sigmoid(SWIGLU_ALPHA * gate))
        y_ref[...] = jnp.dot(act.astype(BF16), wdn_s[...], preferred_element_type=F32) + bdn_ref[0, 0]

    @pl.when(i >= nb_ref[0])
    def _():
        y_ref[...] = jnp.zeros_like(y_ref)


def _moe_call(block_e, n_used, xg, w_gu, b_gu, w_dn, b_dn, layer):
    n_pad = xg.shape[0]
    tm = MOE_TILE

    def used(i, be, nb):
        return (jnp.minimum(i, nb[0] - 1), 0)

    def expert(i, be, nb):
        return (layer, be[i], 0, 0)

    grid_spec = pltpu.PrefetchScalarGridSpec(
        num_scalar_prefetch=2, grid=(n_pad // tm,),
        in_specs=[pl.BlockSpec((tm, D_MODEL), used),
                  pl.BlockSpec((1, 1, D_MODEL, 2 * D_FF), expert), pl.BlockSpec((1, 1, 1, 2 * D_FF), expert),
                  pl.BlockSpec((1, 1, D_FF, D_MODEL), expert), pl.BlockSpec((1, 1, 1, D_MODEL), expert)],
        out_specs=pl.BlockSpec((tm, D_MODEL), lambda i, be, nb: (i, 0)),
        scratch_shapes=[pltpu.VMEM((D_MODEL, 2 * D_FF), BF16), pltpu.VMEM((D_FF, D_MODEL), BF16)])
    return pl.pallas_call(
        _moe_kernel, grid_spec=grid_spec,
        out_shape=jax.ShapeDtypeStruct((n_pad, D_MODEL), F32),
        compiler_params=pltpu.CompilerParams(dimension_semantics=("arbitrary",),
                                             vmem_limit_bytes=56 * 1024 * 1024),
        name="moe",
    )(block_e, n_used, xg, w_gu, b_gu.reshape(DEPTH, N_EXPERTS, 1, 2 * D_FF), w_dn,
      b_dn.reshape(DEPTH, N_EXPERTS, 1, D_MODEL))


def _combine_kernel(x1_ref, gate_ref, dcur_ref, dnext_ref, g_ref, b_ref, y_hbm, op_ref, os_ref, ybuf, sem):
    i = pl.program_id(0)
    tm = x1_ref.shape[0]
    slot = i % 2

    def row_copy(d_ref, s, g, i, k):
        d = d_ref[0, 0, (g * SUBLANES + i) * TOP_K + k]
        return pltpu.make_async_copy(y_hbm.at[pl.ds(d, 1)], ybuf.at[s, k, g, pl.ds(i, 1)], sem.at[s])

    def for_rows(d_ref, s, fn):
        def body(g, c):
            for i in range(SUBLANES):
                for k in range(TOP_K):
                    fn(row_copy(d_ref, s, g, i, k), k)
            return c
        lax.fori_loop(0, tm // SUBLANES, body, 0)

    @pl.when(i == 0)
    def _():
        for_rows(dcur_ref, 0, lambda cp, k: cp.start(priority=k % 2))

    @pl.when(i + 1 < pl.num_programs(0))
    def _():
        for_rows(dnext_ref, 1 - slot, lambda cp, k: cp.start(priority=k % 2))

    for_rows(dcur_ref, slot, lambda cp, k: cp.wait())
    gates = gate_ref[...]
    moe = gates[:, 0:1] * ybuf[slot, 0].reshape(tm, D_MODEL)
    for k in range(1, TOP_K):
        moe = moe + gates[:, k:k + 1] * ybuf[slot, k].reshape(tm, D_MODEL)
    res = _layer_norm(ALPHA * x1_ref[...] + moe, g_ref[...], b_ref[...])
    last = i + 1 == pl.num_programs(0)

    @pl.when(jnp.logical_not(last))
    def _():
        op_ref[...] = res

    @pl.when(last)
    def _():
        os_ref[...] = res


def _combine_call(x1, gates, dest3, g, b, y_rows):
    n = x1.shape[0]
    tm = ROW_TILE
    n_tiles = n // tm
    row = lambda i: (i, 0)
    const2 = lambda i: (0, 0)
    return pl.pallas_call(
        _combine_kernel, grid=(n_tiles,),
        in_specs=[pl.BlockSpec((tm, D_MODEL), row), pl.BlockSpec((tm, TOP_K), row),
                  pl.BlockSpec((1, 1, tm * TOP_K), lambda i: (i, 0, 0), memory_space=pltpu.SMEM),
                  pl.BlockSpec((1, 1, tm * TOP_K), lambda i: (jnp.minimum(i + 1, n_tiles - 1), 0, 0),
                               memory_space=pltpu.SMEM),
                  pl.BlockSpec((1, D_MODEL), const2), pl.BlockSpec((1, D_MODEL), const2),
                  pl.BlockSpec(memory_space=pl.ANY)],
        out_specs=[pl.BlockSpec((tm, D_MODEL), lambda i: (jnp.minimum(i, n_tiles - 2), 0)),
                   pl.BlockSpec((tm, D_MODEL), const2)],
        out_shape=[jax.ShapeDtypeStruct((n - tm, D_MODEL), F32), jax.ShapeDtypeStruct((tm, D_MODEL), F32)],
        scratch_shapes=[pltpu.VMEM((2, TOP_K, tm // SUBLANES, SUBLANES, D_MODEL), F32),
                        pltpu.SemaphoreType.DMA((2,))],
        compiler_params=pltpu.CompilerParams(dimension_semantics=("arbitrary",), vmem_limit_bytes=VMEM_LIMIT),
        name="combine",
    )(x1, gates, dest3, dest3, g, b, y_rows)


def _pack_w_in(w):
    sizes = (D_A, D_A, D_A, H_I * D_I, D_I, H_I, 2 * W_B, D_CQ, D_C, DR_C)
    offs = np.cumsum((0,) + sizes)
    seg = [w[:, offs[k]:offs[k + 1]] for k in range(len(sizes))]
    z = lambda c: jnp.zeros((w.shape[0], c), w.dtype)
    return jnp.concatenate(
        [seg[0], seg[1], seg[2], seg[3], seg[4], seg[5], z(LANES - D_I - H_I), seg[6], seg[7], seg[8],
         seg[9], z(LANES - DR_C)], axis=1).astype(BF16)


def _pack_w_uq(w):
    w3 = w.reshape(D_CQ, H_C, DN_C + DR_C)
    w3 = jnp.pad(w3, ((0, 0), (0, 0), (0, QC_SLOT - DN_C - DR_C)))
    return w3.reshape(D_CQ, D_QC).astype(BF16)


def _pack_w_kc(w_uk):
    wk = jnp.pad(w_uk.reshape(D_C, H_C, DN_C), ((0, 0), (0, 0), (0, QC_SLOT - DN_C))).reshape(D_C, D_QC)
    eye = jnp.pad(jnp.eye(DR_C, dtype=w_uk.dtype), ((0, LANES - DR_C), (DN_C, QC_SLOT - DN_C - DR_C)))
    return jnp.concatenate([wk, jnp.tile(eye, (1, H_C))], axis=0).astype(BF16)


def _rope_tables(pos):
    posf = pos.astype(F32)[:, None]
    n = pos.shape[0]

    def cs(half):
        inv = ROPE_THETA ** (-jnp.arange(half, dtype=F32) / half)
        ang = posf * inv[None, :]
        return jnp.cos(ang), jnp.sin(ang)

    c32, s32 = cs(32)
    z32 = jnp.zeros_like(s32)
    c64 = jnp.tile(jnp.concatenate([c32, c32], 1), (1, 2))
    sa64 = jnp.tile(jnp.concatenate([-s32, z32], 1), (1, 2))
    sb64 = jnp.tile(jnp.concatenate([z32, s32], 1), (1, 2))
    c16, s16 = cs(16)
    z16 = jnp.zeros_like(s16)
    one = lambda c: jnp.ones((n, c), F32)
    zero = lambda c: jnp.zeros((n, c), F32)
    cq = jnp.concatenate([one(DN_C), c16, c16, one(32)], 1)
    saq = jnp.concatenate([zero(DN_C), -s16, z16, zero(32)], 1)
    sbq = jnp.concatenate([zero(DN_C), z16, s16, zero(32)], 1)
    ck = jnp.concatenate([c16, c16, one(96)], 1)
    sak = jnp.concatenate([-s16, z16, zero(96)], 1)
    sbk = jnp.concatenate([z16, s16, zero(96)], 1)
    return [c64, sa64, sb64, cq, saq, sbq, ck, sak, sbk]


def _sgu_tables(w_s, b_s, n_sample_seq, n_sample_batch):
    i = jnp.arange(SGU_CHUNK)
    w_m = jnp.where((i[None, :] // CHUNK) <= (i[:, None] // CHUNK), w_s, 0.0)
    reps = ROW_TILE // SGU_CHUNK
    bd_p = jax.vmap(lambda m: jnp.kron(jnp.eye(reps, dtype=m.dtype), m))(w_m)
    ns = n_sample_seq
    bd_s = jax.vmap(lambda m: jnp.kron(jnp.eye(n_sample_batch, dtype=m.dtype), m[:ns, :ns]))(w_m)
    bd = jnp.stack([bd_p, bd_s]).astype(BF16)
    bias_chunk = jnp.repeat(b_s.T, DG_B, axis=1)
    bias = jnp.stack([jnp.tile(bias_chunk, (reps, 1)), jnp.tile(bias_chunk[:ns], (n_sample_batch, 1))])
    return bd, bias


def _route(top_idx, rank, counts):
    n_assign = top_idx.size
    counts = counts.reshape(N_EXPERTS).astype(I32)
    padded = ((counts + MOE_TILE - 1) // MOE_TILE) * MOE_TILE
    pad_end = jnp.cumsum(padded).astype(I32)
    pad_start = pad_end - padded
    dest = jnp.take(pad_start, top_idx) + rank
    n_blocks = -(-(n_assign + N_EXPERTS * (MOE_TILE - 1)) // MOE_TILE)
    n_used = pad_end[-1] // MOE_TILE
    blk = jnp.minimum(jnp.arange(n_blocks, dtype=I32), n_used - 1) * MOE_TILE
    block_e = jnp.minimum(jnp.sum(pad_end[None, :] <= blk[:, None], axis=1), N_EXPERTS - 1).astype(I32)
    return dest, pad_end, block_e, n_used.reshape(1), n_blocks * MOE_TILE


def _key_tiles_t(v, n_batch, k_rows, tk):
    c = v.shape[1]
    return v.reshape(n_batch, k_rows // tk, tk, c).transpose(0, 1, 3, 2).reshape(n_batch * (k_rows // tk), c, tk)


def kernel(x_prompt, x_sample, cache_a_k, cache_a_v, cache_a_kidx, cache_c_latent, cache_c_krope, w_in, mla_q_norm, mla_kv_norm, w_uq, w_uk, w_uv, sgu_ln_g, sgu_ln_b, w_spatial, b_spatial, w_out, ln1_g, ln1_b, router_w, router_b, w_gate_up, b_gate_up, w_down, b_down, ln2_g, ln2_b):
    batch, seq, _ = x_prompt.shape
    dec_batch, dec_seq, _ = x_sample.shape
    past = cache_a_k.shape[2]
    n_p, n_s = batch * seq, dec_batch * dec_seq
    assert n_p % ROW_TILE == 0 and n_s == ROW_TILE and seq % ROW_TILE == 0
    assert seq % ATT_TK == 0 and seq % ATT_TQ == 0 and dec_seq <= SAMPLE_TQ
    n_all = n_p + n_s
    tab_period = seq // ROW_TILE

    pos_p = jnp.arange(seq, dtype=I32)
    pos_s = past + jnp.arange(dec_seq, dtype=I32)
    tabs = _rope_tables(jnp.concatenate([pos_p, jnp.tile(pos_s, dec_batch)]))

    n_keys_s = past + dec_seq
    l_pad_s = -(-n_keys_s // SAMPLE_TK) * SAMPLE_TK
    top_k_p = min(TOPK_MAX, seq // 4)
    top_k_s = min(TOPK_MAX, n_keys_s // 4)

    xp, xs = x_prompt.reshape(n_p, D_MODEL), x_sample.reshape(n_s, D_MODEL)
    caches_p = [[] for _ in range(5)]
    caches_s = [[] for _ in range(6)]

    def with_cache(cache, new, width):
        allk = jnp.concatenate([cache.astype(BF16), new.reshape(dec_batch, dec_seq, width)], axis=1)
        allk = jnp.pad(allk, ((0, 0), (0, l_pad_s - n_keys_s), (0, 0)))
        return allk.reshape(dec_batch * l_pad_s, width)

    def sample_queries(arr):
        q = arr[n_p:].reshape(dec_batch, dec_seq, -1)
        return jnp.pad(q, ((0, 0), (0, SAMPLE_TQ - dec_seq), (0, 0))).reshape(dec_batch * SAMPLE_TQ, -1)

    def sample_rows(o):
        return o.reshape(dec_batch, SAMPLE_TQ, -1)[:, :dec_seq].reshape(n_s, -1)

    for l in range(DEPTH):
        w_packed = _pack_w_in(w_in[l])
        wuq = _pack_w_uq(w_uq[l])
        wkc = _pack_w_kc(w_uk[l])
        wuv = w_uv[l].astype(BF16)
        bd, bsb = _sgu_tables(w_spatial[l], b_spatial[l], dec_seq, dec_batch)
        (qa, kab, qi, kib, wi, ob, qc, kc,
         ka_p, va_p, ki_p, lat_p, kr_p, vat_p, vct_p,
         ka_s1, va_s1, ki_s1, lat_s1, kr_s1, vb_s1, vab_s, vc_new) = _proj_call(
            xp, xs, w_packed, tabs, mla_q_norm[l].reshape(1, -1), mla_kv_norm[l].reshape(1, -1), wuq, wkc, wuv,
            sgu_ln_g[l].reshape(1, -1), sgu_ln_b[l].reshape(1, -1), bd, bsb, tab_period)

        oa_p = _dsa_call(qi, wi, qa, kib, kab, vat_p, n_batch=batch, n_q=seq, q_row0=0, k_rows=seq,
                         tq=ATT_TQ, tk=ATT_TK, n_valid=seq, q_pos0=0, top_k=top_k_p)
        oc_p = _mla_call(qc, kc, vct_p, n_batch=batch, n_q=seq, q_row0=0, k_rows=seq,
                         tq=MLA_TQ, tk=ATT_TK, n_valid=seq, q_pos0=0)

        kc_c, vc_c = _kvup_call(cache_c_latent[l].reshape(dec_batch * past, D_C),
                                cache_c_krope[l].reshape(dec_batch * past, DR_C), wkc, wuv)
        ki_s = with_cache(cache_a_kidx[l], kib[n_p:], D_I)
        ka_s = with_cache(cache_a_k[l].reshape(dec_batch, past, D_A), kab[n_p:], D_A)
        va_s = with_cache(cache_a_v[l].reshape(dec_batch, past, D_A), vab_s, D_A)
        kc_s = with_cache(kc_c.reshape(dec_batch, past, D_QC), kc[n_p:], D_QC)
        vc_s = with_cache(vc_c.reshape(dec_batch, past, D_VC), vc_new, D_VC)
        oa_s = _dsa_call(sample_queries(qi), sample_queries(wi), sample_queries(qa), ki_s, ka_s,
                         _key_tiles_t(va_s, dec_batch, l_pad_s, SAMPLE_TK),
                         n_batch=dec_batch, n_q=SAMPLE_TQ, q_row0=0, k_rows=l_pad_s,
                         tq=SAMPLE_TQ, tk=SAMPLE_TK, n_valid=n_keys_s, q_pos0=past, top_k=top_k_s)
        oc_s = _mla_call(sample_queries(qc), kc_s, _key_tiles_t(vc_s, dec_batch, l_pad_s, SAMPLE_TK),
                         n_batch=dec_batch, n_q=SAMPLE_TQ, q_row0=0, k_rows=l_pad_s,
                         tq=SAMPLE_TQ, tk=SAMPLE_TK, n_valid=n_keys_s, q_pos0=past)

        x1, top_idx_t, gates_t, rank_t, counts = _outproj_call(
            oa_p, sample_rows(oa_s), ob, oc_p, sample_rows(oc_s), xp, xs, w_out[l].astype(BF16),
            ln1_g[l].reshape(1, -1), ln1_b[l].reshape(1, -1), router_w[l].T.astype(BF16), router_b[l].reshape(-1, 1))
        gates = gates_t.T

        dest, pad_end, block_e, n_used, n_pad = _route(top_idx_t.T, rank_t.T, counts)
        dest3 = dest.reshape(n_all // ROW_TILE, 1, ROW_TILE * TOP_K)
        xg = _dispatch_call(pad_end, x1, dest3, n_pad)
        y_rows = _moe_call(block_e, n_used, xg, w_gate_up, b_gate_up, w_down, b_down, l)
        xp, xs = _combine_call(x1, gates, dest3, ln2_g[l].reshape(1, -1), ln2_b[l].reshape(1, -1), y_rows)

        for dst, arr, shp in ((caches_p[0], ka_p, (H_A, DH_A)), (caches_p[1], va_p, (H_A, DH_A)),
                              (caches_p[2], ki_p, (D_I,)), (caches_p[3], lat_p, (D_C,)), (caches_p[4], kr_p, (DR_C,))):
            dst.append(arr.reshape((batch, seq) + shp))
        for dst, arr, shp in ((caches_s[0], ka_s1, (H_A, DH_A)), (caches_s[1], va_s1, (H_A, DH_A)),
                              (caches_s[2], ki_s1, (D_I,)), (caches_s[3], lat_s1, (D_C,)),
                              (caches_s[4], kr_s1, (DR_C,)), (caches_s[5], vb_s1, (W_B,))):
            dst.append(arr.reshape((dec_batch, dec_seq) + shp))

    y_prompt = xp.reshape(batch, seq, D_MODEL)
    y_sample = xs.reshape(dec_batch, dec_seq, D_MODEL)
    return (y_prompt, y_sample) + tuple(jnp.stack(c) for c in caches_p) + tuple(jnp.stack(c) for c in caches_s)
```
